```python
import functools
import jax, jax.numpy as jnp
from jax import lax
import numpy as np

D_MODEL = 2048
BATCH = 8
SEQ = 4096
DEPTH = 1
DEC_BATCH = 16
DEC_SEQ = 16
PAST_LEN = 4096

CHUNK = 64
N_RET_HEADS = 8
RET_DK = 128
RET_DV = 256
RET_QK = N_RET_HEADS * RET_DK
RET_V = N_RET_HEADS * RET_DV
N_ATT_HEADS = 8
ATT_HEAD_DIM = 128
ATT_W = N_ATT_HEADS * ATT_HEAD_DIM
ATT_LEFT_CHUNKS = 8
ATT_REACH = ATT_LEFT_CHUNKS * CHUNK
BAND = ATT_REACH + CHUNK
REL_CLIP = 128
D_FF = 5632
CONV_W = 3
ROPE_BASE = 10000.0
EPS = 1e-6
IN_SIZES = (RET_QK, RET_QK, RET_V, RET_V, ATT_W, ATT_W, ATT_W, D_MODEL, D_MODEL)
IN_WIDTH = 2 * RET_QK + 2 * RET_V + 3 * ATT_W + 2 * D_MODEL

kernel_name = 'hybrid_retention_chunkband_convffn_step'


def rms_norm(x, g=None):
    xf = x.astype(jnp.float32)
    y = (xf * lax.rsqrt(jnp.mean(xf * xf, axis=-1, keepdims=True) + EPS)).astype(x.dtype)
    return y if g is None else y * g


def rotary(x, pos):
    half = x.shape[-1] // 2
    inv = ROPE_BASE ** (-jnp.arange(half, dtype=jnp.float32) / half)
    ang = pos.astype(jnp.float32)[:, None] * inv[None, :]
    cos = jnp.cos(ang)[:, None, :].astype(x.dtype)
    sin = jnp.sin(ang)[:, None, :].astype(x.dtype)
    x1, x2 = x[..., :half], x[..., half:]
    return jnp.concatenate([x1 * cos - x2 * sin, x1 * sin + x2 * cos], axis=-1)


def retention_block(state, q, k, v, log_gamma):
    L = q.shape[1]
    dt = q.dtype
    t = jnp.arange(L, dtype=jnp.float32)
    decay_in = jnp.exp(log_gamma[:, None, None] * jnp.abs(t[:, None] - t[None, :])).astype(dt)
    decay_q = jnp.exp((t[:, None] + 1.0) * log_gamma[None, :]).astype(dt)
    decay_k = jnp.exp((L - 1.0 - t)[:, None] * log_gamma[None, :]).astype(dt)
    decay_s = jnp.exp(L * log_gamma).astype(dt)
    s = jnp.einsum('bnhd,bmhd->bhnm', q, k) * decay_in[None]
    o = (jnp.einsum('bhnm,bmhe->bnhe', s, v)
         + jnp.einsum('bnhd,bhde->bnhe', q, state) * decay_q[None, :, :, None])
    new_state = (decay_s[None, :, None, None] * state
                 + jnp.einsum('bmhd,bmhe->bhde', k * decay_k[None, :, :, None], v))
    return new_state, o


def retention_prompt(q, k, v, log_gamma):
    B, S, H, dk = q.shape
    nc = S // CHUNK

    def to_chunks(t):
        return t.reshape(B, nc, CHUNK, H, t.shape[-1]).swapaxes(0, 1)

    s0 = jnp.zeros((B, H, dk, v.shape[-1]), q.dtype)
    s_fin, o = lax.scan(lambda st, xs: retention_block(st, xs[0], xs[1], xs[2], log_gamma),
                        s0, (to_chunks(q), to_chunks(k), to_chunks(v)))
    return o.swapaxes(0, 1).reshape(B, S, H, v.shape[-1]), s_fin


def retention_sample(q, k, v, state, log_gamma):
    new_state, o = retention_block(state, q, k, v, log_gamma)
    return o, new_state


def band_attention(q, k, v, q_pos, k_pos, k_valid, rel_bias):
    s = jnp.einsum('bqhd,bkhd->bhqk', q, k).astype(jnp.float32) * (q.shape[-1] ** -0.5)
    rel = jnp.clip(k_pos[None, :] - q_pos[:, None], -REL_CLIP, REL_CLIP) + REL_CLIP
    s = s + rel_bias[:, rel].astype(jnp.float32)[None]
    s = jnp.where(k_valid[None, None, None, :], s, -1e30)
    p = jax.nn.softmax(s, axis=-1).astype(v.dtype)
    return jnp.einsum('bhqk,bkhd->bqhd', p, v)


def attention_prompt(q, k, v, rel_bias):
    B, S, H, dh = q.shape
    nc = S // CHUNK
    pad = jnp.zeros((B, ATT_REACH, H, dh), k.dtype)
    kp = jnp.concatenate([pad, k], axis=1)
    vp = jnp.concatenate([pad, v], axis=1)

    def one_chunk(c):
        start = c * CHUNK
        qc = lax.dynamic_slice_in_dim(q, start, CHUNK, axis=1)
        kb = lax.dynamic_slice_in_dim(kp, start, BAND, axis=1)
        vb = lax.dynamic_slice_in_dim(vp, start, BAND, axis=1)
        q_pos = start + jnp.arange(CHUNK)
        k_pos = start - ATT_REACH + jnp.arange(BAND)
        return band_attention(qc, kb, vb, q_pos, k_pos, k_pos >= 0, rel_bias)

    o = lax.map(one_chunk, jnp.arange(nc))
    keep = min(ATT_REACH, S)
    return o.swapaxes(0, 1).reshape(B, S, H, dh), k[:, S - keep:], v[:, S - keep:]


def attention_sample(q, k, v, cache_k, cache_v, rel_bias):
    L = q.shape[1]
    P = cache_k.shape[1]
    kb = jnp.concatenate([cache_k, k], axis=1)
    vb = jnp.concatenate([cache_v, v], axis=1)
    q_pos = PAST_LEN + jnp.arange(L)
    k_pos = jnp.concatenate([PAST_LEN - P + jnp.arange(P), q_pos])
    o = band_attention(q, kb, vb, q_pos, k_pos, k_pos >= PAST_LEN - ATT_REACH, rel_bias)
    return o, k, v


def conv_ffn(u, conv_buf, w_up, conv_w, conv_b, w_down):
    L = u.shape[1]
    up = u @ w_up
    ext = jnp.concatenate([conv_buf.astype(up.dtype), up], axis=1)
    h = conv_b + conv_w[CONV_W - 1] * ext[:, CONV_W - 1:]
    for j in range(CONV_W - 1):
        h = h + conv_w[j] * ext[:, j:j + L]
    value, gate = jnp.split(h, 2, axis=-1)
    out = (jax.nn.gelu(gate, approximate=True) * value) @ w_down
    return out, ext[:, L:]


def layer_step(x, c, pos, ret_mix, att_mix, conv_buf, w_ada, b_ada, g_pre1, w_in, w_br_ret,
               w_br_att, w_out, g_post1, g_pre2, w_up, conv_w, conv_b, w_down, g_post2):
    B, L, _ = x.shape
    mod = (jax.nn.silu(c) @ w_ada + b_ada)[:, None, :]
    sh1, sc1, gt1, sh2, sc2, gt2 = jnp.split(mod, 6, axis=-1)

    u = rms_norm(x, g_pre1) * (1.0 + sc1) + sh1
    proj = u @ w_in
    idx = np.cumsum(np.array(IN_SIZES))[:-1].tolist()
    rq, rk, rv, rg, aq, ak, av, gr, ga = jnp.split(proj, idx, axis=-1)

    rq = rotary(rq.reshape(B, L, N_RET_HEADS, RET_DK), pos)
    rk = rotary(rk.reshape(B, L, N_RET_HEADS, RET_DK), pos) * (RET_DK ** -0.5)
    rv = rv.reshape(B, L, N_RET_HEADS, RET_DV)
    o_ret, ret_state = ret_mix(rq, rk, rv)
    y_ret = (jax.nn.silu(rg) * rms_norm(o_ret).reshape(B, L, RET_V)) @ w_br_ret

    aq = aq.reshape(B, L, N_ATT_HEADS, ATT_HEAD_DIM)
    ak = ak.reshape(B, L, N_ATT_HEADS, ATT_HEAD_DIM)
    av = av.reshape(B, L, N_ATT_HEADS, ATT_HEAD_DIM)
    o_att, k_rows, v_rows = att_mix(aq, ak, av)
    y_att = o_att.reshape(B, L, ATT_W) @ w_br_att

    merged = jax.nn.sigmoid(gr) * y_ret + jax.nn.sigmoid(ga) * y_att
    x = x + gt1 * rms_norm(merged @ w_out, g_post1)

    u2 = rms_norm(x, g_pre2) * (1.0 + sc2) + sh2
    f, conv_state = conv_ffn(u2, conv_buf, w_up, conv_w, conv_b, w_down)
    x = x + gt2 * rms_norm(f, g_post2)
    return x, ret_state, k_rows, v_rows, conv_state


def setup_inputs(seed: int = 0) -> dict:
    key = jax.random.key(seed)
    ks = jax.random.split(key, 24)

    def nrm(k, shape, scale):
        return jax.random.normal(k, shape, jnp.float32) * scale

    P = min(ATT_REACH, PAST_LEN)
    L = DEPTH
    return {
        'x_prompt': nrm(ks[0], (BATCH, SEQ, D_MODEL), 1.0),
        'x_sample': nrm(ks[1], (DEC_BATCH, DEC_SEQ, D_MODEL), 1.0),
        'cache_att_k': nrm(ks[2], (L, DEC_BATCH, P, N_ATT_HEADS, ATT_HEAD_DIM), 1.0),
        'cache_att_v': nrm(ks[3], (L, DEC_BATCH, P, N_ATT_HEADS, ATT_HEAD_DIM), 1.0),
        'state_ret': nrm(ks[4], (L, DEC_BATCH, N_RET_HEADS, RET_DK, RET_DV), 1.0),
        'state_conv': nrm(ks[5], (L, DEC_BATCH, CONV_W - 1, 2 * D_FF), 1.0),
        'c_prompt': nrm(ks[6], (BATCH, D_MODEL), 1.0),
        'c_sample': nrm(ks[7], (DEC_BATCH, D_MODEL), 1.0),
        'w_ada': nrm(ks[8], (L, D_MODEL, 6 * D_MODEL), D_MODEL ** -0.5),
        'b_ada': nrm(ks[9], (L, 6 * D_MODEL), 0.01),
        'g_pre1': 1.0 + nrm(ks[10], (L, D_MODEL), 0.01),
        'w_in': nrm(ks[11], (L, D_MODEL, IN_WIDTH), D_MODEL ** -0.5),
        'rel_bias': nrm(ks[12], (L, N_ATT_HEADS, 2 * REL_CLIP + 1), 0.5),
        'w_br_ret': nrm(ks[13], (L, RET_V, D_MODEL), RET_V ** -0.5),
        'w_br_att': nrm(ks[14], (L, ATT_W, D_MODEL), ATT_W ** -0.5),
        'w_out': nrm(ks[15], (L, D_MODEL, D_MODEL), D_MODEL ** -0.5),
        'g_post1': 1.0 + nrm(ks[16], (L, D_MODEL), 0.01),
        'g_pre2': 1.0 + nrm(ks[17], (L, D_MODEL), 0.01),
        'w_up': nrm(ks[18], (L, D_MODEL, 2 * D_FF), D_MODEL ** -0.5),
        'conv_w': nrm(ks[19], (L, CONV_W, 2 * D_FF), CONV_W ** -0.5),
        'conv_b': nrm(ks[20], (L, 2 * D_FF), 0.01),
        'w_down': nrm(ks[21], (L, D_FF, D_MODEL), D_FF ** -0.5),
        'g_post2': 1.0 + nrm(ks[22], (L, D_MODEL), 0.01),
    }


def reference(x_prompt, x_sample, cache_att_k, cache_att_v, state_ret, state_conv, c_prompt, c_sample,
              w_ada, b_ada, g_pre1, w_in, rel_bias, w_br_ret, w_br_att, w_out, g_post1, g_pre2,
              w_up, conv_w, conv_b, w_down, g_post2):
    log_gamma = jnp.log(1.0 - 2.0 ** (-5.0 - jnp.arange(N_RET_HEADS, dtype=jnp.float32)))
    pos_p = jnp.arange(x_prompt.shape[1])
    pos_s = PAST_LEN + jnp.arange(x_sample.shape[1])
    conv0 = jnp.zeros((x_prompt.shape[0], CONV_W - 1, 2 * D_FF), x_prompt.dtype)
    yp, ys = x_prompt, x_sample
    kp_l, vp_l, rp_l, cp_l, ks_l, vs_l, rs_l, cs_l = [], [], [], [], [], [], [], []
    for l in range(DEPTH):
        lw = (w_ada[l], b_ada[l], g_pre1[l], w_in[l], w_br_ret[l], w_br_att[l], w_out[l], g_post1[l],
              g_pre2[l], w_up[l], conv_w[l], conv_b[l], w_down[l], g_post2[l])
        ret_p = functools.partial(retention_prompt, log_gamma=log_gamma)
        att_p = functools.partial(attention_prompt, rel_bias=rel_bias[l])
        ret_s = functools.partial(retention_sample, state=state_ret[l], log_gamma=log_gamma)
        att_s = functools.partial(attention_sample, cache_k=cache_att_k[l], cache_v=cache_att_v[l],
                                  rel_bias=rel_bias[l])
        yp, r_p, k_p, v_p, cv_p = layer_step(yp, c_prompt, pos_p, ret_p, att_p, conv0, *lw)
        ys, r_s, k_s, v_s, cv_s = layer_step(ys, c_sample, pos_s, ret_s, att_s, state_conv[l], *lw)
        kp_l.append(k_p); vp_l.append(v_p); rp_l.append(r_p); cp_l.append(cv_p)
        ks_l.append(k_s); vs_l.append(v_s); rs_l.append(r_s); cs_l.append(cv_s)
    return (yp, ys, jnp.stack(kp_l), jnp.stack(vp_l), jnp.stack(rp_l), jnp.stack(cp_l),
            jnp.stack(ks_l), jnp.stack(vs_l), jnp.stack(rs_l), jnp.stack(cs_l))
```

```python
import functools

import jax
import jax.numpy as jnp
from jax import lax
from jax.experimental import pallas as pl
from jax.experimental.pallas import tpu as pltpu

F32 = jnp.float32
BF16 = jnp.bfloat16

D_MODEL = 2048
PAST_LEN = 4096
CHUNK = 64
N_HEADS = 8
RET_DK = 128
RET_DV = 256
ATT_DH = 128
ATT_LEFT_CHUNKS = 8
ATT_REACH = ATT_LEFT_CHUNKS * CHUNK
REL_CLIP = 128
D_FF = 5632
CONV_W = 3
ROPE_BASE = 10000.0
EPS = 1e-6
IN_WIDTH = 13312
NEG = -1e30

OFF_RQ, OFF_RK, OFF_RV, OFF_RG = 0, 1024, 2048, 4096
OFF_AQ, OFF_AK, OFF_AV, OFF_GR, OFF_GA = 6144, 7168, 8192, 9216, 11264
QK_W = N_HEADS * RET_DK
RV_W = N_HEADS * RET_DV

LANE = 128
MXU_N = 256
VMEM_LIMIT = 56 * 1024 * 1024

PROJ_TN = 1024
RET_TILE = 256
ATT_TILE = 256
FFN_TN = 512
FFN_SUB = 128


def _params(sem):
    return pltpu.CompilerParams(dimension_semantics=sem, vmem_limit_bytes=VMEM_LIMIT)


def _rms(x):
    return x * lax.rsqrt(jnp.mean(x * x, axis=-1, keepdims=True) + EPS)


def _const_spec(shape):
    zeros = (0,) * len(shape)
    return pl.BlockSpec(shape, lambda *_: zeros, pipeline_mode=pl.Buffered(1))


def _mod_kernel(c_ref, w_ref, b_ref, o_ref):
    a = jax.nn.silu(c_ref[...]).astype(BF16)
    o_ref[...] = jnp.dot(a, w_ref[...].astype(BF16), preferred_element_type=F32) + b_ref[...]


def _mod(c, w_ada, b_ada):
    n, tn = c.shape[0], 1024
    width = w_ada.shape[1]
    return pl.pallas_call(
        _mod_kernel,
        out_shape=jax.ShapeDtypeStruct((n, width), F32),
        grid=(width // tn,),
        in_specs=[pl.BlockSpec((n, D_MODEL), lambda j: (0, 0)),
                  pl.BlockSpec((D_MODEL, tn), lambda j: (0, j)),
                  pl.BlockSpec((1, tn), lambda j: (0, j))],
        out_specs=pl.BlockSpec((n, tn), lambda j: (0, j)),
        compiler_params=_params(("arbitrary",)),
        name="adaln_mod",
    )(c, w_ada, b_ada.reshape(1, width))


def _inproj_kernel(x_ref, sh_ref, sc_ref, g_ref, w_ref, cos_ref, sin_ref, o_ref, u_ref):
    j = pl.program_id(1)

    @pl.when(j == 0)
    def _():
        u = (_rms(x_ref[...]) * g_ref[...]) * (1.0 + sc_ref[...]) + sh_ref[...]
        u_ref[...] = u.astype(BF16)

    def tile(epilogue):
        for c in range(PROJ_TN // MXU_N):
            cols = slice(c * MXU_N, (c + 1) * MXU_N)
            acc = jnp.dot(u_ref[...], w_ref[:, cols], preferred_element_type=F32)
            o_ref[:, cols] = epilogue(acc).astype(BF16)

    def rotary(acc):
        scale = jnp.where(j == OFF_RK // PROJ_TN, RET_DK ** -0.5, 1.0).astype(F32)
        cos, sin = cos_ref[...], sin_ref[...]
        heads = [acc[:, h * LANE:(h + 1) * LANE] for h in range(MXU_N // LANE)]
        return jnp.concatenate(
            [(a * cos + pltpu.roll(a, LANE // 2, 1) * sin) * scale for a in heads], axis=-1)

    pl.when(j < OFF_RV // PROJ_TN)(lambda: tile(rotary))
    pl.when((j >= OFF_RG // PROJ_TN) & (j < OFF_AQ // PROJ_TN))(lambda: tile(jax.nn.silu))
    pl.when(j >= OFF_GR // PROJ_TN)(lambda: tile(jax.nn.sigmoid))
    pl.when(((j >= OFF_RV // PROJ_TN) & (j < OFF_RG // PROJ_TN))
            | ((j >= OFF_AQ // PROJ_TN) & (j < OFF_GR // PROJ_TN)))(lambda: tile(lambda a: a))


def _in_proj(x, mod, g_pre1, w_in, cos, sin, tm):
    B, S, _ = x.shape
    R = mod.shape[1]
    tpb = S // tm
    assert S % tm == 0 and (R == 1 or tpb == 1)
    rm = R if R == 1 else tm
    return pl.pallas_call(
        _inproj_kernel,
        out_shape=jax.ShapeDtypeStruct((B, S, IN_WIDTH), BF16),
        grid=(B * tpb, IN_WIDTH // PROJ_TN),
        in_specs=[
            pl.BlockSpec((None, tm, D_MODEL), lambda i, j: (i // tpb, i % tpb, 0)),
            pl.BlockSpec((None, rm, D_MODEL), lambda i, j: (i // tpb, 0, 0)),
            pl.BlockSpec((None, rm, D_MODEL), lambda i, j: (i // tpb, 0, 1)),
            pl.BlockSpec((1, D_MODEL), lambda i, j: (0, 0)),
            pl.BlockSpec((D_MODEL, PROJ_TN), lambda i, j: (0, j)),
            pl.BlockSpec((tm, LANE), lambda i, j: (i % tpb, 0)),
            pl.BlockSpec((tm, LANE), lambda i, j: (i % tpb, 0)),
        ],
        out_specs=pl.BlockSpec((None, tm, PROJ_TN), lambda i, j: (i // tpb, i % tpb, j)),
        scratch_shapes=[pltpu.VMEM((tm, D_MODEL), BF16)],
        compiler_params=_params(("arbitrary", "arbitrary")),
        name="in_proj",
    )(x, mod, mod, g_pre1.reshape(1, D_MODEL), w_in, cos, sin)


def _retention_kernel(ds_ref, q_ref, k_ref, v_ref, rg_ref, dmat_ref, dq_ref, dk_ref, s0_ref,
                      o_ref, snew_ref, s_ref):
    t = pl.program_id(1)

    @pl.when(t == 0)
    def _():
        s_ref[...] = s0_ref[...]

    def dots(h):
        q = q_ref[:, h * RET_DK:(h + 1) * RET_DK]
        k = k_ref[:, h * RET_DK:(h + 1) * RET_DK]
        v = v_ref[:, h * RET_DV:(h + 1) * RET_DV]
        state = s_ref[h]
        s = lax.dot_general(q, k, (((1,), (1,)), ((), ())), preferred_element_type=F32)
        qs = jnp.dot(q, state.astype(BF16), preferred_element_type=F32)
        kd = (k.astype(F32) * dk_ref[h]).astype(BF16)
        kv = lax.dot_general(kd, v, (((0,), (0,)), ((), ())), preferred_element_type=F32)
        return s, qs, kv, state, v

    def finish(h, s, qs, kv, state, v):
        o = (jnp.dot((s * dmat_ref[h]).astype(BF16), v, preferred_element_type=F32)
             + qs * dq_ref[h])
        new_state = ds_ref[h] * state + kv
        s_ref[h] = new_state
        cols = slice(h * RET_DV, (h + 1) * RET_DV)
        o_ref[:, cols] = (rg_ref[:, cols].astype(F32) * _rms(o)).astype(BF16)

    pending = dots(0)
    for h in range(1, N_HEADS):
        nxt = dots(h)
        finish(h - 1, *pending)
        pending = nxt
    finish(N_HEADS - 1, *pending)

    @pl.when(t == pl.num_programs(1) - 1)
    def _():
        snew_ref[...] = s_ref[...]


def _retention(proj, state0, log_gamma, tile, chunk):
    B, S, _ = proj.shape
    nt = S // tile
    pos = jnp.arange(tile)
    ch = pos // chunk
    dist = (pos[:, None] - pos[None, :]).astype(F32)
    lg = log_gamma[:, None, None]
    same = ch[:, None] == ch[None, :]
    past = ch[None, :] < ch[:, None]
    dmat = jnp.where(same[None], jnp.exp(lg * jnp.abs(dist)[None]),
                     jnp.where(past[None], jnp.exp(lg * dist[None]), 0.0))
    tf = pos.astype(F32)
    dq = jnp.broadcast_to(jnp.exp((tf[None, :] + 1.0) * log_gamma[:, None])[:, :, None],
                          (N_HEADS, tile, RET_DV))
    dk = jnp.broadcast_to(jnp.exp((tile - 1.0 - tf)[None, :] * log_gamma[:, None])[:, :, None],
                          (N_HEADS, tile, RET_DK))
    ds = jnp.exp(tile * log_gamma)
    state_spec = pl.BlockSpec((None, N_HEADS, RET_DK, RET_DV), lambda b, t: (b, 0, 0, 0))
    return pl.pallas_call(
        _retention_kernel,
        out_shape=(jax.ShapeDtypeStruct((B, S, RV_W), BF16),
                   jax.ShapeDtypeStruct((B, N_HEADS, RET_DK, RET_DV), F32)),
        grid=(B, nt),
        in_specs=[
            pl.BlockSpec(memory_space=pltpu.SMEM),
            pl.BlockSpec((None, tile, QK_W), lambda b, t: (b, t, OFF_RQ // QK_W)),
            pl.BlockSpec((None, tile, QK_W), lambda b, t: (b, t, OFF_RK // QK_W)),
            pl.BlockSpec((None, tile, RV_W), lambda b, t: (b, t, OFF_RV // RV_W)),
            pl.BlockSpec((None, tile, RV_W), lambda b, t: (b, t, OFF_RG // RV_W)),
            _const_spec(dmat.shape), _const_spec(dq.shape), _const_spec(dk.shape),
            state_spec,
        ],
        out_specs=(pl.BlockSpec((None, tile, RV_W), lambda b, t: (b, t, 0)), state_spec),
        scratch_shapes=[pltpu.VMEM((N_HEADS, RET_DK, RET_DV), F32)],
        compiler_params=_params(("arbitrary", "arbitrary")),
        name="retention",
    )(ds, proj, proj, proj, proj, dmat, dq, dk, state0)


def _qk(q, k):
    return lax.dot_general(q, k, (((1,), (1,)), ((), ())), preferred_element_type=F32) * (ATT_DH ** -0.5)


def _softmax_pv(scores, values):
    m = functools.reduce(jnp.maximum, [jnp.max(s, axis=-1, keepdims=True) for s in scores])
    ps = [jnp.exp(s - m) for s in scores]
    l = functools.reduce(jnp.add, [jnp.sum(p, axis=-1, keepdims=True) for p in ps])
    o = functools.reduce(jnp.add, [jnp.dot(p.astype(BF16), v, preferred_element_type=F32)
                                   for p, v in zip(ps, values)])
    return o / l


def _per_head_pipelined(scores_fn, finish_fn):
    pending = scores_fn(0)
    for h in range(1, N_HEADS):
        nxt = scores_fn(h)
        finish_fn(h - 1, pending)
        pending = nxt
    finish_fn(N_HEADS - 1, pending)


def _bias_table_kernel(base_ref, o_ref):
    T, W = o_ref.shape
    n = base_ref.shape[-1]
    x = jnp.broadcast_to(base_ref[...], (T, n))
    x = pltpu.roll(x, n - T + 1, 1, stride=1, stride_axis=0)[:, :W]
    qc = lax.broadcasted_iota(jnp.int32, (T, W), 0) // CHUNK
    kc = lax.broadcasted_iota(jnp.int32, (T, W), 1) // CHUNK
    o_ref[...] = jnp.where((kc >= qc) & (kc <= qc + ATT_LEFT_CHUNKS), x, NEG)


def _bias_table(rel_bias):
    T = ATT_TILE
    W = (ATT_REACH // T + 1) * T
    n = 1024
    assert W + T - 1 <= n
    lo = (T - 1) + ATT_REACH - REL_CLIP
    hi = n - lo - (2 * REL_CLIP + 1)
    assert hi >= 0
    base = jnp.concatenate([jnp.broadcast_to(rel_bias[:, :1], (N_HEADS, lo)), rel_bias,
                            jnp.broadcast_to(rel_bias[:, -1:], (N_HEADS, hi))], axis=1)
    return pl.pallas_call(
        _bias_table_kernel,
        out_shape=jax.ShapeDtypeStruct((N_HEADS, T, W), F32),
        grid=(N_HEADS,),
        in_specs=[pl.BlockSpec((None, 1, n), lambda h: (h, 0, 0))],
        out_specs=pl.BlockSpec((None, T, W), lambda h: (h, 0, 0)),
        compiler_params=_params(("arbitrary",)),
        name="bias_table",
    )(base.reshape(N_HEADS, 1, n).astype(F32))


def _att_prompt_kernel(q_ref, k0_ref, k1_ref, k2_ref, v0_ref, v1_ref, v2_ref, tab_ref, o_ref):
    i = pl.program_id(1)
    T = ATT_TILE

    def scores(h):
        cols = slice(h * ATT_DH, (h + 1) * ATT_DH)
        q = q_ref[:, cols]
        s0 = _qk(q, k0_ref[:, cols]) + tab_ref[h, :, 0:T]
        s1 = _qk(q, k1_ref[:, cols]) + tab_ref[h, :, T:2 * T]
        s2 = _qk(q, k2_ref[:, cols]) + tab_ref[h, :, 2 * T:3 * T]
        return [jnp.where(i >= 2, s0, NEG), jnp.where(i >= 1, s1, NEG), s2]

    def finish(h, ss):
        cols = slice(h * ATT_DH, (h + 1) * ATT_DH)
        o = _softmax_pv(ss, [v0_ref[:, cols], v1_ref[:, cols], v2_ref[:, cols]])
        o_ref[:, cols] = o.astype(BF16)

    _per_head_pipelined(scores, finish)


def _attention_prompt(proj, rel_bias):
    B, S, _ = proj.shape
    T = ATT_TILE
    nq = S // T
    assert ATT_REACH // T + 1 == 3
    table = _bias_table(rel_bias)

    def blk(off, back):
        return pl.BlockSpec((None, T, QK_W), lambda b, i: (b, jnp.maximum(i - back, 0), off // QK_W))

    return pl.pallas_call(
        _att_prompt_kernel,
        out_shape=jax.ShapeDtypeStruct((B, S, QK_W), BF16),
        grid=(B, nq),
        in_specs=[blk(OFF_AQ, 0),
                  blk(OFF_AK, 2), blk(OFF_AK, 1), blk(OFF_AK, 0),
                  blk(OFF_AV, 2), blk(OFF_AV, 1), blk(OFF_AV, 0),
                  _const_spec(table.shape)],
        out_specs=pl.BlockSpec((None, T, QK_W), lambda b, i: (b, i, 0)),
        compiler_params=_params(("arbitrary", "arbitrary")),
        name="attention_prompt",
    )(proj, proj, proj, proj, proj, proj, proj, table)


def _att_sample_kernel(q_ref, kn_ref, vn_ref, kc_ref, vc_ref, tab_ref, o_ref):
    P = kc_ref.shape[0]

    def scores(h):
        cols = slice(h * ATT_DH, (h + 1) * ATT_DH)
        q = q_ref[:, cols]
        sc = _qk(q, kc_ref[:, h, :].astype(BF16)) + tab_ref[h, :, 0:P]
        sn = _qk(q, kn_ref[:, cols]) + tab_ref[h, :, P:]
        return [sc, sn]

    def finish(h, ss):
        cols = slice(h * ATT_DH, (h + 1) * ATT_DH)
        o = _softmax_pv(ss, [vc_ref[:, h, :].astype(BF16), vn_ref[:, cols]])
        o_ref[:, cols] = o.astype(BF16)

    _per_head_pipelined(scores, finish)


def _attention_sample(proj, cache_k, cache_v, layer, rel_bias):
    B, L, _ = proj.shape
    P = cache_k.shape[2]
    assert P == ATT_REACH and PAST_LEN >= ATT_REACH
    t = jnp.arange(L)
    rel = jnp.concatenate([jnp.arange(P)[None, :] - P - t[:, None], t[None, :] - t[:, None]], axis=1)
    table = rel_bias[:, jnp.clip(rel, -REL_CLIP, REL_CLIP) + REL_CLIP].astype(F32)
    cache_spec = pl.BlockSpec((None, None, P, N_HEADS, ATT_DH), lambda b: (layer, b, 0, 0, 0))
    return pl.pallas_call(
        _att_sample_kernel,
        out_shape=jax.ShapeDtypeStruct((B, L, QK_W), BF16),
        grid=(B,),
        in_specs=[pl.BlockSpec((None, L, QK_W), lambda b: (b, 0, OFF_AQ // QK_W)),
                  pl.BlockSpec((None, L, QK_W), lambda b: (b, 0, OFF_AK // QK_W)),
                  pl.BlockSpec((None, L, QK_W), lambda b: (b, 0, OFF_AV // QK_W)),
                  cache_spec, cache_spec,
                  _const_spec(table.shape)],
        out_specs=pl.BlockSpec((None, L, QK_W), lambda b: (b, 0, 0)),
        compiler_params=_params(("arbitrary",)),
        name="attention_sample",
    )(proj, proj, proj, cache_k, cache_v, table)


def _merge_kernel(gret_ref, oatt_ref, gr0_ref, gr1_ref, ga0_ref, ga1_ref, x_ref, gt_ref, g_ref,
                  wr_ref, wa_ref, wo_ref, o_ref, m_ref):
    y_ret = jnp.dot(gret_ref[...], wr_ref[...], preferred_element_type=F32)
    y_att = jnp.dot(oatt_ref[...], wa_ref[...], preferred_element_type=F32)
    half = D_MODEL // 2
    m_ref[:, :half] = (gr0_ref[...].astype(F32) * y_ret[:, :half]
                       + ga0_ref[...].astype(F32) * y_att[:, :half]).astype(BF16)
    m_ref[:, half:] = (gr1_ref[...].astype(F32) * y_ret[:, half:]
                       + ga1_ref[...].astype(F32) * y_att[:, half:]).astype(BF16)
    z = jnp.dot(m_ref[...], wo_ref[...], preferred_element_type=F32)
    o_ref[...] = x_ref[...] + gt_ref[...] * (_rms(z) * g_ref[...])


def _merge(x, g_ret, o_att, proj, mod, g_post1, w_br_ret, w_br_att, w_out, tm):
    B, S, _ = x.shape
    R = mod.shape[1]
    tpb = S // tm
    assert S % tm == 0 and (R == 1 or tpb == 1)
    rm = R if R == 1 else tm
    half = D_MODEL // 2
    row = lambda i: (i // tpb, i % tpb)

    def gate_spec(off):
        return pl.BlockSpec((None, tm, half), lambda i: (*row(i), off // half))

    return pl.pallas_call(
        _merge_kernel,
        out_shape=jax.ShapeDtypeStruct((B, S, D_MODEL), F32),
        grid=(B * tpb,),
        in_specs=[pl.BlockSpec((None, tm, RV_W), lambda i: (*row(i), 0)),
                  pl.BlockSpec((None, tm, QK_W), lambda i: (*row(i), 0)),
                  gate_spec(OFF_GR), gate_spec(OFF_GR + half),
                  gate_spec(OFF_GA), gate_spec(OFF_GA + half),
                  pl.BlockSpec((None, tm, D_MODEL), lambda i: (*row(i), 0)),
                  pl.BlockSpec((None, rm, D_MODEL), lambda i: (i // tpb, 0, 2)),
                  pl.BlockSpec((1, D_MODEL), lambda i: (0, 0)),
                  _const_spec(w_br_ret.shape), _const_spec(w_br_att.shape), _const_spec(w_out.shape)],
        out_specs=pl.BlockSpec((None, tm, D_MODEL), lambda i: (*row(i), 0)),
        scratch_shapes=[pltpu.VMEM((tm, D_MODEL), BF16)],
        compiler_params=_params(("arbitrary",)),
        name="merge_out",
    )(g_ret, o_att, proj, proj, proj, proj, x, mod, g_post1.reshape(1, D_MODEL),
      w_br_ret, w_br_att, w_out)


def _interleave_ff(a):
    lead = a.shape[:-1]
    a = a.reshape(*lead, 2, D_FF // FFN_SUB, FFN_SUB)
    return jnp.swapaxes(a, -3, -2).reshape(*lead, 2 * D_FF)


def _deinterleave_ff(a):
    lead = a.shape[:-1]
    a = a.reshape(*lead, D_FF // FFN_SUB, 2, FFN_SUB)
    return jnp.swapaxes(a, -3, -2).reshape(*lead, 2 * D_FF)


def _conv(up_ref, cols, carry_ref, cw_ref, cb_ref, nb):
    L = up_ref.shape[0] // nb
    row = lax.broadcasted_iota(jnp.int32, (L, cols.stop - cols.start), 0)
    w0, w1, w2, b = cw_ref[0:1, cols], cw_ref[1:2, cols], cw_ref[2:3, cols], cb_ref[:, cols]
    outs = []
    for s in range(nb):
        cur = up_ref[s * L:(s + 1) * L, cols]
        h0, h1 = carry_ref[s, 0:1, cols], carry_ref[s, 1:2, cols]
        prev1 = jnp.where(row == 0, h1, pltpu.roll(cur, 1, 0))
        prev2 = jnp.where(row == 0, h0, jnp.where(row == 1, h1, pltpu.roll(cur, 2, 0)))
        outs.append(b + w2 * cur + w1 * prev1 + w0 * prev2)
        carry_ref[s, :, cols] = cur[L - 2:L]
    return outs[0] if nb == 1 else jnp.concatenate(outs, axis=0)


def _ffn_kernel(x_ref, sh_ref, sc_ref, gt_ref, gpre_ref, gpost_ref, wup_ref, cw_ref, cb_ref, wd_ref,
                cin_ref, o_ref, cout_ref, u_ref, upa_ref, upb_ref, carry_ref, h_ref, *, nb, tpb, nj):
    i, j = pl.program_id(0), pl.program_id(1)

    n_sub = FFN_TN // FFN_SUB
    n_chunk = 2
    per = n_sub // n_chunk

    def up_chunk(dst_ref, c):
        w = 2 * FFN_TN // n_chunk
        cols = slice(c * w, (c + 1) * w)
        dst_ref[:, cols] = jnp.dot(u_ref[...], wup_ref[:, cols], preferred_element_type=F32)

    def gate_piece(src_ref, s):
        cols = slice(2 * s * FFN_SUB, (2 * s + 2) * FFN_SUB)
        conv = _conv(src_ref, cols, carry_ref.at[j - 1], cw_ref, cb_ref, nb)
        cout_ref[:, :, cols] = carry_ref[j - 1, :, :, cols]
        value, gate = conv[:, :FFN_SUB], conv[:, FFN_SUB:]
        h_ref[:, s * FFN_SUB:(s + 1) * FFN_SUB] = (
            jax.nn.gelu(gate, approximate=True) * value).astype(BF16)

    def down_phase():
        w = 512
        for c in range(D_MODEL // w):
            o_ref[:, c * w:(c + 1) * w] += jnp.dot(h_ref[...], wd_ref[:, c * w:(c + 1) * w],
                                                   preferred_element_type=F32)

    def up_phase(dst_ref):
        for c in range(n_chunk):
            up_chunk(dst_ref, c)

    def gate_down_phase(src_ref):
        for s in range(n_sub):
            gate_piece(src_ref, s)
        down_phase()

    @pl.when(j == 0)
    def _():
        u = (_rms(x_ref[...]) * gpre_ref[...]) * (1.0 + sc_ref[...]) + sh_ref[...]
        u_ref[...] = u.astype(BF16)
        o_ref[...] = jnp.zeros(o_ref.shape, F32)

        @pl.when(i % tpb == 0)
        def _():
            for jj in range(nj):
                carry_ref[jj] = cin_ref[:, :, jj * 2 * FFN_TN:(jj + 1) * 2 * FFN_TN]

        up_phase(upa_ref)

    def steady(up_prev, up_cur):
        for c in range(n_chunk):
            for s in range(c * per, (c + 1) * per):
                gate_piece(up_prev, s)
            up_chunk(up_cur, c)
        down_phase()

    pl.when((j > 0) & (j < nj) & (j % 2 == 1))(lambda: steady(upa_ref, upb_ref))
    pl.when((j > 0) & (j < nj) & (j % 2 == 0))(lambda: steady(upb_ref, upa_ref))

    @pl.when(j == nj)
    def _():
        gate_down_phase(upa_ref if (nj - 1) % 2 == 0 else upb_ref)
        o_ref[...] = x_ref[...] + gt_ref[...] * (_rms(o_ref[...]) * gpost_ref[...])


def _ffn(x, mod, conv_in, g_pre2, g_post2, w_up_il, conv_w_il, conv_b_il, w_down, tm, nb):
    B, S, _ = x.shape
    R = mod.shape[1]
    tpb = S // tm
    assert S % tm == 0 and (R == 1 or tpb == 1) and (nb == 1 or tpb == 1)
    rm = R if R == 1 else tm
    tn = FFN_TN
    nj = D_FF // tn
    row = lambda i: (i // tpb, i % tpb)
    upj = lambda j: jnp.minimum(j, nj - 1)
    dnj = lambda j: jnp.maximum(j - 1, 0)
    return pl.pallas_call(
        functools.partial(_ffn_kernel, nb=nb, tpb=tpb, nj=nj),
        out_shape=(jax.ShapeDtypeStruct((B, S, D_MODEL), F32),
                   jax.ShapeDtypeStruct((B * tpb * nb, CONV_W - 1, 2 * D_FF), F32)),
        grid=(B * tpb, nj + 1),
        in_specs=[pl.BlockSpec((None, tm, D_MODEL), lambda i, j: (*row(i), 0)),
                  pl.BlockSpec((None, rm, D_MODEL), lambda i, j: (i // tpb, 0, 3)),
                  pl.BlockSpec((None, rm, D_MODEL), lambda i, j: (i // tpb, 0, 4)),
                  pl.BlockSpec((None, rm, D_MODEL), lambda i, j: (i // tpb, 0, 5)),
                  pl.BlockSpec((1, D_MODEL), lambda i, j: (0, 0)),
                  pl.BlockSpec((1, D_MODEL), lambda i, j: (0, 0)),
                  pl.BlockSpec((D_MODEL, 2 * tn), lambda i, j: (0, upj(j))),
                  pl.BlockSpec((CONV_W, 2 * tn), lambda i, j: (0, dnj(j))),
                  pl.BlockSpec((1, 2 * tn), lambda i, j: (0, dnj(j))),
                  pl.BlockSpec((tn, D_MODEL), lambda i, j: (dnj(j), 0)),
                  pl.BlockSpec((nb, CONV_W - 1, 2 * D_FF), lambda i, j: (i // tpb, 0, 0))],
        out_specs=(pl.BlockSpec((None, tm, D_MODEL), lambda i, j: (*row(i), 0)),
                   pl.BlockSpec((nb, CONV_W - 1, 2 * tn), lambda i, j: (i, 0, dnj(j)))),
        scratch_shapes=[pltpu.VMEM((tm, D_MODEL), BF16),
                        pltpu.VMEM((tm, 2 * tn), F32),
                        pltpu.VMEM((tm, 2 * tn), F32),
                        pltpu.VMEM((nj, nb, CONV_W - 1, 2 * tn), F32),
                        pltpu.VMEM((tm, tn), BF16)],
        compiler_params=_params(("arbitrary", "arbitrary")),
        name="conv_ffn",
    )(x, mod, mod, mod, g_pre2.reshape(1, D_MODEL), g_post2.reshape(1, D_MODEL), w_up_il,
      conv_w_il, conv_b_il.reshape(1, 2 * D_FF), w_down, conv_in)


def _rotary_tables(pos):
    half = RET_DK // 2
    inv = ROPE_BASE ** (-jnp.arange(half, dtype=F32) / half)
    ang = pos.astype(F32)[:, None] * inv[None, :]
    cos, sin = jnp.cos(ang), jnp.sin(ang)
    return jnp.concatenate([cos, cos], axis=-1), jnp.concatenate([-sin, sin], axis=-1)


def kernel(x_prompt, x_sample, cache_att_k, cache_att_v, state_ret, state_conv, c_prompt, c_sample,
           w_ada, b_ada, g_pre1, w_in, rel_bias, w_br_ret, w_br_att, w_out, g_post1, g_pre2,
           w_up, conv_w, conv_b, w_down, g_post2):
    depth = w_ada.shape[0]
    Bp, Sp, _ = x_prompt.shape
    Bs, Ls, _ = x_sample.shape
    log_gamma = jnp.log(1.0 - 2.0 ** (-5.0 - jnp.arange(N_HEADS, dtype=F32)))
    cos_p, sin_p = _rotary_tables(jnp.arange(Sp))
    cos_s, sin_s = _rotary_tables(PAST_LEN + jnp.arange(Ls))
    cos_s, sin_s = jnp.tile(cos_s, (Bs, 1)), jnp.tile(sin_s, (Bs, 1))
    keep = min(ATT_REACH, Sp)
    tm_p = 512

    yp = x_prompt
    ys = x_sample.reshape(1, Bs * Ls, D_MODEL)
    outs = [[] for _ in range(8)]
    for l in range(depth):
        w_in_b, w_down_b = w_in[l].astype(BF16), w_down[l].astype(BF16)
        w_up_b = _interleave_ff(w_up[l]).astype(BF16)
        conv_w_il, conv_b_il = _interleave_ff(conv_w[l]), _interleave_ff(conv_b[l])
        w_br_ret_b, w_br_att_b, w_out_b = (w_br_ret[l].astype(BF16), w_br_att[l].astype(BF16),
                                           w_out[l].astype(BF16))
        mod = _mod(jnp.concatenate([c_prompt, c_sample], axis=0), w_ada[l], b_ada[l])
        mod_p = mod[:Bp].reshape(Bp, 1, 6 * D_MODEL)
        mod_s = jnp.repeat(mod[Bp:], Ls, axis=0).reshape(1, Bs * Ls, 6 * D_MODEL)

        proj = _in_proj(yp, mod_p, g_pre1[l], w_in_b, cos_p, sin_p, tm=tm_p)
        zero_state = jnp.zeros((Bp, N_HEADS, RET_DK, RET_DV), F32)
        g_ret, r_p = _retention(proj, zero_state, log_gamma, RET_TILE, CHUNK)
        o_att = _attention_prompt(proj, rel_bias[l])
        x1 = _merge(yp, g_ret, o_att, proj, mod_p, g_post1[l], w_br_ret_b, w_br_att_b, w_out_b, tm=tm_p)
        conv0 = jnp.zeros((Bp, CONV_W - 1, 2 * D_FF), F32)
        yp, cv_p = _ffn(x1, mod_p, conv0, g_pre2[l], g_post2[l], w_up_b, conv_w_il, conv_b_il,
                        w_down_b, tm=tm_p, nb=1)
        cv_p = _deinterleave_ff(cv_p.reshape(Bp, Sp // tm_p, CONV_W - 1, 2 * D_FF)[:, -1])
        k_p = proj[:, Sp - keep:, OFF_AK:OFF_AV].astype(F32).reshape(Bp, keep, N_HEADS, ATT_DH)
        v_p = proj[:, Sp - keep:, OFF_AV:OFF_GR].astype(F32).reshape(Bp, keep, N_HEADS, ATT_DH)

        proj_s = _in_proj(ys, mod_s, g_pre1[l], w_in_b, cos_s, sin_s, tm=Bs * Ls)
        proj_s3 = proj_s.reshape(Bs, Ls, IN_WIDTH)
        g_ret_s, r_s = _retention(proj_s3, state_ret[l], log_gamma, Ls, Ls)
        o_att_s = _attention_sample(proj_s3, cache_att_k, cache_att_v, l, rel_bias[l])
        x1_s = _merge(ys, g_ret_s.reshape(1, Bs * Ls, -1), o_att_s.reshape(1, Bs * Ls, -1), proj_s,
                      mod_s, g_post1[l], w_br_ret_b, w_br_att_b, w_out_b, tm=Bs * Ls)
        ys, cv_s = _ffn(x1_s, mod_s, _interleave_ff(state_conv[l]), g_pre2[l], g_post2[l], w_up_b,
                        conv_w_il, conv_b_il, w_down_b, tm=Bs * Ls, nb=Bs)
        cv_s = _deinterleave_ff(cv_s)
        k_s = proj_s3[:, :, OFF_AK:OFF_AV].astype(F32).reshape(Bs, Ls, N_HEADS, ATT_DH)
        v_s = proj_s3[:, :, OFF_AV:OFF_GR].astype(F32).reshape(Bs, Ls, N_HEADS, ATT_DH)

        for lst, val in zip(outs, (k_p, v_p, r_p, cv_p, k_s, v_s, r_s, cv_s)):
            lst.append(val)
    return (yp, ys.reshape(Bs, Ls, D_MODEL), *[jnp.stack(o) for o in outs])
```

```python
import functools

import jax
import jax.numpy as jnp
from jax import lax
from jax.experimental import pallas as pl
from jax.experimental.pallas import tpu as pltpu

F32 = jnp.float32
BF16 = jnp.bfloat16

D_MODEL = 2048
PAST_LEN = 4096
CHUNK = 64
N_HEADS = 8
RET_DK = 128
RET_DV = 256
ATT_DH = 128
ATT_LEFT_CHUNKS = 8
ATT_REACH = ATT_LEFT_CHUNKS * CHUNK
REL_CLIP = 128
D_FF = 5632
CONV_W = 3
ROPE_BASE = 10000.0
EPS = 1e-6
IN_WIDTH = 13312
NEG = -1e30

OFF_RQ, OFF_RK, OFF_RV, OFF_RG = 0, 1024, 2048, 4096
OFF_AQ, OFF_AK, OFF_AV, OFF_GR, OFF_GA = 6144, 7168, 8192, 9216, 11264
QK_W = N_HEADS * RET_DK
RV_W = N_HEADS * RET_DV

LANE = 128
MXU_N = 256
VMEM_LIMIT = 56 * 1024 * 1024

PROJ_TN = 1024
RET_TILE = 256
ATT_TILE = 256
FFN_TN = 512
FFN_SUB = 128


def _params(sem):
    return pltpu.CompilerParams(dimension_semantics=sem, vmem_limit_bytes=VMEM_LIMIT)


def _rms(x):
    return x * lax.rsqrt(jnp.mean(x * x, axis=-1, keepdims=True) + EPS)


def _const_spec(shape):
    zeros = (0,) * len(shape)
    return pl.BlockSpec(shape, lambda *_: zeros, pipeline_mode=pl.Buffered(1))


def _mod_kernel(c_ref, w_ref, b_ref, o_ref):
    a = jax.nn.silu(c_ref[...]).astype(BF16)
    o_ref[...] = jnp.dot(a, w_ref[...].astype(BF16), preferred_element_type=F32) + b_ref[...]


def _mod(c, w_ada, b_ada):
    n, tn = c.shape[0], 1024
    width = w_ada.shape[1]
    return pl.pallas_call(
        _mod_kernel,
        out_shape=jax.ShapeDtypeStruct((n, width), F32),
        grid=(width // tn,),
        in_specs=[pl.BlockSpec((n, D_MODEL), lambda j: (0, 0)),
                  pl.BlockSpec((D_MODEL, tn), lambda j: (0, j)),
                  pl.BlockSpec((1, tn), lambda j: (0, j))],
        out_specs=pl.BlockSpec((n, tn), lambda j: (0, j)),
        compiler_params=_params(("arbitrary",)),
        name="adaln_mod",
    )(c, w_ada, b_ada.reshape(1, width))


def _inproj_kernel(x_ref, sh_ref, sc_ref, g_ref, w_ref, cos_ref, sin_ref, o_ref, u_ref):
    j = pl.program_id(1)

    @pl.when(j == 0)
    def _():
        u = (_rms(x_ref[...]) * g_ref[...]) * (1.0 + sc_ref[...]) + sh_ref[...]
        u_ref[...] = u.astype(BF16)

    def tile(epilogue):
        for c in range(PROJ_TN // MXU_N):
            cols = slice(c * MXU_N, (c + 1) * MXU_N)
            acc = jnp.dot(u_ref[...], w_ref[:, cols], preferred_element_type=F32)
            o_ref[:, cols] = epilogue(acc).astype(BF16)

    def rotary(acc):
        scale = jnp.where(j == OFF_RK // PROJ_TN, RET_DK ** -0.5, 1.0).astype(F32)
        cos, sin = cos_ref[...], sin_ref[...]
        heads = [acc[:, h * LANE:(h + 1) * LANE] for h in range(MXU_N // LANE)]
        return jnp.concatenate(
            [(a * cos + pltpu.roll(a, LANE // 2, 1) * sin) * scale for a in heads], axis=-1)

    pl.when(j < OFF_RV // PROJ_TN)(lambda: tile(rotary))
    pl.when((j >= OFF_RG // PROJ_TN) & (j < OFF_AQ // PROJ_TN))(lambda: tile(jax.nn.silu))
    pl.when(j >= OFF_GR // PROJ_TN)(lambda: tile(jax.nn.sigmoid))
    pl.when(((j >= OFF_RV // PROJ_TN) & (j < OFF_RG // PROJ_TN))
            | ((j >= OFF_AQ // PROJ_TN) & (j < OFF_GR // PROJ_TN)))(lambda: tile(lambda a: a))


def _in_proj(x, mod, g_pre1, w_in, cos, sin, tm):
    B, S, _ = x.shape
    R = mod.shape[1]
    tpb = S // tm
    assert S % tm == 0 and (R == 1 or tpb == 1)
    rm = R if R == 1 else tm
    return pl.pallas_call(
        _inproj_kernel,
        out_shape=jax.ShapeDtypeStruct((B, S, IN_WIDTH), BF16),
        grid=(B * tpb, IN_WIDTH // PROJ_TN),
        in_specs=[
            pl.BlockSpec((None, tm, D_MODEL), lambda i, j: (i // tpb, i % tpb, 0)),
            pl.BlockSpec((None, rm, D_MODEL), lambda i, j: (i // tpb, 0, 0)),
            pl.BlockSpec((None, rm, D_MODEL), lambda i, j: (i // tpb, 0, 1)),
            pl.BlockSpec((1, D_MODEL), lambda i, j: (0, 0)),
            pl.BlockSpec((D_MODEL, PROJ_TN), lambda i, j: (0, j)),
            pl.BlockSpec((tm, LANE), lambda i, j: (i % tpb, 0)),
            pl.BlockSpec((tm, LANE), lambda i, j: (i % tpb, 0)),
        ],
        out_specs=pl.BlockSpec((None, tm, PROJ_TN), lambda i, j: (i // tpb, i % tpb, j)),
        scratch_shapes=[pltpu.VMEM((tm, D_MODEL), BF16)],
        compiler_params=_params(("arbitrary", "arbitrary")),
        name="in_proj",
    )(x, mod, mod, g_pre1.reshape(1, D_MODEL), w_in, cos, sin)


def _retention_kernel(ds_ref, q_ref, k_ref, v_ref, rg_ref, dmat_ref, dq_ref, dk_ref, s0_ref,
                      o_ref, snew_ref, s_ref):
    t = pl.program_id(1)

    @pl.when(t == 0)
    def _():
        s_ref[...] = s0_ref[...]

    def dots(h):
        q = q_ref[:, h * RET_DK:(h + 1) * RET_DK]
        k = k_ref[:, h * RET_DK:(h + 1) * RET_DK]
        v = v_ref[:, h * RET_DV:(h + 1) * RET_DV]
        state = s_ref[h]
        s = lax.dot_general(q, k, (((1,), (1,)), ((), ())), preferred_element_type=F32)
        qs = jnp.dot(q, state.astype(BF16), preferred_element_type=F32)
        kd = (k.astype(F32) * dk_ref[h]).astype(BF16)
        kv = lax.dot_general(kd, v, (((0,), (0,)), ((), ())), preferred_element_type=F32)
        return s, qs, kv, state, v

    def finish(h, s, qs, kv, state, v):
        o = (jnp.dot((s * dmat_ref[h]).astype(BF16), v, preferred_element_type=F32)
             + qs * dq_ref[h])
        new_state = ds_ref[h] * state + kv
        s_ref[h] = new_state
        cols = slice(h * RET_DV, (h + 1) * RET_DV)
        o_ref[:, cols] = (rg_ref[:, cols].astype(F32) * _rms(o)).astype(BF16)

    pending = dots(0)
    for h in range(1, N_HEADS):
        nxt = dots(h)
        finish(h - 1, *pending)
        pending = nxt
    finish(N_HEADS - 1, *pending)

    @pl.when(t == pl.num_programs(1) - 1)
    def _():
        snew_ref[...] = s_ref[...]


def _retention(proj, state0, log_gamma, tile, chunk):
    B, S, _ = proj.shape
    nt = S // tile
    pos = jnp.arange(tile)
    ch = pos // chunk
    dist = (pos[:, None] - pos[None, :]).astype(F32)
    lg = log_gamma[:, None, None]
    same = ch[:, None] == ch[None, :]
    past = ch[None, :] < ch[:, None]
    dmat = jnp.where(same[None], jnp.exp(lg * jnp.abs(dist)[None]),
                     jnp.where(past[None], jnp.exp(lg * dist[None]), 0.0))
    tf = pos.astype(F32)
    dq = jnp.broadcast_to(jnp.exp((tf[None, :] + 1.0) * log_gamma[:, None])[:, :, None],
                          (N_HEADS, tile, RET_DV))
    dk = jnp.broadcast_to(jnp.exp((tile - 1.0 - tf)[None, :] * log_gamma[:, None])[:, :, None],
                          (N_HEADS, tile, RET_DK))
    ds = jnp.exp(tile * log_gamma)
    state_spec = pl.BlockSpec((None, N_HEADS, RET_DK, RET_DV), lambda b, t: (b, 0, 0, 0))
    return pl.pallas_call(
        _retention_kernel,
        out_shape=(jax.ShapeDtypeStruct((B, S, RV_W), BF16),
                   jax.ShapeDtypeStruct((B, N_HEADS, RET_DK, RET_DV), F32)),
        grid=(B, nt),
        in_specs=[
            pl.BlockSpec(memory_space=pltpu.SMEM),
            pl.BlockSpec((None, tile, QK_W), lambda b, t: (b, t, OFF_RQ // QK_W)),
            pl.BlockSpec((None, tile, QK_W), lambda b, t: (b, t, OFF_RK // QK_W)),
            pl.BlockSpec((None, tile, RV_W), lambda b, t: (b, t, OFF_RV // RV_W)),
            pl.BlockSpec((None, tile, RV_W), lambda b, t: (b, t, OFF_RG // RV_W)),
            _const_spec(dmat.shape), _const_spec(dq.shape), _const_spec(dk.shape),
            state_spec,
        ],
        out_specs=(pl.BlockSpec((None, tile, RV_W), lambda b, t: (b, t, 0)), state_spec),
        scratch_shapes=[pltpu.VMEM((N_HEADS, RET_DK, RET_DV), F32)],
        compiler_params=_params(("arbitrary", "arbitrary")),
        name="retention",
    )(ds, proj, proj, proj, proj, dmat, dq, dk, state0)


def _qk(q, k):
    return lax.dot_general(q, k, (((1,), (1,)), ((), ())), preferred_element_type=F32) * (ATT_DH ** -0.5)


def _softmax_pv(scores, values):
    m = functools.reduce(jnp.maximum, [jnp.max(s, axis=-1, keepdims=True) for s in scores])
    ps = [jnp.exp(s - m) for s in scores]
    l = functools.reduce(jnp.add, [jnp.sum(p, axis=-1, keepdims=True) for p in ps])
    o = functools.reduce(jnp.add, [jnp.dot(p.astype(BF16), v, preferred_element_type=F32)
                                   for p, v in zip(ps, values)])
    return o / l


def _per_head_pipelined(scores_fn, finish_fn):
    pending = scores_fn(0)
    for h in range(1, N_HEADS):
        nxt = scores_fn(h)
        finish_fn(h - 1, pending)
        pending = nxt
    finish_fn(N_HEADS - 1, pending)


def _bias_table_kernel(base_ref, o_ref):
    T, W = o_ref.shape
    n = base_ref.shape[-1]
    x = jnp.broadcast_to(base_ref[...], (T, n))
    x = pltpu.roll(x, n - T + 1, 1, stride=1, stride_axis=0)[:, :W]
    qc = lax.broadcasted_iota(jnp.int32, (T, W), 0) // CHUNK
    kc = lax.broadcasted_iota(jnp.int32, (T, W), 1) // CHUNK
    o_ref[...] = jnp.where((kc >= qc) & (kc <= qc + ATT_LEFT_CHUNKS), x, NEG)


def _bias_table(rel_bias):
    T = ATT_TILE
    W = (ATT_REACH // T + 1) * T
    n = 1024
    assert W + T - 1 <= n
    lo = (T - 1) + ATT_REACH - REL_CLIP
    hi = n - lo - (2 * REL_CLIP + 1)
    assert hi >= 0
    base = jnp.concatenate([jnp.broadcast_to(rel_bias[:, :1], (N_HEADS, lo)), rel_bias,
                            jnp.broadcast_to(rel_bias[:, -1:], (N_HEADS, hi))], axis=1)
    return pl.pallas_call(
        _bias_table_kernel,
        out_shape=jax.ShapeDtypeStruct((N_HEADS, T, W), F32),
        grid=(N_HEADS,),
        in_specs=[pl.BlockSpec((None, 1, n), lambda h: (h, 0, 0))],
        out_specs=pl.BlockSpec((None, T, W), lambda h: (h, 0, 0)),
        compiler_params=_params(("arbitrary",)),
        name="bias_table",
    )(base.reshape(N_HEADS, 1, n).astype(F32))


def _att_prompt_kernel(q_ref, k0_ref, k1_ref, k2_ref, v0_ref, v1_ref, v2_ref, tab_ref, o_ref):
    i = pl.program_id(1)
    T = ATT_TILE

    def scores(h):
        cols = slice(h * ATT_DH, (h + 1) * ATT_DH)
        q = q_ref[:, cols]
        s0 = _qk(q, k0_ref[:, cols]) + tab_ref[h, :, 0:T]
        s1 = _qk(q, k1_ref[:, cols]) + tab_ref[h, :, T:2 * T]
        s2 = _qk(q, k2_ref[:, cols]) + tab_ref[h, :, 2 * T:3 * T]
        return [jnp.where(i >= 2, s0, NEG), jnp.where(i >= 1, s1, NEG), s2]

    def finish(h, ss):
        cols = slice(h * ATT_DH, (h + 1) * ATT_DH)
        o = _softmax_pv(ss, [v0_ref[:, cols], v1_ref[:, cols], v2_ref[:, cols]])
        o_ref[:, cols] = o.astype(BF16)

    _per_head_pipelined(scores, finish)


def _attention_prompt(proj, rel_bias):
    B, S, _ = proj.shape
    T = ATT_TILE
    nq = S // T
    assert ATT_REACH // T + 1 == 3
    table = _bias_table(rel_bias)

    def blk(off, back):
        return pl.BlockSpec((None, T, QK_W), lambda b, i: (b, jnp.maximum(i - back, 0), off // QK_W))

    return pl.pallas_call(
        _att_prompt_kernel,
        out_shape=jax.ShapeDtypeStruct((B, S, QK_W), BF16),
        grid=(B, nq),
        in_specs=[blk(OFF_AQ, 0),
                  blk(OFF_AK, 2), blk(OFF_AK, 1), blk(OFF_AK, 0),
                  blk(OFF_AV, 2), blk(OFF_AV, 1), blk(OFF_AV, 0),
                  _const_spec(table.shape)],
        out_specs=pl.BlockSpec((None, T, QK_W), lambda b, i: (b, i, 0)),
        compiler_params=_params(("arbitrary", "arbitrary")),
        name="attention_prompt",
    )(proj, proj, proj, proj, proj, proj, proj, table)


def _att_sample_kernel(q_ref, kn_ref, vn_ref, kc_ref, vc_ref, tab_ref, o_ref):
    P = kc_ref.shape[0]

    def scores(h):
        cols = slice(h * ATT_DH, (h + 1) * ATT_DH)
        q = q_ref[:, cols]
        sc = _qk(q, kc_ref[:, h, :].astype(BF16)) + tab_ref[h, :, 0:P]
        sn = _qk(q, kn_ref[:, cols]) + tab_ref[h, :, P:]
        return [sc, sn]

    def finish(h, ss):
        cols = slice(h * ATT_DH, (h + 1) * ATT_DH)
        o = _softmax_pv(ss, [vc_ref[:, h, :].astype(BF16), vn_ref[:, cols]])
        o_ref[:, cols] = o.astype(BF16)

    _per_head_pipelined(scores, finish)


def _attention_sample(proj, cache_k, cache_v, layer, rel_bias):
    B, L, _ = proj.shape
    P = cache_k.shape[2]
    assert P == ATT_REACH and PAST_LEN >= ATT_REACH
    t = jnp.arange(L)
    rel = jnp.concatenate([jnp.arange(P)[None, :] - P - t[:, None], t[None, :] - t[:, None]], axis=1)
    table = rel_bias[:, jnp.clip(rel, -REL_CLIP, REL_CLIP) + REL_CLIP].astype(F32)
    cache_spec = pl.BlockSpec((None, None, P, N_HEADS, ATT_DH), lambda b: (layer, b, 0, 0, 0))
    return pl.pallas_call(
        _att_sample_kernel,
        out_shape=jax.ShapeDtypeStruct((B, L, QK_W), BF16),
        grid=(B,),
        in_specs=[pl.BlockSpec((None, L, QK_W), lambda b: (b, 0, OFF_AQ // QK_W)),
                  pl.BlockSpec((None, L, QK_W), lambda b: (b, 0, OFF_AK // QK_W)),
                  pl.BlockSpec((None, L, QK_W), lambda b: (b, 0, OFF_AV // QK_W)),
                  cache_spec, cache_spec,
                  _const_spec(table.shape)],
        out_specs=pl.BlockSpec((None, L, QK_W), lambda b: (b, 0, 0)),
        compiler_params=_params(("arbitrary",)),
        name="attention_sample",
    )(proj, proj, proj, cache_k, cache_v, table)


def _merge_kernel(gret_ref, oatt_ref, gr0_ref, gr1_ref, ga0_ref, ga1_ref, x_ref, gt_ref, g_ref,
                  wr_ref, wa_ref, wo_ref, o_ref, m_ref):
    y_ret = jnp.dot(gret_ref[...], wr_ref[...], preferred_element_type=F32)
    y_att = jnp.dot(oatt_ref[...], wa_ref[...], preferred_element_type=F32)
    half = D_MODEL // 2
    m_ref[:, :half] = (gr0_ref[...].astype(F32) * y_ret[:, :half]
                       + ga0_ref[...].astype(F32) * y_att[:, :half]).astype(BF16)
    m_ref[:, half:] = (gr1_ref[...].astype(F32) * y_ret[:, half:]
                       + ga1_ref[...].astype(F32) * y_att[:, half:]).astype(BF16)
    z = jnp.dot(m_ref[...], wo_ref[...], preferred_element_type=F32)
    o_ref[...] = x_ref[...] + gt_ref[...] * (_rms(z) * g_ref[...])


def _merge(x, g_ret, o_att, proj, mod, g_post1, w_br_ret, w_br_att, w_out, tm):
    B, S, _ = x.shape
    R = mod.shape[1]
    tpb = S // tm
    assert S % tm == 0 and (R == 1 or tpb == 1)
    rm = R if R == 1 else tm
    half = D_MODEL // 2
    row = lambda i: (i // tpb, i % tpb)

    def gate_spec(off):
        return pl.BlockSpec((None, tm, half), lambda i: (*row(i), off // half))

    return pl.pallas_call(
        _merge_kernel,
        out_shape=jax.ShapeDtypeStruct((B, S, D_MODEL), F32),
        grid=(B * tpb,),
        in_specs=[pl.BlockSpec((None, tm, RV_W), lambda i: (*row(i), 0)),
                  pl.BlockSpec((None, tm, QK_W), lambda i: (*row(i), 0)),
                  gate_spec(OFF_GR), gate_spec(OFF_GR + half),
                  gate_spec(OFF_GA), gate_spec(OFF_GA + half),
                  pl.BlockSpec((None, tm, D_MODEL), lambda i: (*row(i), 0)),
                  pl.BlockSpec((None, rm, D_MODEL), lambda i: (i // tpb, 0, 2)),
                  pl.BlockSpec((1, D_MODEL), lambda i: (0, 0)),
                  _const_spec(w_br_ret.shape), _const_spec(w_br_att.shape), _const_spec(w_out.shape)],
        out_specs=pl.BlockSpec((None, tm, D_MODEL), lambda i: (*row(i), 0)),
        scratch_shapes=[pltpu.VMEM((tm, D_MODEL), BF16)],
        compiler_params=_params(("arbitrary",)),
        name="merge_out",
    )(g_ret, o_att, proj, proj, proj, proj, x, mod, g_post1.reshape(1, D_MODEL),
      w_br_ret, w_br_att, w_out)


HALO = 8


def _ffn_kernel(x_ref, sh_ref, sc_ref, gt_ref, gpre_ref, gpost_ref, wv_ref, wg_ref, cwv_ref, cwg_ref,
                cbv_ref, cbg_ref, wd_ref, cin_ref, o_ref, cov_ref, cog_ref,
                u_ref, upa_ref, upb_ref, h_ref, carry_ref, *, nb, tpb, nj):
    i, j = pl.program_id(0), pl.program_id(1)
    tn = FFN_TN
    L = x_ref.shape[0] // nb
    ups = (upa_ref, upb_ref)
    base = [s * (L + HALO) + HALO for s in range(nb)]
    rb = min(L, 64)

    def up_half(t, half):
        acc = jnp.dot(u_ref[...], (wv_ref, wg_ref)[half][...], preferred_element_type=F32)
        for s in range(nb):
            ups[t % 2][base[s]:base[s] + L, half * tn:(half + 1) * tn] = acc[s * L:(s + 1) * L]

    def make_gate_pieces(t):
        src = ups[t % 2]

        def history():
            for s in range(nb):
                src[base[s] - 2:base[s], :] = carry_ref[j - 1, s]

        def conv(r0, cols, cw_ref, cb_ref, wc):
            cur = src[r0:r0 + rb, cols]
            p1 = src[r0 - 1:r0 - 1 + rb, cols]
            p2 = src[r0 - 2:r0 - 2 + rb, cols]
            return (cb_ref[:, wc] + cw_ref[2:3, wc] * cur + cw_ref[1:2, wc] * p1 + cw_ref[0:1, wc] * p2)

        def piece(p, s, r):
            wc = slice(p * FFN_SUB, (p + 1) * FFN_SUB)
            cg = slice(tn + p * FFN_SUB, tn + (p + 1) * FFN_SUB)
            value = conv(base[s] + r, wc, cwv_ref, cbv_ref, wc)
            gate = conv(base[s] + r, cg, cwg_ref, cbg_ref, wc)
            h_ref[s * L + r:s * L + r + rb, wc] = (jax.nn.gelu(gate, approximate=True) * value).astype(BF16)

        def finish():
            for s in range(nb):
                last = src[base[s] + L - 2:base[s] + L, :]
                carry_ref[j - 1, s] = last
                cov_ref[s] = last[:, :tn]
                cog_ref[s] = last[:, tn:]

        pieces = [functools.partial(piece, p, s, r) for p in range(tn // FFN_SUB)
                  for s in range(nb) for r in range(0, L, rb)]
        return history, pieces, finish

    def down_half(c):
        kk = slice(c * tn // 2, (c + 1) * tn // 2)
        w = 512
        for n in range(D_MODEL // w):
            o_ref[:, n * w:(n + 1) * w] += jnp.dot(h_ref[:, kk], wd_ref[kk, n * w:(n + 1) * w],
                                                   preferred_element_type=F32)

    def gate_down(t, c):
        history, pieces, finish = make_gate_pieces(t)
        m = len(pieces) // 2
        if c == 0:
            history()
        for thunk in pieces[c * m:(c + 1) * m]:
            thunk()
        down_half(c)
        if c == 1:
            finish()

    @pl.when(j == 0)
    def _():
        u = (_rms(x_ref[...]) * gpre_ref[...]) * (1.0 + sc_ref[...]) + sh_ref[...]
        u_ref[...] = u.astype(BF16)
        o_ref[...] = jnp.zeros(o_ref.shape, F32)

        @pl.when(i % tpb == 0)
        def _():
            for jj in range(nj):
                carry_ref[jj, :, :, :tn] = cin_ref[:, :, jj * tn:(jj + 1) * tn]
                carry_ref[jj, :, :, tn:] = cin_ref[:, :, D_FF + jj * tn:D_FF + (jj + 1) * tn]

        up_half(0, 0)
        up_half(0, 1)

    def steady(parity):
        up_half(parity, 0)
        gate_down(parity + 1, 0)
        up_half(parity, 1)
        gate_down(parity + 1, 1)

    pl.when((j >= 1) & (j < nj) & (j % 2 == 0))(lambda: steady(0))
    pl.when((j >= 1) & (j < nj) & (j % 2 == 1))(lambda: steady(1))

    @pl.when(j == nj)
    def _():
        gate_down(nj - 1, 0)
        gate_down(nj - 1, 1)
        o_ref[...] = x_ref[...] + gt_ref[...] * (_rms(o_ref[...]) * gpost_ref[...])


def _ffn(x, mod, conv_in, g_pre2, g_post2, w_up, conv_w, conv_b, w_down, tm, nb):
    B, S, _ = x.shape
    R = mod.shape[1]
    tpb = S // tm
    assert S % tm == 0 and (R == 1 or tpb == 1) and (nb == 1 or tpb == 1)
    rm = R if R == 1 else tm
    tn = FFN_TN
    nj = D_FF // tn
    row = lambda i: (i // tpb, i % tpb)
    t_up = lambda j: jnp.minimum(j, nj - 1)
    t_gate = lambda j: jnp.maximum(j - 1, 0)
    conv_b2 = conv_b.reshape(1, 2 * D_FF)
    conv_out = jax.ShapeDtypeStruct((B * tpb * nb, CONV_W - 1, D_FF), F32)
    conv_out_spec = pl.BlockSpec((nb, CONV_W - 1, tn), lambda i, j: (i, 0, t_gate(j)))
    up_rows = nb * (tm // nb + HALO)
    y, co_v, co_g = pl.pallas_call(
        functools.partial(_ffn_kernel, nb=nb, tpb=tpb, nj=nj),
        out_shape=(jax.ShapeDtypeStruct((B, S, D_MODEL), F32), conv_out, conv_out),
        grid=(B * tpb, nj + 1),
        in_specs=[pl.BlockSpec((None, tm, D_MODEL), lambda i, j: (*row(i), 0)),
                  pl.BlockSpec((None, rm, D_MODEL), lambda i, j: (i // tpb, 0, 3)),
                  pl.BlockSpec((None, rm, D_MODEL), lambda i, j: (i // tpb, 0, 4)),
                  pl.BlockSpec((None, rm, D_MODEL), lambda i, j: (i // tpb, 0, 5)),
                  pl.BlockSpec((1, D_MODEL), lambda i, j: (0, 0)),
                  pl.BlockSpec((1, D_MODEL), lambda i, j: (0, 0)),
                  pl.BlockSpec((D_MODEL, tn), lambda i, j: (0, t_up(j))),
                  pl.BlockSpec((D_MODEL, tn), lambda i, j: (0, nj + t_up(j))),
                  pl.BlockSpec((CONV_W, tn), lambda i, j: (0, t_gate(j))),
                  pl.BlockSpec((CONV_W, tn), lambda i, j: (0, nj + t_gate(j))),
                  pl.BlockSpec((1, tn), lambda i, j: (0, t_gate(j))),
                  pl.BlockSpec((1, tn), lambda i, j: (0, nj + t_gate(j))),
                  pl.BlockSpec((tn, D_MODEL), lambda i, j: (t_gate(j), 0)),
                  pl.BlockSpec((nb, CONV_W - 1, 2 * D_FF), lambda i, j: (i // tpb, 0, 0))],
        out_specs=(pl.BlockSpec((None, tm, D_MODEL), lambda i, j: (*row(i), 0)),
                   conv_out_spec, conv_out_spec),
        scratch_shapes=[pltpu.VMEM((tm, D_MODEL), BF16),
                        pltpu.VMEM((up_rows, 2 * tn), F32),
                        pltpu.VMEM((up_rows, 2 * tn), F32),
                        pltpu.VMEM((tm, tn), BF16),
                        pltpu.VMEM((nj, nb, CONV_W - 1, 2 * tn), F32)],
        compiler_params=_params(("arbitrary", "arbitrary")),
        name="conv_ffn",
    )(x, mod, mod, mod, g_pre2.reshape(1, D_MODEL), g_post2.reshape(1, D_MODEL), w_up, w_up,
      conv_w, conv_w, conv_b2, conv_b2, w_down, conv_in)
    return y, jnp.concatenate([co_v, co_g], axis=-1)


def _rotary_tables(pos):
    half = RET_DK // 2
    inv = ROPE_BASE ** (-jnp.arange(half, dtype=F32) / half)
    ang = pos.astype(F32)[:, None] * inv[None, :]
    cos, sin = jnp.cos(ang), jnp.sin(ang)
    return jnp.concatenate([cos, cos], axis=-1), jnp.concatenate([-sin, sin], axis=-1)


def kernel(x_prompt, x_sample, cache_att_k, cache_att_v, state_ret, state_conv, c_prompt, c_sample,
           w_ada, b_ada, g_pre1, w_in, rel_bias, w_br_ret, w_br_att, w_out, g_post1, g_pre2,
           w_up, conv_w, conv_b, w_down, g_post2):
    depth = w_ada.shape[0]
    Bp, Sp, _ = x_prompt.shape
    Bs, Ls, _ = x_sample.shape
    log_gamma = jnp.log(1.0 - 2.0 ** (-5.0 - jnp.arange(N_HEADS, dtype=F32)))
    cos_p, sin_p = _rotary_tables(jnp.arange(Sp))
    cos_s, sin_s = _rotary_tables(PAST_LEN + jnp.arange(Ls))
    cos_s, sin_s = jnp.tile(cos_s, (Bs, 1)), jnp.tile(sin_s, (Bs, 1))
    keep = min(ATT_REACH, Sp)
    tm_p = 512

    yp = x_prompt
    ys = x_sample.reshape(1, Bs * Ls, D_MODEL)
    outs = [[] for _ in range(8)]
    for l in range(depth):
        w_in_b, w_up_b, w_down_b = w_in[l].astype(BF16), w_up[l].astype(BF16), w_down[l].astype(BF16)
        w_br_ret_b, w_br_att_b, w_out_b = (w_br_ret[l].astype(BF16), w_br_att[l].astype(BF16),
                                           w_out[l].astype(BF16))
        mod = _mod(jnp.concatenate([c_prompt, c_sample], axis=0), w_ada[l], b_ada[l])
        mod_p = mod[:Bp].reshape(Bp, 1, 6 * D_MODEL)
        mod_s = jnp.repeat(mod[Bp:], Ls, axis=0).reshape(1, Bs * Ls, 6 * D_MODEL)

        proj = _in_proj(yp, mod_p, g_pre1[l], w_in_b, cos_p, sin_p, tm=tm_p)
        zero_state = jnp.zeros((Bp, N_HEADS, RET_DK, RET_DV), F32)
        g_ret, r_p = _retention(proj, zero_state, log_gamma, RET_TILE, CHUNK)
        o_att = _attention_prompt(proj, rel_bias[l])
        x1 = _merge(yp, g_ret, o_att, proj, mod_p, g_post1[l], w_br_ret_b, w_br_att_b, w_out_b, tm=tm_p)
        conv0 = jnp.zeros((Bp, CONV_W - 1, 2 * D_FF), F32)
        yp, cv_p = _ffn(x1, mod_p, conv0, g_pre2[l], g_post2[l], w_up_b, conv_w[l], conv_b[l],
                        w_down_b, tm=tm_p, nb=1)
        cv_p = cv_p.reshape(Bp, Sp // tm_p, CONV_W - 1, 2 * D_FF)[:, -1]
        k_p = proj[:, Sp - keep:, OFF_AK:OFF_AV].astype(F32).reshape(Bp, keep, N_HEADS, ATT_DH)
        v_p = proj[:, Sp - keep:, OFF_AV:OFF_GR].astype(F32).reshape(Bp, keep, N_HEADS, ATT_DH)

        proj_s = _in_proj(ys, mod_s, g_pre1[l], w_in_b, cos_s, sin_s, tm=Bs * Ls)
        proj_s3 = proj_s.reshape(Bs, Ls, IN_WIDTH)
        g_ret_s, r_s = _retention(proj_s3, state_ret[l], log_gamma, Ls, Ls)
        o_att_s = _attention_sample(proj_s3, cache_att_k, cache_att_v, l, rel_bias[l])
        x1_s = _merge(ys, g_ret_s.reshape(1, Bs * Ls, -1), o_att_s.reshape(1, Bs * Ls, -1), proj_s,
                      mod_s, g_post1[l], w_br_ret_b, w_br_att_b, w_out_b, tm=Bs * Ls)
        ys, cv_s = _ffn(x1_s, mod_s, state_conv[l], g_pre2[l], g_post2[l], w_up_b, conv_w[l],
                        conv_b[l], w_down_b, tm=Bs * Ls, nb=Bs)
        k_s = proj_s3[:, :, OFF_AK:OFF_AV].astype(F32).reshape(Bs, Ls, N_HEADS, ATT_DH)
        v_s = proj_s3[:, :, OFF_AV:OFF_GR].astype(F32).reshape(Bs, Ls, N_HEADS, ATT_DH)

        for lst, val in zip(outs, (k_p, v_p, r_p, cv_p, k_s, v_s, r_s, cv_s)):
            lst.append(val)
    return (yp, ys.reshape(Bs, Ls, D_MODEL), *[jnp.stack(o) for o in outs])
```

```python
import functools

import jax
import jax.numpy as jnp
from jax import lax
from jax.experimental import pallas as pl
from jax.experimental.pallas import tpu as pltpu

F32 = jnp.float32
BF16 = jnp.bfloat16

D_MODEL = 2048
PAST_LEN = 4096
CHUNK = 64
N_HEADS = 8
RET_DK = 128
RET_DV = 256
ATT_DH = 128
ATT_LEFT_CHUNKS = 8
ATT_REACH = ATT_LEFT_CHUNKS * CHUNK
REL_CLIP = 128
D_FF = 5632
CONV_W = 3
ROPE_BASE = 10000.0
EPS = 1e-6
IN_WIDTH = 13312
NEG = -1e30

OFF_RQ, OFF_RK, OFF_RV, OFF_RG = 0, 1024, 2048, 4096
OFF_AQ, OFF_AK, OFF_AV, OFF_GR, OFF_GA = 6144, 7168, 8192, 9216, 11264
QK_W = N_HEADS * RET_DK
RV_W = N_HEADS * RET_DV

LANE = 128
MXU_N = 256
VMEM_LIMIT = 56 * 1024 * 1024

PROJ_TN = 1024
RET_TILE = 256
ATT_TILE = 256
FFN_TN = 512
FFN_SUB = 128


def _params(sem):
    return pltpu.CompilerParams(dimension_semantics=sem, vmem_limit_bytes=VMEM_LIMIT)


def _rms(x):
    return x * lax.rsqrt(jnp.mean(x * x, axis=-1, keepdims=True) + EPS)


def _const_spec(shape):
    zeros = (0,) * len(shape)
    return pl.BlockSpec(shape, lambda *_: zeros, pipeline_mode=pl.Buffered(1))


def _mod_kernel(c_ref, w_ref, b_ref, o_ref):
    a = jax.nn.silu(c_ref[...]).astype(BF16)
    o_ref[...] = jnp.dot(a, w_ref[...].astype(BF16), preferred_element_type=F32) + b_ref[...]


def _mod(c, w_ada, b_ada):
    n, tn = c.shape[0], 1024
    width = w_ada.shape[1]
    return pl.pallas_call(
        _mod_kernel,
        out_shape=jax.ShapeDtypeStruct((n, width), F32),
        grid=(width // tn,),
        in_specs=[pl.BlockSpec((n, D_MODEL), lambda j: (0, 0)),
                  pl.BlockSpec((D_MODEL, tn), lambda j: (0, j)),
                  pl.BlockSpec((1, tn), lambda j: (0, j))],
        out_specs=pl.BlockSpec((n, tn), lambda j: (0, j)),
        compiler_params=_params(("arbitrary",)),
        name="adaln_mod",
    )(c, w_ada, b_ada.reshape(1, width))


def _inproj_kernel(x_ref, sh_ref, sc_ref, g_ref, w_ref, cos_ref, sin_ref, o_ref, u_ref):
    j = pl.program_id(1)

    @pl.when(j == 0)
    def _():
        u = (_rms(x_ref[...]) * g_ref[...]) * (1.0 + sc_ref[...]) + sh_ref[...]
        u_ref[...] = u.astype(BF16)

    def tile(epilogue):
        for c in range(PROJ_TN // MXU_N):
            cols = slice(c * MXU_N, (c + 1) * MXU_N)
            acc = jnp.dot(u_ref[...], w_ref[:, cols], preferred_element_type=F32)
            o_ref[:, cols] = epilogue(acc).astype(BF16)

    def rotary(acc):
        scale = jnp.where(j == OFF_RK // PROJ_TN, RET_DK ** -0.5, 1.0).astype(F32)
        cos, sin = cos_ref[...], sin_ref[...]
        heads = [acc[:, h * LANE:(h + 1) * LANE] for h in range(MXU_N // LANE)]
        return jnp.concatenate(
            [(a * cos + pltpu.roll(a, LANE // 2, 1) * sin) * scale for a in heads], axis=-1)

    pl.when(j < OFF_RV // PROJ_TN)(lambda: tile(rotary))
    pl.when((j >= OFF_RG // PROJ_TN) & (j < OFF_AQ // PROJ_TN))(lambda: tile(jax.nn.silu))
    pl.when(j >= OFF_GR // PROJ_TN)(lambda: tile(jax.nn.sigmoid))
    pl.when(((j >= OFF_RV // PROJ_TN) & (j < OFF_RG // PROJ_TN))
            | ((j >= OFF_AQ // PROJ_TN) & (j < OFF_GR // PROJ_TN)))(lambda: tile(lambda a: a))


def _in_proj(x, mod, g_pre1, w_in, cos, sin, tm):
    B, S, _ = x.shape
    R = mod.shape[1]
    tpb = S // tm
    assert S % tm == 0 and (R == 1 or tpb == 1)
    rm = R if R == 1 else tm
    return pl.pallas_call(
        _inproj_kernel,
        out_shape=jax.ShapeDtypeStruct((B, S, IN_WIDTH), BF16),
        grid=(B * tpb, IN_WIDTH // PROJ_TN),
        in_specs=[
            pl.BlockSpec((None, tm, D_MODEL), lambda i, j: (i // tpb, i % tpb, 0)),
            pl.BlockSpec((None, rm, D_MODEL), lambda i, j: (i // tpb, 0, 0)),
            pl.BlockSpec((None, rm, D_MODEL), lambda i, j: (i // tpb, 0, 1)),
            pl.BlockSpec((1, D_MODEL), lambda i, j: (0, 0)),
            pl.BlockSpec((None, D_MODEL, PROJ_TN), lambda i, j: (j, 0, 0)),
            pl.BlockSpec((tm, LANE), lambda i, j: (i % tpb, 0)),
            pl.BlockSpec((tm, LANE), lambda i, j: (i % tpb, 0)),
        ],
        out_specs=pl.BlockSpec((None, tm, PROJ_TN), lambda i, j: (i // tpb, i % tpb, j)),
        scratch_shapes=[pltpu.VMEM((tm, D_MODEL), BF16)],
        compiler_params=_params(("arbitrary", "arbitrary")),
        name="in_proj",
    )(x, mod, mod, g_pre1.reshape(1, D_MODEL), w_in, cos, sin)


def _retention_kernel(ds_ref, q_ref, k_ref, v_ref, rg_ref, dmat_ref, dq_ref, dk_ref, s0_ref,
                      o_ref, snew_ref, s_ref):
    t = pl.program_id(1)

    @pl.when(t == 0)
    def _():
        s_ref[...] = s0_ref[...]

    def dots(h):
        q = q_ref[:, h * RET_DK:(h + 1) * RET_DK]
        k = k_ref[:, h * RET_DK:(h + 1) * RET_DK]
        v = v_ref[:, h * RET_DV:(h + 1) * RET_DV]
        state = s_ref[h]
        s = lax.dot_general(q, k, (((1,), (1,)), ((), ())), preferred_element_type=F32)
        qs = jnp.dot(q, state.astype(BF16), preferred_element_type=F32)
        kd = (k.astype(F32) * dk_ref[h]).astype(BF16)
        kv = lax.dot_general(kd, v, (((0,), (0,)), ((), ())), preferred_element_type=F32)
        return s, qs, kv, state, v

    def finish(h, s, qs, kv, state, v):
        o = (jnp.dot((s * dmat_ref[h]).astype(BF16), v, preferred_element_type=F32)
             + qs * dq_ref[h])
        new_state = ds_ref[h] * state + kv
        s_ref[h] = new_state
        cols = slice(h * RET_DV, (h + 1) * RET_DV)
        o_ref[:, cols] = (rg_ref[:, cols].astype(F32) * _rms(o)).astype(BF16)

    pending = dots(0)
    for h in range(1, N_HEADS):
        nxt = dots(h)
        finish(h - 1, *pending)
        pending = nxt
    finish(N_HEADS - 1, *pending)

    @pl.when(t == pl.num_programs(1) - 1)
    def _():
        snew_ref[...] = s_ref[...]


def _retention(proj, state0, log_gamma, tile, chunk):
    B, S, _ = proj.shape
    nt = S // tile
    pos = jnp.arange(tile)
    ch = pos // chunk
    dist = (pos[:, None] - pos[None, :]).astype(F32)
    lg = log_gamma[:, None, None]
    same = ch[:, None] == ch[None, :]
    past = ch[None, :] < ch[:, None]
    dmat = jnp.where(same[None], jnp.exp(lg * jnp.abs(dist)[None]),
                     jnp.where(past[None], jnp.exp(lg * dist[None]), 0.0))
    tf = pos.astype(F32)
    dq = jnp.broadcast_to(jnp.exp((tf[None, :] + 1.0) * log_gamma[:, None])[:, :, None],
                          (N_HEADS, tile, RET_DV))
    dk = jnp.broadcast_to(jnp.exp((tile - 1.0 - tf)[None, :] * log_gamma[:, None])[:, :, None],
                          (N_HEADS, tile, RET_DK))
    ds = jnp.exp(tile * log_gamma)
    state_spec = pl.BlockSpec((None, N_HEADS, RET_DK, RET_DV), lambda b, t: (b, 0, 0, 0))
    return pl.pallas_call(
        _retention_kernel,
        out_shape=(jax.ShapeDtypeStruct((B, S, RV_W), BF16),
                   jax.ShapeDtypeStruct((B, N_HEADS, RET_DK, RET_DV), F32)),
        grid=(B, nt),
        in_specs=[
            pl.BlockSpec(memory_space=pltpu.SMEM),
            pl.BlockSpec((None, tile, QK_W), lambda b, t: (b, t, OFF_RQ // QK_W)),
            pl.BlockSpec((None, tile, QK_W), lambda b, t: (b, t, OFF_RK // QK_W)),
            pl.BlockSpec((None, tile, RV_W), lambda b, t: (b, t, OFF_RV // RV_W)),
            pl.BlockSpec((None, tile, RV_W), lambda b, t: (b, t, OFF_RG // RV_W)),
            _const_spec(dmat.shape), _const_spec(dq.shape), _const_spec(dk.shape),
            state_spec,
        ],
        out_specs=(pl.BlockSpec((None, tile, RV_W), lambda b, t: (b, t, 0)), state_spec),
        scratch_shapes=[pltpu.VMEM((N_HEADS, RET_DK, RET_DV), F32)],
        compiler_params=_params(("arbitrary", "arbitrary")),
        name="retention",
    )(ds, proj, proj, proj, proj, dmat, dq, dk, state0)


def _qk(q, k):
    return lax.dot_general(q, k, (((1,), (1,)), ((), ())), preferred_element_type=F32) * (ATT_DH ** -0.5)


def _softmax_pv(scores, values):
    m = functools.reduce(jnp.maximum, [jnp.max(s, axis=-1, keepdims=True) for s in scores])
    ps = [jnp.exp(s - m) for s in scores]
    l = functools.reduce(jnp.add, [jnp.sum(p, axis=-1, keepdims=True) for p in ps])
    o = functools.reduce(jnp.add, [jnp.dot(p.astype(BF16), v, preferred_element_type=F32)
                                   for p, v in zip(ps, values)])
    return o / l


def _per_head_pipelined(scores_fn, finish_fn):
    pending = scores_fn(0)
    for h in range(1, N_HEADS):
        nxt = scores_fn(h)
        finish_fn(h - 1, pending)
        pending = nxt
    finish_fn(N_HEADS - 1, pending)


def _bias_table_kernel(base_ref, o_ref):
    T, W = o_ref.shape
    n = base_ref.shape[-1]
    x = jnp.broadcast_to(base_ref[...], (T, n))
    x = pltpu.roll(x, n - T + 1, 1, stride=1, stride_axis=0)[:, :W]
    qc = lax.broadcasted_iota(jnp.int32, (T, W), 0) // CHUNK
    kc = lax.broadcasted_iota(jnp.int32, (T, W), 1) // CHUNK
    o_ref[...] = jnp.where((kc >= qc) & (kc <= qc + ATT_LEFT_CHUNKS), x, NEG)


def _bias_table(rel_bias):
    T = ATT_TILE
    W = (ATT_REACH // T + 1) * T
    n = 1024
    assert W + T - 1 <= n
    lo = (T - 1) + ATT_REACH - REL_CLIP
    hi = n - lo - (2 * REL_CLIP + 1)
    assert hi >= 0
    base = jnp.concatenate([jnp.broadcast_to(rel_bias[:, :1], (N_HEADS, lo)), rel_bias,
                            jnp.broadcast_to(rel_bias[:, -1:], (N_HEADS, hi))], axis=1)
    return pl.pallas_call(
        _bias_table_kernel,
        out_shape=jax.ShapeDtypeStruct((N_HEADS, T, W), F32),
        grid=(N_HEADS,),
        in_specs=[pl.BlockSpec((None, 1, n), lambda h: (h, 0, 0))],
        out_specs=pl.BlockSpec((None, T, W), lambda h: (h, 0, 0)),
        compiler_params=_params(("arbitrary",)),
        name="bias_table",
    )(base.reshape(N_HEADS, 1, n).astype(F32))


def _att_prompt_kernel(q_ref, k0_ref, k1_ref, k2_ref, v0_ref, v1_ref, v2_ref, tab_ref, o_ref):
    i = pl.program_id(1)
    T = ATT_TILE

    def scores(h):
        cols = slice(h * ATT_DH, (h + 1) * ATT_DH)
        q = q_ref[:, cols]
        s0 = _qk(q, k0_ref[:, cols]) + tab_ref[h, :, 0:T]
        s1 = _qk(q, k1_ref[:, cols]) + tab_ref[h, :, T:2 * T]
        s2 = _qk(q, k2_ref[:, cols]) + tab_ref[h, :, 2 * T:3 * T]
        return [jnp.where(i >= 2, s0, NEG), jnp.where(i >= 1, s1, NEG), s2]

    def finish(h, ss):
        cols = slice(h * ATT_DH, (h + 1) * ATT_DH)
        o = _softmax_pv(ss, [v0_ref[:, cols], v1_ref[:, cols], v2_ref[:, cols]])
        o_ref[:, cols] = o.astype(BF16)

    _per_head_pipelined(scores, finish)


def _attention_prompt(proj, rel_bias):
    B, S, _ = proj.shape
    T = ATT_TILE
    nq = S // T
    assert ATT_REACH // T + 1 == 3
    table = _bias_table(rel_bias)

    def blk(off, back):
        return pl.BlockSpec((None, T, QK_W), lambda b, i: (b, jnp.maximum(i - back, 0), off // QK_W))

    return pl.pallas_call(
        _att_prompt_kernel,
        out_shape=jax.ShapeDtypeStruct((B, S, QK_W), BF16),
        grid=(B, nq),
        in_specs=[blk(OFF_AQ, 0),
                  blk(OFF_AK, 2), blk(OFF_AK, 1), blk(OFF_AK, 0),
                  blk(OFF_AV, 2), blk(OFF_AV, 1), blk(OFF_AV, 0),
                  _const_spec(table.shape)],
        out_specs=pl.BlockSpec((None, T, QK_W), lambda b, i: (b, i, 0)),
        compiler_params=_params(("arbitrary", "arbitrary")),
        name="attention_prompt",
    )(proj, proj, proj, proj, proj, proj, proj, table)


def _att_sample_kernel(q_ref, kn_ref, vn_ref, kc_ref, vc_ref, tab_ref, o_ref):
    P = kc_ref.shape[0]

    def scores(h):
        cols = slice(h * ATT_DH, (h + 1) * ATT_DH)
        q = q_ref[:, cols]
        sc = _qk(q, kc_ref[:, h, :].astype(BF16)) + tab_ref[h, :, 0:P]
        sn = _qk(q, kn_ref[:, cols]) + tab_ref[h, :, P:]
        return [sc, sn]

    def finish(h, ss):
        cols = slice(h * ATT_DH, (h + 1) * ATT_DH)
        o = _softmax_pv(ss, [vc_ref[:, h, :].astype(BF16), vn_ref[:, cols]])
        o_ref[:, cols] = o.astype(BF16)

    _per_head_pipelined(scores, finish)


def _attention_sample(proj, cache_k, cache_v, layer, rel_bias):
    B, L, _ = proj.shape
    P = cache_k.shape[2]
    assert P == ATT_REACH and PAST_LEN >= ATT_REACH
    t = jnp.arange(L)
    rel = jnp.concatenate([jnp.arange(P)[None, :] - P - t[:, None], t[None, :] - t[:, None]], axis=1)
    table = rel_bias[:, jnp.clip(rel, -REL_CLIP, REL_CLIP) + REL_CLIP].astype(F32)
    cache_spec = pl.BlockSpec((None, None, P, N_HEADS, ATT_DH), lambda b: (layer, b, 0, 0, 0))
    return pl.pallas_call(
        _att_sample_kernel,
        out_shape=jax.ShapeDtypeStruct((B, L, QK_W), BF16),
        grid=(B,),
        in_specs=[pl.BlockSpec((None, L, QK_W), lambda b: (b, 0, OFF_AQ // QK_W)),
                  pl.BlockSpec((None, L, QK_W), lambda b: (b, 0, OFF_AK // QK_W)),
                  pl.BlockSpec((None, L, QK_W), lambda b: (b, 0, OFF_AV // QK_W)),
                  cache_spec, cache_spec,
                  _const_spec(table.shape)],
        out_specs=pl.BlockSpec((None, L, QK_W), lambda b: (b, 0, 0)),
        compiler_params=_params(("arbitrary",)),
        name="attention_sample",
    )(proj, proj, proj, cache_k, cache_v, table)


def _merge_kernel(gret_ref, oatt_ref, gr0_ref, gr1_ref, ga0_ref, ga1_ref, x_ref, gt_ref, g_ref,
                  wr_ref, wa_ref, wo_ref, o_ref, m_ref):
    y_ret = jnp.dot(gret_ref[...], wr_ref[...], preferred_element_type=F32)
    y_att = jnp.dot(oatt_ref[...], wa_ref[...], preferred_element_type=F32)
    half = D_MODEL // 2
    m_ref[:, :half] = (gr0_ref[...].astype(F32) * y_ret[:, :half]
                       + ga0_ref[...].astype(F32) * y_att[:, :half]).astype(BF16)
    m_ref[:, half:] = (gr1_ref[...].astype(F32) * y_ret[:, half:]
                       + ga1_ref[...].astype(F32) * y_att[:, half:]).astype(BF16)
    z = jnp.dot(m_ref[...], wo_ref[...], preferred_element_type=F32)
    o_ref[...] = x_ref[...] + gt_ref[...] * (_rms(z) * g_ref[...])


def _merge(x, g_ret, o_att, proj, mod, g_post1, w_br_ret, w_br_att, w_out, tm):
    B, S, _ = x.shape
    R = mod.shape[1]
    tpb = S // tm
    assert S % tm == 0 and (R == 1 or tpb == 1)
    rm = R if R == 1 else tm
    half = D_MODEL // 2
    row = lambda i: (i // tpb, i % tpb)

    def gate_spec(off):
        return pl.BlockSpec((None, tm, half), lambda i: (*row(i), off // half))

    return pl.pallas_call(
        _merge_kernel,
        out_shape=jax.ShapeDtypeStruct((B, S, D_MODEL), F32),
        grid=(B * tpb,),
        in_specs=[pl.BlockSpec((None, tm, RV_W), lambda i: (*row(i), 0)),
                  pl.BlockSpec((None, tm, QK_W), lambda i: (*row(i), 0)),
                  gate_spec(OFF_GR), gate_spec(OFF_GR + half),
                  gate_spec(OFF_GA), gate_spec(OFF_GA + half),
                  pl.BlockSpec((None, tm, D_MODEL), lambda i: (*row(i), 0)),
                  pl.BlockSpec((None, rm, D_MODEL), lambda i: (i // tpb, 0, 2)),
                  pl.BlockSpec((1, D_MODEL), lambda i: (0, 0)),
                  _const_spec(w_br_ret.shape), _const_spec(w_br_att.shape), _const_spec(w_out.shape)],
        out_specs=pl.BlockSpec((None, tm, D_MODEL), lambda i: (*row(i), 0)),
        scratch_shapes=[pltpu.VMEM((tm, D_MODEL), BF16)],
        compiler_params=_params(("arbitrary",)),
        name="merge_out",
    )(g_ret, o_att, proj, proj, proj, proj, x, mod, g_post1.reshape(1, D_MODEL),
      w_br_ret, w_br_att, w_out)


HALO = 8


def _ffn_kernel(x_ref, sh_ref, sc_ref, gt_ref, gpre_ref, gpost_ref, wv_ref, wg_ref, cwv_ref, cwg_ref,
                cbv_ref, cbg_ref, wd_ref, cin_ref, o_ref, cov_ref, cog_ref,
                u_ref, upa_ref, upb_ref, h_ref, carry_ref, *, nb, tpb, nj):
    i, j = pl.program_id(0), pl.program_id(1)
    tn = FFN_TN
    L = x_ref.shape[0] // nb
    ups = (upa_ref, upb_ref)
    base = [s * (L + HALO) + HALO for s in range(nb)]
    rb = min(L, 64)

    def up_half(t, half):
        acc = jnp.dot(u_ref[...], (wv_ref, wg_ref)[half][...], preferred_element_type=F32)
        for s in range(nb):
            ups[t % 2][base[s]:base[s] + L, half * tn:(half + 1) * tn] = acc[s * L:(s + 1) * L]

    def make_gate_pieces(t):
        src = ups[t % 2]

        def history():
            for s in range(nb):
                src[base[s] - 2:base[s], :] = carry_ref[j - 1, s]

        def conv(r0, cols, cw_ref, cb_ref, wc):
            cur = src[r0:r0 + rb, cols]
            p1 = src[r0 - 1:r0 - 1 + rb, cols]
            p2 = src[r0 - 2:r0 - 2 + rb, cols]
            return (cb_ref[:, wc] + cw_ref[2:3, wc] * cur + cw_ref[1:2, wc] * p1 + cw_ref[0:1, wc] * p2)

        def piece(p, s, r):
            wc = slice(p * FFN_SUB, (p + 1) * FFN_SUB)
            cg = slice(tn + p * FFN_SUB, tn + (p + 1) * FFN_SUB)
            value = conv(base[s] + r, wc, cwv_ref, cbv_ref, wc)
            gate = conv(base[s] + r, cg, cwg_ref, cbg_ref, wc)
            h_ref[s * L + r:s * L + r + rb, wc] = (jax.nn.gelu(gate, approximate=True) * value).astype(BF16)

        def finish():
            for s in range(nb):
                last = src[base[s] + L - 2:base[s] + L, :]
                carry_ref[j - 1, s] = last
                cov_ref[s] = last[:, :tn]
                cog_ref[s] = last[:, tn:]

        pieces = [functools.partial(piece, p, s, r) for p in range(tn // FFN_SUB)
                  for s in range(nb) for r in range(0, L, rb)]
        return history, pieces, finish

    def down_half(c):
        kk = slice(c * tn // 2, (c + 1) * tn // 2)
        w = 512
        for n in range(D_MODEL // w):
            o_ref[:, n * w:(n + 1) * w] += jnp.dot(h_ref[:, kk], wd_ref[kk, n * w:(n + 1) * w],
                                                   preferred_element_type=F32)

    def gate_down(t, c):
        history, pieces, finish = make_gate_pieces(t)
        m = len(pieces) // 2
        if c == 0:
            history()
        for thunk in pieces[c * m:(c + 1) * m]:
            thunk()
        down_half(c)
        if c == 1:
            finish()

    @pl.when(j == 0)
    def _():
        u = (_rms(x_ref[...]) * gpre_ref[...]) * (1.0 + sc_ref[...]) + sh_ref[...]
        u_ref[...] = u.astype(BF16)
        o_ref[...] = jnp.zeros(o_ref.shape, F32)

        @pl.when(i % tpb == 0)
        def _():
            for jj in range(nj):
                carry_ref[jj, :, :, :tn] = cin_ref[:, :, jj * tn:(jj + 1) * tn]
                carry_ref[jj, :, :, tn:] = cin_ref[:, :, D_FF + jj * tn:D_FF + (jj + 1) * tn]

        up_half(0, 0)
        up_half(0, 1)

    def steady(parity):
        up_half(parity, 0)
        gate_down(parity + 1, 0)
        up_half(parity, 1)
        gate_down(parity + 1, 1)

    pl.when((j >= 1) & (j < nj) & (j % 2 == 0))(lambda: steady(0))
    pl.when((j >= 1) & (j < nj) & (j % 2 == 1))(lambda: steady(1))

    @pl.when(j == nj)
    def _():
        gate_down(nj - 1, 0)
        gate_down(nj - 1, 1)
        o_ref[...] = x_ref[...] + gt_ref[...] * (_rms(o_ref[...]) * gpost_ref[...])


def _ffn(x, mod, conv_in, g_pre2, g_post2, w_up, conv_w, conv_b, w_down, tm, nb):
    B, S, _ = x.shape
    R = mod.shape[1]
    tpb = S // tm
    assert S % tm == 0 and (R == 1 or tpb == 1) and (nb == 1 or tpb == 1)
    rm = R if R == 1 else tm
    tn = FFN_TN
    nj = D_FF // tn
    row = lambda i: (i // tpb, i % tpb)
    t_up = lambda j: jnp.minimum(j, nj - 1)
    t_gate = lambda j: jnp.maximum(j - 1, 0)
    conv_b2 = conv_b.reshape(1, 2 * D_FF)
    conv_out = jax.ShapeDtypeStruct((B * tpb * nb, CONV_W - 1, D_FF), F32)
    conv_out_spec = pl.BlockSpec((nb, CONV_W - 1, tn), lambda i, j: (i, 0, t_gate(j)))
    up_rows = nb * (tm // nb + HALO)
    y, co_v, co_g = pl.pallas_call(
        functools.partial(_ffn_kernel, nb=nb, tpb=tpb, nj=nj),
        out_shape=(jax.ShapeDtypeStruct((B, S, D_MODEL), F32), conv_out, conv_out),
        grid=(B * tpb, nj + 1),
        in_specs=[pl.BlockSpec((None, tm, D_MODEL), lambda i, j: (*row(i), 0)),
                  pl.BlockSpec((None, rm, D_MODEL), lambda i, j: (i // tpb, 0, 3)),
                  pl.BlockSpec((None, rm, D_MODEL), lambda i, j: (i // tpb, 0, 4)),
                  pl.BlockSpec((None, rm, D_MODEL), lambda i, j: (i // tpb, 0, 5)),
                  pl.BlockSpec((1, D_MODEL), lambda i, j: (0, 0)),
                  pl.BlockSpec((1, D_MODEL), lambda i, j: (0, 0)),
                  pl.BlockSpec((None, D_MODEL, tn), lambda i, j: (t_up(j), 0, 0)),
                  pl.BlockSpec((None, D_MODEL, tn), lambda i, j: (nj + t_up(j), 0, 0)),
                  pl.BlockSpec((CONV_W, tn), lambda i, j: (0, t_gate(j))),
                  pl.BlockSpec((CONV_W, tn), lambda i, j: (0, nj + t_gate(j))),
                  pl.BlockSpec((1, tn), lambda i, j: (0, t_gate(j))),
                  pl.BlockSpec((1, tn), lambda i, j: (0, nj + t_gate(j))),
                  pl.BlockSpec((tn, D_MODEL), lambda i, j: (t_gate(j), 0)),
                  pl.BlockSpec((nb, CONV_W - 1, 2 * D_FF), lambda i, j: (i // tpb, 0, 0))],
        out_specs=(pl.BlockSpec((None, tm, D_MODEL), lambda i, j: (*row(i), 0)),
                   conv_out_spec, conv_out_spec),
        scratch_shapes=[pltpu.VMEM((tm, D_MODEL), BF16),
                        pltpu.VMEM((up_rows, 2 * tn), F32),
                        pltpu.VMEM((up_rows, 2 * tn), F32),
                        pltpu.VMEM((tm, tn), BF16),
                        pltpu.VMEM((nj, nb, CONV_W - 1, 2 * tn), F32)],
        compiler_params=_params(("arbitrary", "arbitrary")),
        name="conv_ffn",
    )(x, mod, mod, mod, g_pre2.reshape(1, D_MODEL), g_post2.reshape(1, D_MODEL), w_up, w_up,
      conv_w, conv_w, conv_b2, conv_b2, w_down, conv_in)
    return y, jnp.concatenate([co_v, co_g], axis=-1)


def _column_tiles(w, tn):
    k, n = w.shape
    return jnp.swapaxes(w.astype(BF16).reshape(k, n // tn, tn), 0, 1)


def _rotary_tables(pos):
    half = RET_DK // 2
    inv = ROPE_BASE ** (-jnp.arange(half, dtype=F32) / half)
    ang = pos.astype(F32)[:, None] * inv[None, :]
    cos, sin = jnp.cos(ang), jnp.sin(ang)
    return jnp.concatenate([cos, cos], axis=-1), jnp.concatenate([-sin, sin], axis=-1)


def kernel(x_prompt, x_sample, cache_att_k, cache_att_v, state_ret, state_conv, c_prompt, c_sample,
           w_ada, b_ada, g_pre1, w_in, rel_bias, w_br_ret, w_br_att, w_out, g_post1, g_pre2,
           w_up, conv_w, conv_b, w_down, g_post2):
    depth = w_ada.shape[0]
    Bp, Sp, _ = x_prompt.shape
    Bs, Ls, _ = x_sample.shape
    log_gamma = jnp.log(1.0 - 2.0 ** (-5.0 - jnp.arange(N_HEADS, dtype=F32)))
    cos_p, sin_p = _rotary_tables(jnp.arange(Sp))
    cos_s, sin_s = _rotary_tables(PAST_LEN + jnp.arange(Ls))
    cos_s, sin_s = jnp.tile(cos_s, (Bs, 1)), jnp.tile(sin_s, (Bs, 1))
    keep = min(ATT_REACH, Sp)
    tm_p = 512

    yp = x_prompt
    ys = x_sample.reshape(1, Bs * Ls, D_MODEL)
    outs = [[] for _ in range(8)]
    for l in range(depth):
        w_in_b = _column_tiles(w_in[l], PROJ_TN)
        w_up_b = _column_tiles(w_up[l], FFN_TN)
        w_down_b = w_down[l].astype(BF16)
        w_br_ret_b, w_br_att_b, w_out_b = (w_br_ret[l].astype(BF16), w_br_att[l].astype(BF16),
                                           w_out[l].astype(BF16))
        mod = _mod(jnp.concatenate([c_prompt, c_sample], axis=0), w_ada[l], b_ada[l])
        mod_p = mod[:Bp].reshape(Bp, 1, 6 * D_MODEL)
        mod_s = jnp.repeat(mod[Bp:], Ls, axis=0).reshape(1, Bs * Ls, 6 * D_MODEL)

        proj = _in_proj(yp, mod_p, g_pre1[l], w_in_b, cos_p, sin_p, tm=tm_p)
        zero_state = jnp.zeros((Bp, N_HEADS, RET_DK, RET_DV), F32)
        g_ret, r_p = _retention(proj, zero_state, log_gamma, RET_TILE, CHUNK)
        o_att = _attention_prompt(proj, rel_bias[l])
        x1 = _merge(yp, g_ret, o_att, proj, mod_p, g_post1[l], w_br_ret_b, w_br_att_b, w_out_b, tm=tm_p)
        conv0 = jnp.zeros((Bp, CONV_W - 1, 2 * D_FF), F32)
        yp, cv_p = _ffn(x1, mod_p, conv0, g_pre2[l], g_post2[l], w_up_b, conv_w[l], conv_b[l],
                        w_down_b, tm=tm_p, nb=1)
        cv_p = cv_p.reshape(Bp, Sp // tm_p, CONV_W - 1, 2 * D_FF)[:, -1]
        k_p = proj[:, Sp - keep:, OFF_AK:OFF_AV].astype(F32).reshape(Bp, keep, N_HEADS, ATT_DH)
        v_p = proj[:, Sp - keep:, OFF_AV:OFF_GR].astype(F32).reshape(Bp, keep, N_HEADS, ATT_DH)

        proj_s = _in_proj(ys, mod_s, g_pre1[l], w_in_b, cos_s, sin_s, tm=Bs * Ls)
        proj_s3 = proj_s.reshape(Bs, Ls, IN_WIDTH)
        g_ret_s, r_s = _retention(proj_s3, state_ret[l], log_gamma, Ls, Ls)
        o_att_s = _attention_sample(proj_s3, cache_att_k, cache_att_v, l, rel_bias[l])
        x1_s = _merge(ys, g_ret_s.reshape(1, Bs * Ls, -1), o_att_s.reshape(1, Bs * Ls, -1), proj_s,
                      mod_s, g_post1[l], w_br_ret_b, w_br_att_b, w_out_b, tm=Bs * Ls)
        ys, cv_s = _ffn(x1_s, mod_s, state_conv[l], g_pre2[l], g_post2[l], w_up_b, conv_w[l],
                        conv_b[l], w_down_b, tm=Bs * Ls, nb=Bs)
        k_s = proj_s3[:, :, OFF_AK:OFF_AV].astype(F32).reshape(Bs, Ls, N_HEADS, ATT_DH)
        v_s = proj_s3[:, :, OFF_AV:OFF_GR].astype(F32).reshape(Bs, Ls, N_HEADS, ATT_DH)

        for lst, val in zip(outs, (k_p, v_p, r_p, cv_p, k_s, v_s, r_s, cv_s)):
            lst.append(val)
    return (yp, ys.reshape(Bs, Ls, D_MODEL), *[jnp.stack(o) for o in outs])
```

```python
import functools

import jax
import jax.numpy as jnp
from jax import lax
from jax.experimental import pallas as pl
from jax.experimental.pallas import tpu as pltpu

F32 = jnp.float32
BF16 = jnp.bfloat16

D_MODEL = 2048
PAST_LEN = 4096
CHUNK = 64
N_HEADS = 8
RET_DK = 128
RET_DV = 256
ATT_DH = 128
ATT_LEFT_CHUNKS = 8
ATT_REACH = ATT_LEFT_CHUNKS * CHUNK
REL_CLIP = 128
D_FF = 5632
CONV_W = 3
ROPE_BASE = 10000.0
EPS = 1e-6
IN_WIDTH = 13312
NEG = -1e30

OFF_RQ, OFF_RK, OFF_RV, OFF_RG = 0, 1024, 2048, 4096
OFF_AQ, OFF_AK, OFF_AV, OFF_GR, OFF_GA = 6144, 7168, 8192, 9216, 11264
QK_W = N_HEADS * RET_DK
RV_W = N_HEADS * RET_DV

LANE = 128
MXU_N = 256
VMEM_LIMIT = 60 * 1024 * 1024

PROJ_TN = 1024
RET_TILE = 256
ATT_TILE = 256
FFN_TN = 512
FFN_SUB = 128


def _params(sem):
    return pltpu.CompilerParams(dimension_semantics=sem, vmem_limit_bytes=VMEM_LIMIT)


def _rms(x):
    return x * lax.rsqrt(jnp.mean(x * x, axis=-1, keepdims=True) + EPS)


def _const_spec(shape):
    zeros = (0,) * len(shape)
    return pl.BlockSpec(shape, lambda *_: zeros, pipeline_mode=pl.Buffered(1))


def _mod_kernel(c_ref, w_ref, b_ref, o_ref):
    a = jax.nn.silu(c_ref[...]).astype(BF16)
    o_ref[...] = jnp.dot(a, w_ref[...].astype(BF16), preferred_element_type=F32) + b_ref[...]


def _mod(c, w_ada, b_ada):
    n, tn = c.shape[0], 1024
    width = w_ada.shape[1]
    return pl.pallas_call(
        _mod_kernel,
        out_shape=jax.ShapeDtypeStruct((n, width), F32),
        grid=(width // tn,),
        in_specs=[pl.BlockSpec((n, D_MODEL), lambda j: (0, 0)),
                  pl.BlockSpec((D_MODEL, tn), lambda j: (0, j)),
                  pl.BlockSpec((1, tn), lambda j: (0, j))],
        out_specs=pl.BlockSpec((n, tn), lambda j: (0, j)),
        compiler_params=_params(("arbitrary",)),
        name="adaln_mod",
    )(c, w_ada, b_ada.reshape(1, width))


def _inproj_kernel(x_ref, sh_ref, sc_ref, g_ref, w_ref, cos_ref, sin_ref, o_ref, u_ref):
    j = pl.program_id(1)

    @pl.when(j == 0)
    def _():
        u = (_rms(x_ref[...]) * g_ref[...]) * (1.0 + sc_ref[...]) + sh_ref[...]
        u_ref[...] = u.astype(BF16)

    def tile(epilogue):
        for c in range(PROJ_TN // MXU_N):
            cols = slice(c * MXU_N, (c + 1) * MXU_N)
            acc = jnp.dot(u_ref[...], w_ref[:, cols], preferred_element_type=F32)
            o_ref[:, cols] = epilogue(acc).astype(BF16)

    def rotary(acc):
        scale = jnp.where(j == OFF_RK // PROJ_TN, RET_DK ** -0.5, 1.0).astype(F32)
        cos, sin = cos_ref[...], sin_ref[...]
        heads = [acc[:, h * LANE:(h + 1) * LANE] for h in range(MXU_N // LANE)]
        return jnp.concatenate(
            [(a * cos + pltpu.roll(a, LANE // 2, 1) * sin) * scale for a in heads], axis=-1)

    pl.when(j < OFF_RV // PROJ_TN)(lambda: tile(rotary))
    pl.when((j >= OFF_RG // PROJ_TN) & (j < OFF_AQ // PROJ_TN))(lambda: tile(jax.nn.silu))
    pl.when(j >= OFF_GR // PROJ_TN)(lambda: tile(jax.nn.sigmoid))
    pl.when(((j >= OFF_RV // PROJ_TN) & (j < OFF_RG // PROJ_TN))
            | ((j >= OFF_AQ // PROJ_TN) & (j < OFF_GR // PROJ_TN)))(lambda: tile(lambda a: a))


def _in_proj(x, mod, g_pre1, w_in, cos, sin, tm):
    B, S, _ = x.shape
    R = mod.shape[1]
    tpb = S // tm
    assert S % tm == 0 and (R == 1 or tpb == 1)
    rm = R if R == 1 else tm
    return pl.pallas_call(
        _inproj_kernel,
        out_shape=jax.ShapeDtypeStruct((B, S, IN_WIDTH), BF16),
        grid=(B * tpb, IN_WIDTH // PROJ_TN),
        in_specs=[
            pl.BlockSpec((None, tm, D_MODEL), lambda i, j: (i // tpb, i % tpb, 0)),
            pl.BlockSpec((None, rm, D_MODEL), lambda i, j: (i // tpb, 0, 0)),
            pl.BlockSpec((None, rm, D_MODEL), lambda i, j: (i // tpb, 0, 1)),
            pl.BlockSpec((1, D_MODEL), lambda i, j: (0, 0)),
            pl.BlockSpec((D_MODEL, PROJ_TN), lambda i, j: (0, j)),
            pl.BlockSpec((tm, LANE), lambda i, j: (i % tpb, 0)),
            pl.BlockSpec((tm, LANE), lambda i, j: (i % tpb, 0)),
        ],
        out_specs=pl.BlockSpec((None, tm, PROJ_TN), lambda i, j: (i // tpb, i % tpb, j)),
        scratch_shapes=[pltpu.VMEM((tm, D_MODEL), BF16)],
        compiler_params=_params(("arbitrary", "arbitrary")),
        name="in_proj",
    )(x, mod, mod, g_pre1.reshape(1, D_MODEL), w_in, cos, sin)


def _retention_kernel(ds_ref, q_ref, k_ref, v_ref, rg_ref, dmat_ref, dq_ref, dk_ref, s0_ref,
                      o_ref, snew_ref, s_ref):
    t = pl.program_id(1)

    @pl.when(t == 0)
    def _():
        s_ref[...] = s0_ref[...]

    def dots(h):
        q = q_ref[:, h * RET_DK:(h + 1) * RET_DK]
        k = k_ref[:, h * RET_DK:(h + 1) * RET_DK]
        v = v_ref[:, h * RET_DV:(h + 1) * RET_DV]
        state = s_ref[h]
        s = lax.dot_general(q, k, (((1,), (1,)), ((), ())), preferred_element_type=F32)
        qs = jnp.dot(q, state.astype(BF16), preferred_element_type=F32)
        kd = (k.astype(F32) * dk_ref[h]).astype(BF16)
        kv = lax.dot_general(kd, v, (((0,), (0,)), ((), ())), preferred_element_type=F32)
        return s, qs, kv, state, v

    def finish(h, s, qs, kv, state, v):
        o = (jnp.dot((s * dmat_ref[h]).astype(BF16), v, preferred_element_type=F32)
             + qs * dq_ref[h])
        new_state = ds_ref[h] * state + kv
        s_ref[h] = new_state
        cols = slice(h * RET_DV, (h + 1) * RET_DV)
        o_ref[:, cols] = (rg_ref[:, cols].astype(F32) * _rms(o)).astype(BF16)

    pending = dots(0)
    for h in range(1, N_HEADS):
        nxt = dots(h)
        finish(h - 1, *pending)
        pending = nxt
    finish(N_HEADS - 1, *pending)

    @pl.when(t == pl.num_programs(1) - 1)
    def _():
        snew_ref[...] = s_ref[...]


def _retention(proj, state0, log_gamma, tile, chunk):
    B, S, _ = proj.shape
    nt = S // tile
    pos = jnp.arange(tile)
    ch = pos // chunk
    dist = (pos[:, None] - pos[None, :]).astype(F32)
    lg = log_gamma[:, None, None]
    same = ch[:, None] == ch[None, :]
    past = ch[None, :] < ch[:, None]
    dmat = jnp.where(same[None], jnp.exp(lg * jnp.abs(dist)[None]),
                     jnp.where(past[None], jnp.exp(lg * dist[None]), 0.0))
    tf = pos.astype(F32)
    dq = jnp.broadcast_to(jnp.exp((tf[None, :] + 1.0) * log_gamma[:, None])[:, :, None],
                          (N_HEADS, tile, RET_DV))
    dk = jnp.broadcast_to(jnp.exp((tile - 1.0 - tf)[None, :] * log_gamma[:, None])[:, :, None],
                          (N_HEADS, tile, RET_DK))
    ds = jnp.exp(tile * log_gamma)
    state_spec = pl.BlockSpec((None, N_HEADS, RET_DK, RET_DV), lambda b, t: (b, 0, 0, 0))
    return pl.pallas_call(
        _retention_kernel,
        out_shape=(jax.ShapeDtypeStruct((B, S, RV_W), BF16),
                   jax.ShapeDtypeStruct((B, N_HEADS, RET_DK, RET_DV), F32)),
        grid=(B, nt),
        in_specs=[
            pl.BlockSpec(memory_space=pltpu.SMEM),
            pl.BlockSpec((None, tile, QK_W), lambda b, t: (b, t, OFF_RQ // QK_W)),
            pl.BlockSpec((None, tile, QK_W), lambda b, t: (b, t, OFF_RK // QK_W)),
            pl.BlockSpec((None, tile, RV_W), lambda b, t: (b, t, OFF_RV // RV_W)),
            pl.BlockSpec((None, tile, RV_W), lambda b, t: (b, t, OFF_RG // RV_W)),
            _const_spec(dmat.shape), _const_spec(dq.shape), _const_spec(dk.shape),
            state_spec,
        ],
        out_specs=(pl.BlockSpec((None, tile, RV_W), lambda b, t: (b, t, 0)), state_spec),
        scratch_shapes=[pltpu.VMEM((N_HEADS, RET_DK, RET_DV), F32)],
        compiler_params=_params(("arbitrary", "arbitrary")),
        name="retention",
    )(ds, proj, proj, proj, proj, dmat, dq, dk, state0)


def _qk(q, k):
    return lax.dot_general(q, k, (((1,), (1,)), ((), ())), preferred_element_type=F32) * (ATT_DH ** -0.5)


def _softmax_pv(scores, values):
    m = functools.reduce(jnp.maximum, [jnp.max(s, axis=-1, keepdims=True) for s in scores])
    ps = [jnp.exp(s - m) for s in scores]
    l = functools.reduce(jnp.add, [jnp.sum(p, axis=-1, keepdims=True) for p in ps])
    o = functools.reduce(jnp.add, [jnp.dot(p.astype(BF16), v, preferred_element_type=F32)
                                   for p, v in zip(ps, values)])
    return o / l


def _per_head_pipelined(scores_fn, finish_fn):
    pending = scores_fn(0)
    for h in range(1, N_HEADS):
        nxt = scores_fn(h)
        finish_fn(h - 1, pending)
        pending = nxt
    finish_fn(N_HEADS - 1, pending)


def _bias_table_kernel(base_ref, o_ref):
    T, W = o_ref.shape
    n = base_ref.shape[-1]
    x = jnp.broadcast_to(base_ref[...], (T, n))
    x = pltpu.roll(x, n - T + 1, 1, stride=1, stride_axis=0)[:, :W]
    qc = lax.broadcasted_iota(jnp.int32, (T, W), 0) // CHUNK
    kc = lax.broadcasted_iota(jnp.int32, (T, W), 1) // CHUNK
    o_ref[...] = jnp.where((kc >= qc) & (kc <= qc + ATT_LEFT_CHUNKS), x, NEG)


def _bias_table(rel_bias):
    T = ATT_TILE
    W = (ATT_REACH // T + 1) * T
    n = 1024
    assert W + T - 1 <= n
    lo = (T - 1) + ATT_REACH - REL_CLIP
    hi = n - lo - (2 * REL_CLIP + 1)
    assert hi >= 0
    base = jnp.concatenate([jnp.broadcast_to(rel_bias[:, :1], (N_HEADS, lo)), rel_bias,
                            jnp.broadcast_to(rel_bias[:, -1:], (N_HEADS, hi))], axis=1)
    return pl.pallas_call(
        _bias_table_kernel,
        out_shape=jax.ShapeDtypeStruct((N_HEADS, T, W), F32),
        grid=(N_HEADS,),
        in_specs=[pl.BlockSpec((None, 1, n), lambda h: (h, 0, 0))],
        out_specs=pl.BlockSpec((None, T, W), lambda h: (h, 0, 0)),
        compiler_params=_params(("arbitrary",)),
        name="bias_table",
    )(base.reshape(N_HEADS, 1, n).astype(F32))


def _att_prompt_kernel(q_ref, k0_ref, k1_ref, k2_ref, v0_ref, v1_ref, v2_ref, tab_ref, o_ref):
    i = pl.program_id(1)
    T = ATT_TILE

    def scores(h):
        cols = slice(h * ATT_DH, (h + 1) * ATT_DH)
        q = q_ref[:, cols]
        s0 = _qk(q, k0_ref[:, cols]) + tab_ref[h, :, 0:T]
        s1 = _qk(q, k1_ref[:, cols]) + tab_ref[h, :, T:2 * T]
        s2 = _qk(q, k2_ref[:, cols]) + tab_ref[h, :, 2 * T:3 * T]
        return [jnp.where(i >= 2, s0, NEG), jnp.where(i >= 1, s1, NEG), s2]

    def finish(h, ss):
        cols = slice(h * ATT_DH, (h + 1) * ATT_DH)
        o = _softmax_pv(ss, [v0_ref[:, cols], v1_ref[:, cols], v2_ref[:, cols]])
        o_ref[:, cols] = o.astype(BF16)

    _per_head_pipelined(scores, finish)


def _attention_prompt(proj, rel_bias):
    B, S, _ = proj.shape
    T = ATT_TILE
    nq = S // T
    assert ATT_REACH // T + 1 == 3
    table = _bias_table(rel_bias)

    def blk(off, back):
        return pl.BlockSpec((None, T, QK_W), lambda b, i: (b, jnp.maximum(i - back, 0), off // QK_W))

    return pl.pallas_call(
        _att_prompt_kernel,
        out_shape=jax.ShapeDtypeStruct((B, S, QK_W), BF16),
        grid=(B, nq),
        in_specs=[blk(OFF_AQ, 0),
                  blk(OFF_AK, 2), blk(OFF_AK, 1), blk(OFF_AK, 0),
                  blk(OFF_AV, 2), blk(OFF_AV, 1), blk(OFF_AV, 0),
                  _const_spec(table.shape)],
        out_specs=pl.BlockSpec((None, T, QK_W), lambda b, i: (b, i, 0)),
        compiler_params=_params(("arbitrary", "arbitrary")),
        name="attention_prompt",
    )(proj, proj, proj, proj, proj, proj, proj, table)


def _att_sample_kernel(q_ref, kn_ref, vn_ref, kc_ref, vc_ref, tab_ref, o_ref):
    P = kc_ref.shape[0]

    def scores(h):
        cols = slice(h * ATT_DH, (h + 1) * ATT_DH)
        q = q_ref[:, cols]
        sc = _qk(q, kc_ref[:, h, :].astype(BF16)) + tab_ref[h, :, 0:P]
        sn = _qk(q, kn_ref[:, cols]) + tab_ref[h, :, P:]
        return [sc, sn]

    def finish(h, ss):
        cols = slice(h * ATT_DH, (h + 1) * ATT_DH)
        o = _softmax_pv(ss, [vc_ref[:, h, :].astype(BF16), vn_ref[:, cols]])
        o_ref[:, cols] = o.astype(BF16)

    _per_head_pipelined(scores, finish)


def _attention_sample(proj, cache_k, cache_v, layer, rel_bias):
    B, L, _ = proj.shape
    P = cache_k.shape[2]
    assert P == ATT_REACH and PAST_LEN >= ATT_REACH
    t = jnp.arange(L)
    rel = jnp.concatenate([jnp.arange(P)[None, :] - P - t[:, None], t[None, :] - t[:, None]], axis=1)
    table = rel_bias[:, jnp.clip(rel, -REL_CLIP, REL_CLIP) + REL_CLIP].astype(F32)
    cache_spec = pl.BlockSpec((None, None, P, N_HEADS, ATT_DH), lambda b: (layer, b, 0, 0, 0))
    return pl.pallas_call(
        _att_sample_kernel,
        out_shape=jax.ShapeDtypeStruct((B, L, QK_W), BF16),
        grid=(B,),
        in_specs=[pl.BlockSpec((None, L, QK_W), lambda b: (b, 0, OFF_AQ // QK_W)),
                  pl.BlockSpec((None, L, QK_W), lambda b: (b, 0, OFF_AK // QK_W)),
                  pl.BlockSpec((None, L, QK_W), lambda b: (b, 0, OFF_AV // QK_W)),
                  cache_spec, cache_spec,
                  _const_spec(table.shape)],
        out_specs=pl.BlockSpec((None, L, QK_W), lambda b: (b, 0, 0)),
        compiler_params=_params(("arbitrary",)),
        name="attention_sample",
    )(proj, proj, proj, cache_k, cache_v, table)


def _merge_kernel(gret_ref, oatt_ref, gr0_ref, gr1_ref, ga0_ref, ga1_ref, x_ref, gt_ref, g_ref,
                  wr_ref, wa_ref, wo_ref, o_ref, m_ref):
    y_ret = jnp.dot(gret_ref[...], wr_ref[...], preferred_element_type=F32)
    y_att = jnp.dot(oatt_ref[...], wa_ref[...], preferred_element_type=F32)
    half = D_MODEL // 2
    m_ref[:, :half] = (gr0_ref[...].astype(F32) * y_ret[:, :half]
                       + ga0_ref[...].astype(F32) * y_att[:, :half]).astype(BF16)
    m_ref[:, half:] = (gr1_ref[...].astype(F32) * y_ret[:, half:]
                       + ga1_ref[...].astype(F32) * y_att[:, half:]).astype(BF16)
    z = jnp.dot(m_ref[...], wo_ref[...], preferred_element_type=F32)
    o_ref[...] = x_ref[...] + gt_ref[...] * (_rms(z) * g_ref[...])


def _merge(x, g_ret, o_att, proj, mod, g_post1, w_br_ret, w_br_att, w_out, tm):
    B, S, _ = x.shape
    R = mod.shape[1]
    tpb = S // tm
    assert S % tm == 0 and (R == 1 or tpb == 1)
    rm = R if R == 1 else tm
    half = D_MODEL // 2
    row = lambda i: (i // tpb, i % tpb)

    def gate_spec(off):
        return pl.BlockSpec((None, tm, half), lambda i: (*row(i), off // half))

    return pl.pallas_call(
        _merge_kernel,
        out_shape=jax.ShapeDtypeStruct((B, S, D_MODEL), F32),
        grid=(B * tpb,),
        in_specs=[pl.BlockSpec((None, tm, RV_W), lambda i: (*row(i), 0)),
                  pl.BlockSpec((None, tm, QK_W), lambda i: (*row(i), 0)),
                  gate_spec(OFF_GR), gate_spec(OFF_GR + half),
                  gate_spec(OFF_GA), gate_spec(OFF_GA + half),
                  pl.BlockSpec((None, tm, D_MODEL), lambda i: (*row(i), 0)),
                  pl.BlockSpec((None, rm, D_MODEL), lambda i: (i // tpb, 0, 2)),
                  pl.BlockSpec((1, D_MODEL), lambda i: (0, 0)),
                  _const_spec(w_br_ret.shape), _const_spec(w_br_att.shape), _const_spec(w_out.shape)],
        out_specs=pl.BlockSpec((None, tm, D_MODEL), lambda i: (*row(i), 0)),
        scratch_shapes=[pltpu.VMEM((tm, D_MODEL), BF16)],
        compiler_params=_params(("arbitrary",)),
        name="merge_out",
    )(g_ret, o_att, proj, proj, proj, proj, x, mod, g_post1.reshape(1, D_MODEL),
      w_br_ret, w_br_att, w_out)


HALO = 8


def _ffn_kernel(x_ref, sh_ref, sc_ref, gt_ref, gpre_ref, gpost_ref, wv_ref, wg_ref, cwv_ref, cwg_ref,
                cbv_ref, cbg_ref, wd_ref, cin_ref, o_ref, cov_ref, cog_ref,
                u_ref, upa_ref, upb_ref, h_ref, carry_ref, *, nb, tpb, nj):
    i, j = pl.program_id(0), pl.program_id(1)
    tn = FFN_TN
    L = x_ref.shape[0] // nb
    ups = (upa_ref, upb_ref)
    base = [s * (L + HALO) + HALO for s in range(nb)]
    rb = min(L, 64)

    def up_half(t, half):
        acc = jnp.dot(u_ref[...], (wv_ref, wg_ref)[half][...], preferred_element_type=F32)
        for s in range(nb):
            ups[t % 2][base[s]:base[s] + L, half * tn:(half + 1) * tn] = acc[s * L:(s + 1) * L]

    def make_gate_pieces(t):
        src = ups[t % 2]

        def history():
            for s in range(nb):
                src[base[s] - 2:base[s], :] = carry_ref[j - 1, s]

        def conv(r0, cols, cw_ref, cb_ref, wc):
            cur = src[r0:r0 + rb, cols]
            p1 = src[r0 - 1:r0 - 1 + rb, cols]
            p2 = src[r0 - 2:r0 - 2 + rb, cols]
            return (cb_ref[:, wc] + cw_ref[2:3, wc] * cur + cw_ref[1:2, wc] * p1 + cw_ref[0:1, wc] * p2)

        def piece(p, s, r):
            wc = slice(p * FFN_SUB, (p + 1) * FFN_SUB)
            cg = slice(tn + p * FFN_SUB, tn + (p + 1) * FFN_SUB)
            value = conv(base[s] + r, wc, cwv_ref, cbv_ref, wc)
            gate = conv(base[s] + r, cg, cwg_ref, cbg_ref, wc)
            h_ref[s * L + r:s * L + r + rb, wc] = (jax.nn.gelu(gate, approximate=True) * value).astype(BF16)

        def finish():
            for s in range(nb):
                last = src[base[s] + L - 2:base[s] + L, :]
                carry_ref[j - 1, s] = last
                cov_ref[s] = last[:, :tn]
                cog_ref[s] = last[:, tn:]

        pieces = [functools.partial(piece, p, s, r) for p in range(tn // FFN_SUB)
                  for s in range(nb) for r in range(0, L, rb)]
        return history, pieces, finish

    def down_half(c):
        kk = slice(c * tn // 2, (c + 1) * tn // 2)
        w = 512
        for n in range(D_MODEL // w):
            o_ref[:, n * w:(n + 1) * w] += jnp.dot(h_ref[:, kk], wd_ref[kk, n * w:(n + 1) * w],
                                                   preferred_element_type=F32)

    def gate_down(t, c):
        history, pieces, finish = make_gate_pieces(t)
        m = len(pieces) // 2
        if c == 0:
            history()
        for thunk in pieces[c * m:(c + 1) * m]:
            thunk()
        down_half(c)
        if c == 1:
            finish()

    @pl.when(j == 0)
    def _():
        u = (_rms(x_ref[...]) * gpre_ref[...]) * (1.0 + sc_ref[...]) + sh_ref[...]
        u_ref[...] = u.astype(BF16)
        o_ref[...] = jnp.zeros(o_ref.shape, F32)

        @pl.when(i % tpb == 0)
        def _():
            for jj in range(nj):
                carry_ref[jj, :, :, :tn] = cin_ref[:, :, jj * tn:(jj + 1) * tn]
                carry_ref[jj, :, :, tn:] = cin_ref[:, :, D_FF + jj * tn:D_FF + (jj + 1) * tn]

        up_half(0, 0)
        up_half(0, 1)

    def steady(parity):
        up_half(parity, 0)
        gate_down(parity + 1, 0)
        up_half(parity, 1)
        gate_down(parity + 1, 1)

    pl.when((j >= 1) & (j < nj) & (j % 2 == 0))(lambda: steady(0))
    pl.when((j >= 1) & (j < nj) & (j % 2 == 1))(lambda: steady(1))

    @pl.when(j == nj)
    def _():
        gate_down(nj - 1, 0)
        gate_down(nj - 1, 1)
        o_ref[...] = x_ref[...] + gt_ref[...] * (_rms(o_ref[...]) * gpost_ref[...])


def _ffn(x, mod, conv_in, g_pre2, g_post2, w_up, conv_w, conv_b, w_down, tm, nb):
    B, S, _ = x.shape
    R = mod.shape[1]
    tpb = S // tm
    assert S % tm == 0 and (R == 1 or tpb == 1) and (nb == 1 or tpb == 1)
    rm = R if R == 1 else tm
    tn = FFN_TN
    nj = D_FF // tn
    row = lambda i: (i // tpb, i % tpb)
    t_up = lambda j: jnp.minimum(j, nj - 1)
    t_gate = lambda j: jnp.maximum(j - 1, 0)
    conv_b2 = conv_b.reshape(1, 2 * D_FF)
    conv_out = jax.ShapeDtypeStruct((B * tpb * nb, CONV_W - 1, D_FF), F32)
    conv_out_spec = pl.BlockSpec((nb, CONV_W - 1, tn), lambda i, j: (i, 0, t_gate(j)))
    up_rows = nb * (tm // nb + HALO)
    y, co_v, co_g = pl.pallas_call(
        functools.partial(_ffn_kernel, nb=nb, tpb=tpb, nj=nj),
        out_shape=(jax.ShapeDtypeStruct((B, S, D_MODEL), F32), conv_out, conv_out),
        grid=(B * tpb, nj + 1),
        in_specs=[pl.BlockSpec((None, tm, D_MODEL), lambda i, j: (*row(i), 0),
                               pipeline_mode=pl.Buffered(1)),
                  pl.BlockSpec((None, rm, D_MODEL), lambda i, j: (i // tpb, 0, 3)),
                  pl.BlockSpec((None, rm, D_MODEL), lambda i, j: (i // tpb, 0, 4)),
                  pl.BlockSpec((None, rm, D_MODEL), lambda i, j: (i // tpb, 0, 5)),
                  pl.BlockSpec((1, D_MODEL), lambda i, j: (0, 0)),
                  pl.BlockSpec((1, D_MODEL), lambda i, j: (0, 0)),
                  pl.BlockSpec((D_MODEL, tn), lambda i, j: (0, t_up(j))),
                  pl.BlockSpec((D_MODEL, tn), lambda i, j: (0, nj + t_up(j))),
                  pl.BlockSpec((CONV_W, tn), lambda i, j: (0, t_gate(j))),
                  pl.BlockSpec((CONV_W, tn), lambda i, j: (0, nj + t_gate(j))),
                  pl.BlockSpec((1, tn), lambda i, j: (0, t_gate(j))),
                  pl.BlockSpec((1, tn), lambda i, j: (0, nj + t_gate(j))),
                  pl.BlockSpec((tn, D_MODEL), lambda i, j: (t_gate(j), 0)),
                  pl.BlockSpec((nb, CONV_W - 1, 2 * D_FF), lambda i, j: (i // tpb, 0, 0))],
        out_specs=(pl.BlockSpec((None, tm, D_MODEL), lambda i, j: (*row(i), 0)),
                   conv_out_spec, conv_out_spec),
        scratch_shapes=[pltpu.VMEM((tm, D_MODEL), BF16),
                        pltpu.VMEM((up_rows, 2 * tn), F32),
                        pltpu.VMEM((up_rows, 2 * tn), F32),
                        pltpu.VMEM((tm, tn), BF16),
                        pltpu.VMEM((nj, nb, CONV_W - 1, 2 * tn), F32)],
        compiler_params=_params(("arbitrary", "arbitrary")),
        name="conv_ffn",
    )(x, mod, mod, mod, g_pre2.reshape(1, D_MODEL), g_post2.reshape(1, D_MODEL), w_up, w_up,
      conv_w, conv_w, conv_b2, conv_b2, w_down, conv_in)
    return y, jnp.concatenate([co_v, co_g], axis=-1)


def _rotary_tables(pos):
    half = RET_DK // 2
    inv = ROPE_BASE ** (-jnp.arange(half, dtype=F32) / half)
    ang = pos.astype(F32)[:, None] * inv[None, :]
    cos, sin = jnp.cos(ang), jnp.sin(ang)
    return jnp.concatenate([cos, cos], axis=-1), jnp.concatenate([-sin, sin], axis=-1)


def kernel(x_prompt, x_sample, cache_att_k, cache_att_v, state_ret, state_conv, c_prompt, c_sample,
           w_ada, b_ada, g_pre1, w_in, rel_bias, w_br_ret, w_br_att, w_out, g_post1, g_pre2,
           w_up, conv_w, conv_b, w_down, g_post2):
    depth = w_ada.shape[0]
    Bp, Sp, _ = x_prompt.shape
    Bs, Ls, _ = x_sample.shape
    log_gamma = jnp.log(1.0 - 2.0 ** (-5.0 - jnp.arange(N_HEADS, dtype=F32)))
    cos_p, sin_p = _rotary_tables(jnp.arange(Sp))
    cos_s, sin_s = _rotary_tables(PAST_LEN + jnp.arange(Ls))
    cos_s, sin_s = jnp.tile(cos_s, (Bs, 1)), jnp.tile(sin_s, (Bs, 1))
    keep = min(ATT_REACH, Sp)
    tm_p, tm_wide = 512, 1024

    yp = x_prompt
    ys = x_sample.reshape(1, Bs * Ls, D_MODEL)
    outs = [[] for _ in range(8)]
    for l in range(depth):
        w_in_b, w_up_b, w_down_b = w_in[l].astype(BF16), w_up[l].astype(BF16), w_down[l].astype(BF16)
        w_br_ret_b, w_br_att_b, w_out_b = (w_br_ret[l].astype(BF16), w_br_att[l].astype(BF16),
                                           w_out[l].astype(BF16))
        mod = _mod(jnp.concatenate([c_prompt, c_sample], axis=0), w_ada[l], b_ada[l])
        mod_p = mod[:Bp].reshape(Bp, 1, 6 * D_MODEL)
        mod_s = jnp.repeat(mod[Bp:], Ls, axis=0).reshape(1, Bs * Ls, 6 * D_MODEL)

        proj = _in_proj(yp, mod_p, g_pre1[l], w_in_b, cos_p, sin_p, tm=tm_wide)
        zero_state = jnp.zeros((Bp, N_HEADS, RET_DK, RET_DV), F32)
        g_ret, r_p = _retention(proj, zero_state, log_gamma, RET_TILE, CHUNK)
        o_att = _attention_prompt(proj, rel_bias[l])
        x1 = _merge(yp, g_ret, o_att, proj, mod_p, g_post1[l], w_br_ret_b, w_br_att_b, w_out_b, tm=tm_p)
        conv0 = jnp.zeros((Bp, CONV_W - 1, 2 * D_FF), F32)
        yp, cv_p = _ffn(x1, mod_p, conv0, g_pre2[l], g_post2[l], w_up_b, conv_w[l], conv_b[l],
                        w_down_b, tm=tm_wide, nb=1)
        cv_p = cv_p.reshape(Bp, Sp // tm_wide, CONV_W - 1, 2 * D_FF)[:, -1]
        k_p = proj[:, Sp - keep:, OFF_AK:OFF_AV].astype(F32).reshape(Bp, keep, N_HEADS, ATT_DH)
        v_p = proj[:, Sp - keep:, OFF_AV:OFF_GR].astype(F32).reshape(Bp, keep, N_HEADS, ATT_DH)

        proj_s = _in_proj(ys, mod_s, g_pre1[l], w_in_b, cos_s, sin_s, tm=Bs * Ls)
        proj_s3 = proj_s.reshape(Bs, Ls, IN_WIDTH)
        g_ret_s, r_s = _retention(proj_s3, state_ret[l], log_gamma, Ls, Ls)
        o_att_s = _attention_sample(proj_s3, cache_att_k, cache_att_v, l, rel_bias[l])
        x1_s = _merge(ys, g_ret_s.reshape(1, Bs * Ls, -1), o_att_s.reshape(1, Bs * Ls, -1), proj_s,
                      mod_s, g_post1[l], w_br_ret_b, w_br_att_b, w_out_b, tm=Bs * Ls)
        ys, cv_s = _ffn(x1_s, mod_s, state_conv[l], g_pre2[l], g_post2[l], w_up_b, conv_w[l],
                        conv_b[l], w_down_b, tm=Bs * Ls, nb=Bs)
        k_s = proj_s3[:, :, OFF_AK:OFF_AV].astype(F32).reshape(Bs, Ls, N_HEADS, ATT_DH)
        v_s = proj_s3[:, :, OFF_AV:OFF_GR].astype(F32).reshape(Bs, Ls, N_HEADS, ATT_DH)

        for lst, val in zip(outs, (k_p, v_p, r_p, cv_p, k_s, v_s, r_s, cv_s)):
            lst.append(val)
    return (yp, ys.reshape(Bs, Ls, D_MODEL), *[jnp.stack(o) for o in outs])
```

```python
import functools

import jax
import jax.numpy as jnp
from jax import lax
from jax.experimental import pallas as pl
from jax.experimental.pallas import tpu as pltpu

F32 = jnp.float32
BF16 = jnp.bfloat16

D_MODEL = 2048
PAST_LEN = 4096
CHUNK = 64
N_HEADS = 8
RET_DK = 128
RET_DV = 256
ATT_DH = 128
ATT_LEFT_CHUNKS = 8
ATT_REACH = ATT_LEFT_CHUNKS * CHUNK
REL_CLIP = 128
D_FF = 5632
CONV_W = 3
ROPE_BASE = 10000.0
EPS = 1e-6
IN_WIDTH = 13312
NEG = -1e30

OFF_RQ, OFF_RK, OFF_RV, OFF_RG = 0, 1024, 2048, 4096
OFF_AQ, OFF_AK, OFF_AV, OFF_GR, OFF_GA = 6144, 7168, 8192, 9216, 11264
QK_W = N_HEADS * RET_DK
RV_W = N_HEADS * RET_DV

LANE = 128
MXU_N = 256
VMEM_LIMIT = 56 * 1024 * 1024

PROJ_TN = 1024
RET_TILE = 256
ATT_TILE = 256
FFN_TN = 512
FFN_SUB = 128


def _params(sem):
    return pltpu.CompilerParams(dimension_semantics=sem, vmem_limit_bytes=VMEM_LIMIT)


def _rms(x):
    return x * lax.rsqrt(jnp.mean(x * x, axis=-1, keepdims=True) + EPS)


def _const_spec(shape):
    zeros = (0,) * len(shape)
    return pl.BlockSpec(shape, lambda *_: zeros, pipeline_mode=pl.Buffered(1))


def _mod_kernel(c_ref, w_ref, b_ref, o_ref):
    a = jax.nn.silu(c_ref[...]).astype(BF16)
    o_ref[...] = jnp.dot(a, w_ref[...].astype(BF16), preferred_element_type=F32) + b_ref[...]


def _mod(c, w_ada, b_ada):
    n, tn = c.shape[0], 1024
    width = w_ada.shape[1]
    return pl.pallas_call(
        _mod_kernel,
        out_shape=jax.ShapeDtypeStruct((n, width), F32),
        grid=(width // tn,),
        in_specs=[pl.BlockSpec((n, D_MODEL), lambda j: (0, 0)),
                  pl.BlockSpec((D_MODEL, tn), lambda j: (0, j)),
                  pl.BlockSpec((1, tn), lambda j: (0, j))],
        out_specs=pl.BlockSpec((n, tn), lambda j: (0, j)),
        compiler_params=_params(("arbitrary",)),
        name="adaln_mod",
    )(c, w_ada, b_ada.reshape(1, width))


def _inproj_kernel(x_ref, sh_ref, sc_ref, g_ref, w_ref, cos_ref, sin_ref, o_ref, u_ref):
    j = pl.program_id(1)

    @pl.when(j == 0)
    def _():
        u = (_rms(x_ref[...]) * g_ref[...]) * (1.0 + sc_ref[...]) + sh_ref[...]
        u_ref[...] = u.astype(BF16)

    def tile(epilogue):
        for c in range(PROJ_TN // MXU_N):
            cols = slice(c * MXU_N, (c + 1) * MXU_N)
            acc = jnp.dot(u_ref[...], w_ref[:, cols], preferred_element_type=F32)
            o_ref[:, cols] = epilogue(acc).astype(BF16)

    def rotary(acc):
        scale = jnp.where(j == OFF_RK // PROJ_TN, RET_DK ** -0.5, 1.0).astype(F32)
        cos, sin = cos_ref[...], sin_ref[...]
        heads = [acc[:, h * LANE:(h + 1) * LANE] for h in range(MXU_N // LANE)]
        return jnp.concatenate(
            [(a * cos + pltpu.roll(a, LANE // 2, 1) * sin) * scale for a in heads], axis=-1)

    pl.when(j < OFF_RV // PROJ_TN)(lambda: tile(rotary))
    pl.when((j >= OFF_RG // PROJ_TN) & (j < OFF_AQ // PROJ_TN))(lambda: tile(jax.nn.silu))
    pl.when(j >= OFF_GR // PROJ_TN)(lambda: tile(jax.nn.sigmoid))
    pl.when(((j >= OFF_RV // PROJ_TN) & (j < OFF_RG // PROJ_TN))
            | ((j >= OFF_AQ // PROJ_TN) & (j < OFF_GR // PROJ_TN)))(lambda: tile(lambda a: a))


def _in_proj(x, mod, g_pre1, w_in, cos, sin, tm):
    B, S, _ = x.shape
    R = mod.shape[1]
    tpb = S // tm
    assert S % tm == 0 and (R == 1 or tpb == 1)
    rm = R if R == 1 else tm
    return pl.pallas_call(
        _inproj_kernel,
        out_shape=jax.ShapeDtypeStruct((B, S, IN_WIDTH), BF16),
        grid=(B * tpb, IN_WIDTH // PROJ_TN),
        in_specs=[
            pl.BlockSpec((None, tm, D_MODEL), lambda i, j: (i // tpb, i % tpb, 0)),
            pl.BlockSpec((None, rm, D_MODEL), lambda i, j: (i // tpb, 0, 0)),
            pl.BlockSpec((None, rm, D_MODEL), lambda i, j: (i // tpb, 0, 1)),
            pl.BlockSpec((1, D_MODEL), lambda i, j: (0, 0)),
            pl.BlockSpec((D_MODEL, PROJ_TN), lambda i, j: (0, j)),
            pl.BlockSpec((tm, LANE), lambda i, j: (i % tpb, 0)),
            pl.BlockSpec((tm, LANE), lambda i, j: (i % tpb, 0)),
        ],
        out_specs=pl.BlockSpec((None, tm, PROJ_TN), lambda i, j: (i // tpb, i % tpb, j)),
        scratch_shapes=[pltpu.VMEM((tm, D_MODEL), BF16)],
        compiler_params=_params(("arbitrary", "arbitrary")),
        name="in_proj",
    )(x, mod, mod, g_pre1.reshape(1, D_MODEL), w_in, cos, sin)


def _retention_kernel(ds_ref, q_ref, k_ref, v_ref, rg_ref, dmat_ref, dq_ref, dk_ref, s0_ref,
                      o_ref, snew_ref, s_ref):
    t = pl.program_id(1)

    @pl.when(t == 0)
    def _():
        s_ref[...] = s0_ref[...]

    def dots(h):
        q = q_ref[:, h * RET_DK:(h + 1) * RET_DK]
        k = k_ref[:, h * RET_DK:(h + 1) * RET_DK]
        v = v_ref[:, h * RET_DV:(h + 1) * RET_DV]
        state = s_ref[h]
        s = lax.dot_general(q, k, (((1,), (1,)), ((), ())), preferred_element_type=F32)
        qs = jnp.dot(q, state.astype(BF16), preferred_element_type=F32)
        kd = (k.astype(F32) * dk_ref[h]).astype(BF16)
        kv = lax.dot_general(kd, v, (((0,), (0,)), ((), ())), preferred_element_type=F32)
        return s, qs, kv, state, v

    def finish(h, s, qs, kv, state, v):
        o = (jnp.dot((s * dmat_ref[h]).astype(BF16), v, preferred_element_type=F32)
             + qs * dq_ref[h])
        new_state = ds_ref[h] * state + kv
        s_ref[h] = new_state
        cols = slice(h * RET_DV, (h + 1) * RET_DV)
        o_ref[:, cols] = (rg_ref[:, cols].astype(F32) * _rms(o)).astype(BF16)

    pending = dots(0)
    for h in range(1, N_HEADS):
        nxt = dots(h)
        finish(h - 1, *pending)
        pending = nxt
    finish(N_HEADS - 1, *pending)

    @pl.when(t == pl.num_programs(1) - 1)
    def _():
        snew_ref[...] = s_ref[...]


def _retention(proj, state0, log_gamma, tile, chunk):
    B, S, _ = proj.shape
    nt = S // tile
    pos = jnp.arange(tile)
    ch = pos // chunk
    dist = (pos[:, None] - pos[None, :]).astype(F32)
    lg = log_gamma[:, None, None]
    same = ch[:, None] == ch[None, :]
    past = ch[None, :] < ch[:, None]
    dmat = jnp.where(same[None], jnp.exp(lg * jnp.abs(dist)[None]),
                     jnp.where(past[None], jnp.exp(lg * dist[None]), 0.0))
    tf = pos.astype(F32)
    dq = jnp.broadcast_to(jnp.exp((tf[None, :] + 1.0) * log_gamma[:, None])[:, :, None],
                          (N_HEADS, tile, RET_DV))
    dk = jnp.broadcast_to(jnp.exp((tile - 1.0 - tf)[None, :] * log_gamma[:, None])[:, :, None],
                          (N_HEADS, tile, RET_DK))
    ds = jnp.exp(tile * log_gamma)
    state_spec = pl.BlockSpec((None, N_HEADS, RET_DK, RET_DV), lambda b, t: (b, 0, 0, 0))
    return pl.pallas_call(
        _retention_kernel,
        out_shape=(jax.ShapeDtypeStruct((B, S, RV_W), BF16),
                   jax.ShapeDtypeStruct((B, N_HEADS, RET_DK, RET_DV), F32)),
        grid=(B, nt),
        in_specs=[
            pl.BlockSpec(memory_space=pltpu.SMEM),
            pl.BlockSpec((None, tile, QK_W), lambda b, t: (b, t, OFF_RQ // QK_W)),
            pl.BlockSpec((None, tile, QK_W), lambda b, t: (b, t, OFF_RK // QK_W)),
            pl.BlockSpec((None, tile, RV_W), lambda b, t: (b, t, OFF_RV // RV_W)),
            pl.BlockSpec((None, tile, RV_W), lambda b, t: (b, t, OFF_RG // RV_W)),
            _const_spec(dmat.shape), _const_spec(dq.shape), _const_spec(dk.shape),
            state_spec,
        ],
        out_specs=(pl.BlockSpec((None, tile, RV_W), lambda b, t: (b, t, 0)), state_spec),
        scratch_shapes=[pltpu.VMEM((N_HEADS, RET_DK, RET_DV), F32)],
        compiler_params=_params(("arbitrary", "arbitrary")),
        name="retention",
    )(ds, proj, proj, proj, proj, dmat, dq, dk, state0)


def _qk(q, k):
    return lax.dot_general(q, k, (((1,), (1,)), ((), ())), preferred_element_type=F32) * (ATT_DH ** -0.5)


def _softmax_pv(scores, values):
    m = functools.reduce(jnp.maximum, [jnp.max(s, axis=-1, keepdims=True) for s in scores])
    ps = [jnp.exp(s - m) for s in scores]
    l = functools.reduce(jnp.add, [jnp.sum(p, axis=-1, keepdims=True) for p in ps])
    o = functools.reduce(jnp.add, [jnp.dot(p.astype(BF16), v, preferred_element_type=F32)
                                   for p, v in zip(ps, values)])
    return o / l


def _per_head_pipelined(scores_fn, finish_fn):
    pending = scores_fn(0)
    for h in range(1, N_HEADS):
        nxt = scores_fn(h)
        finish_fn(h - 1, pending)
        pending = nxt
    finish_fn(N_HEADS - 1, pending)


def _bias_table_kernel(base_ref, o_ref):
    T, W = o_ref.shape
    n = base_ref.shape[-1]
    x = jnp.broadcast_to(base_ref[...], (T, n))
    x = pltpu.roll(x, n - T + 1, 1, stride=1, stride_axis=0)[:, :W]
    qc = lax.broadcasted_iota(jnp.int32, (T, W), 0) // CHUNK
    kc = lax.broadcasted_iota(jnp.int32, (T, W), 1) // CHUNK
    o_ref[...] = jnp.where((kc >= qc) & (kc <= qc + ATT_LEFT_CHUNKS), x, NEG)


def _bias_table(rel_bias):
    T = ATT_TILE
    W = (ATT_REACH // T + 1) * T
    n = 1024
    assert W + T - 1 <= n
    lo = (T - 1) + ATT_REACH - REL_CLIP
    hi = n - lo - (2 * REL_CLIP + 1)
    assert hi >= 0
    base = jnp.concatenate([jnp.broadcast_to(rel_bias[:, :1], (N_HEADS, lo)), rel_bias,
                            jnp.broadcast_to(rel_bias[:, -1:], (N_HEADS, hi))], axis=1)
    return pl.pallas_call(
        _bias_table_kernel,
        out_shape=jax.ShapeDtypeStruct((N_HEADS, T, W), F32),
        grid=(N_HEADS,),
        in_specs=[pl.BlockSpec((None, 1, n), lambda h: (h, 0, 0))],
        out_specs=pl.BlockSpec((None, T, W), lambda h: (h, 0, 0)),
        compiler_params=_params(("arbitrary",)),
        name="bias_table",
    )(base.reshape(N_HEADS, 1, n).astype(F32))


def _att_prompt_kernel(q_ref, k0_ref, k1_ref, k2_ref, v0_ref, v1_ref, v2_ref, tab_ref, o_ref):
    i = pl.program_id(1)
    T = ATT_TILE

    def scores(h):
        cols = slice(h * ATT_DH, (h + 1) * ATT_DH)
        q = q_ref[:, cols]
        s0 = _qk(q, k0_ref[:, cols]) + tab_ref[h, :, 0:T]
        s1 = _qk(q, k1_ref[:, cols]) + tab_ref[h, :, T:2 * T]
        s2 = _qk(q, k2_ref[:, cols]) + tab_ref[h, :, 2 * T:3 * T]
        return [jnp.where(i >= 2, s0, NEG), jnp.where(i >= 1, s1, NEG), s2]

    def finish(h, ss):
        cols = slice(h * ATT_DH, (h + 1) * ATT_DH)
        o = _softmax_pv(ss, [v0_ref[:, cols], v1_ref[:, cols], v2_ref[:, cols]])
        o_ref[:, cols] = o.astype(BF16)

    _per_head_pipelined(scores, finish)


def _attention_prompt(proj, rel_bias):
    B, S, _ = proj.shape
    T = ATT_TILE
    nq = S // T
    assert ATT_REACH // T + 1 == 3
    table = _bias_table(rel_bias)

    def blk(off, back):
        return pl.BlockSpec((None, T, QK_W), lambda b, i: (b, jnp.maximum(i - back, 0), off // QK_W))

    return pl.pallas_call(
        _att_prompt_kernel,
        out_shape=jax.ShapeDtypeStruct((B, S, QK_W), BF16),
        grid=(B, nq),
        in_specs=[blk(OFF_AQ, 0),
                  blk(OFF_AK, 2), blk(OFF_AK, 1), blk(OFF_AK, 0),
                  blk(OFF_AV, 2), blk(OFF_AV, 1), blk(OFF_AV, 0),
                  _const_spec(table.shape)],
        out_specs=pl.BlockSpec((None, T, QK_W), lambda b, i: (b, i, 0)),
        compiler_params=_params(("arbitrary", "arbitrary")),
        name="attention_prompt",
    )(proj, proj, proj, proj, proj, proj, proj, table)


def _att_sample_kernel(q_ref, kn_ref, vn_ref, kc_ref, vc_ref, tab_ref, o_ref):
    P = kc_ref.shape[0]

    def scores(h):
        cols = slice(h * ATT_DH, (h + 1) * ATT_DH)
        q = q_ref[:, cols]
        sc = _qk(q, kc_ref[:, h, :].astype(BF16)) + tab_ref[h, :, 0:P]
        sn = _qk(q, kn_ref[:, cols]) + tab_ref[h, :, P:]
        return [sc, sn]

    def finish(h, ss):
        cols = slice(h * ATT_DH, (h + 1) * ATT_DH)
        o = _softmax_pv(ss, [vc_ref[:, h, :].astype(BF16), vn_ref[:, cols]])
        o_ref[:, cols] = o.astype(BF16)

    _per_head_pipelined(scores, finish)


def _attention_sample(proj, cache_k, cache_v, layer, rel_bias):
    B, L, _ = proj.shape
    P = cache_k.shape[2]
    assert P == ATT_REACH and PAST_LEN >= ATT_REACH
    t = jnp.arange(L)
    rel = jnp.concatenate([jnp.arange(P)[None, :] - P - t[:, None], t[None, :] - t[:, None]], axis=1)
    table = rel_bias[:, jnp.clip(rel, -REL_CLIP, REL_CLIP) + REL_CLIP].astype(F32)
    cache_spec = pl.BlockSpec((None, None, P, N_HEADS, ATT_DH), lambda b: (layer, b, 0, 0, 0))
    return pl.pallas_call(
        _att_sample_kernel,
        out_shape=jax.ShapeDtypeStruct((B, L, QK_W), BF16),
        grid=(B,),
        in_specs=[pl.BlockSpec((None, L, QK_W), lambda b: (b, 0, OFF_AQ // QK_W)),
                  pl.BlockSpec((None, L, QK_W), lambda b: (b, 0, OFF_AK // QK_W)),
                  pl.BlockSpec((None, L, QK_W), lambda b: (b, 0, OFF_AV // QK_W)),
                  cache_spec, cache_spec,
                  _const_spec(table.shape)],
        out_specs=pl.BlockSpec((None, L, QK_W), lambda b: (b, 0, 0)),
        compiler_params=_params(("arbitrary",)),
        name="attention_sample",
    )(proj, proj, proj, cache_k, cache_v, table)


def _merge_kernel(gret_ref, oatt_ref, gr0_ref, gr1_ref, ga0_ref, ga1_ref, x_ref, gt_ref, g_ref,
                  wr_ref, wa_ref, wo_ref, o_ref, m_ref):
    y_ret = jnp.dot(gret_ref[...], wr_ref[...], preferred_element_type=F32)
    y_att = jnp.dot(oatt_ref[...], wa_ref[...], preferred_element_type=F32)
    half = D_MODEL // 2
    m_ref[:, :half] = (gr0_ref[...].astype(F32) * y_ret[:, :half]
                       + ga0_ref[...].astype(F32) * y_att[:, :half]).astype(BF16)
    m_ref[:, half:] = (gr1_ref[...].astype(F32) * y_ret[:, half:]
                       + ga1_ref[...].astype(F32) * y_att[:, half:]).astype(BF16)
    z = jnp.dot(m_ref[...], wo_ref[...], preferred_element_type=F32)
    o_ref[...] = x_ref[...] + gt_ref[...] * (_rms(z) * g_ref[...])


def _merge(x, g_ret, o_att, proj, mod, g_post1, w_br_ret, w_br_att, w_out, tm):
    B, S, _ = x.shape
    R = mod.shape[1]
    tpb = S // tm
    assert S % tm == 0 and (R == 1 or tpb == 1)
    rm = R if R == 1 else tm
    half = D_MODEL // 2
    row = lambda i: (i // tpb, i % tpb)

    def gate_spec(off):
        return pl.BlockSpec((None, tm, half), lambda i: (*row(i), off // half))

    return pl.pallas_call(
        _merge_kernel,
        out_shape=jax.ShapeDtypeStruct((B, S, D_MODEL), F32),
        grid=(B * tpb,),
        in_specs=[pl.BlockSpec((None, tm, RV_W), lambda i: (*row(i), 0)),
                  pl.BlockSpec((None, tm, QK_W), lambda i: (*row(i), 0)),
                  gate_spec(OFF_GR), gate_spec(OFF_GR + half),
                  gate_spec(OFF_GA), gate_spec(OFF_GA + half),
                  pl.BlockSpec((None, tm, D_MODEL), lambda i: (*row(i), 0)),
                  pl.BlockSpec((None, rm, D_MODEL), lambda i: (i // tpb, 0, 2)),
                  pl.BlockSpec((1, D_MODEL), lambda i: (0, 0)),
                  _const_spec(w_br_ret.shape), _const_spec(w_br_att.shape), _const_spec(w_out.shape)],
        out_specs=pl.BlockSpec((None, tm, D_MODEL), lambda i: (*row(i), 0)),
        scratch_shapes=[pltpu.VMEM((tm, D_MODEL), BF16)],
        compiler_params=_params(("arbitrary",)),
        name="merge_out",
    )(g_ret, o_att, proj, proj, proj, proj, x, mod, g_post1.reshape(1, D_MODEL),
      w_br_ret, w_br_att, w_out)


HALO = 8


def _ffn_kernel(x_ref, sh_ref, sc_ref, gt_ref, gpre_ref, gpost_ref, wv_ref, wg_ref, cwv_ref, cwg_ref,
                cbv_ref, cbg_ref, wd_ref, cin_ref, o_ref, cov_ref, cog_ref,
                u_ref, upa_ref, upb_ref, h_ref, carry_ref, *, nb, tpb, nj):
    i, j = pl.program_id(0), pl.program_id(1)
    tn = FFN_TN
    L = x_ref.shape[0] // nb
    ups = (upa_ref, upb_ref)
    base = [s * (L + HALO) + HALO for s in range(nb)]
    rb = min(L, 64)

    def up_phase(t):
        for half, w_ref in enumerate((wv_ref, wg_ref)):
            acc = jnp.dot(u_ref[...], w_ref[...], preferred_element_type=F32)
            for s in range(nb):
                ups[t % 2][base[s]:base[s] + L, half * tn:(half + 1) * tn] = acc[s * L:(s + 1) * L]

    def gate_down_phase(t):
        src = ups[t % 2]
        for s in range(nb):
            src[base[s] - 2:base[s], :] = carry_ref[j - 1, s]

        def conv(r0, cols, cw_ref, cb_ref, wc):
            cur = src[r0:r0 + rb, cols]
            p1 = src[r0 - 1:r0 - 1 + rb, cols]
            p2 = src[r0 - 2:r0 - 2 + rb, cols]
            return (cb_ref[:, wc] + cw_ref[2:3, wc] * cur + cw_ref[1:2, wc] * p1 + cw_ref[0:1, wc] * p2)

        for p in range(tn // FFN_SUB):
            wc = slice(p * FFN_SUB, (p + 1) * FFN_SUB)
            cg = slice(tn + p * FFN_SUB, tn + (p + 1) * FFN_SUB)
            for s in range(nb):
                for r in range(0, L, rb):
                    value = conv(base[s] + r, wc, cwv_ref, cbv_ref, wc)
                    gate = conv(base[s] + r, cg, cwg_ref, cbg_ref, wc)
                    h_ref[s * L + r:s * L + r + rb, wc] = (
                        jax.nn.gelu(gate, approximate=True) * value).astype(BF16)

        for s in range(nb):
            last = src[base[s] + L - 2:base[s] + L, :]
            carry_ref[j - 1, s] = last
            cov_ref[s] = last[:, :tn]
            cog_ref[s] = last[:, tn:]

        w = 512
        for n in range(D_MODEL // w):
            o_ref[:, n * w:(n + 1) * w] += jnp.dot(h_ref[...], wd_ref[:, n * w:(n + 1) * w],
                                                   preferred_element_type=F32)

    @pl.when(j == 0)
    def _():
        u = (_rms(x_ref[...]) * gpre_ref[...]) * (1.0 + sc_ref[...]) + sh_ref[...]
        u_ref[...] = u.astype(BF16)
        o_ref[...] = jnp.zeros(o_ref.shape, F32)

        @pl.when(i % tpb == 0)
        def _():
            for jj in range(nj):
                carry_ref[jj, :, :, :tn] = cin_ref[:, :, jj * tn:(jj + 1) * tn]
                carry_ref[jj, :, :, tn:] = cin_ref[:, :, D_FF + jj * tn:D_FF + (jj + 1) * tn]

        up_phase(0)

    def steady(parity):
        up_phase(parity)
        gate_down_phase(parity + 1)

    pl.when((j >= 1) & (j < nj) & (j % 2 == 0))(lambda: steady(0))
    pl.when((j >= 1) & (j < nj) & (j % 2 == 1))(lambda: steady(1))

    @pl.when(j == nj)
    def _():
        gate_down_phase(nj - 1)
        o_ref[...] = x_ref[...] + gt_ref[...] * (_rms(o_ref[...]) * gpost_ref[...])


def _ffn(x, mod, conv_in, g_pre2, g_post2, w_up, conv_w, conv_b, w_down, tm, nb):
    B, S, _ = x.shape
    R = mod.shape[1]
    tpb = S // tm
    assert S % tm == 0 and (R == 1 or tpb == 1) and (nb == 1 or tpb == 1)
    rm = R if R == 1 else tm
    tn = FFN_TN
    nj = D_FF // tn
    row = lambda i: (i // tpb, i % tpb)
    t_up = lambda j: jnp.minimum(j, nj - 1)
    t_gate = lambda j: jnp.maximum(j - 1, 0)
    conv_b2 = conv_b.reshape(1, 2 * D_FF)
    conv_out = jax.ShapeDtypeStruct((B * tpb * nb, CONV_W - 1, D_FF), F32)
    conv_out_spec = pl.BlockSpec((nb, CONV_W - 1, tn), lambda i, j: (i, 0, t_gate(j)))
    up_rows = nb * (tm // nb + HALO)
    y, co_v, co_g = pl.pallas_call(
        functools.partial(_ffn_kernel, nb=nb, tpb=tpb, nj=nj),
        out_shape=(jax.ShapeDtypeStruct((B, S, D_MODEL), F32), conv_out, conv_out),
        grid=(B * tpb, nj + 1),
        in_specs=[pl.BlockSpec((None, tm, D_MODEL), lambda i, j: (*row(i), 0)),
                  pl.BlockSpec((None, rm, D_MODEL), lambda i, j: (i // tpb, 0, 3)),
                  pl.BlockSpec((None, rm, D_MODEL), lambda i, j: (i // tpb, 0, 4)),
                  pl.BlockSpec((None, rm, D_MODEL), lambda i, j: (i // tpb, 0, 5)),
                  pl.BlockSpec((1, D_MODEL), lambda i, j: (0, 0)),
                  pl.BlockSpec((1, D_MODEL), lambda i, j: (0, 0)),
                  pl.BlockSpec((D_MODEL, tn), lambda i, j: (0, t_up(j))),
                  pl.BlockSpec((D_MODEL, tn), lambda i, j: (0, nj + t_up(j))),
                  pl.BlockSpec((CONV_W, tn), lambda i, j: (0, t_gate(j))),
                  pl.BlockSpec((CONV_W, tn), lambda i, j: (0, nj + t_gate(j))),
                  pl.BlockSpec((1, tn), lambda i, j: (0, t_gate(j))),
                  pl.BlockSpec((1, tn), lambda i, j: (0, nj + t_gate(j))),
                  pl.BlockSpec((tn, D_MODEL), lambda i, j: (t_gate(j), 0)),
                  pl.BlockSpec((nb, CONV_W - 1, 2 * D_FF), lambda i, j: (i // tpb, 0, 0))],
        out_specs=(pl.BlockSpec((None, tm, D_MODEL), lambda i, j: (*row(i), 0)),
                   conv_out_spec, conv_out_spec),
        scratch_shapes=[pltpu.VMEM((tm, D_MODEL), BF16),
                        pltpu.VMEM((up_rows, 2 * tn), F32),
                        pltpu.VMEM((up_rows, 2 * tn), F32),
                        pltpu.VMEM((tm, tn), BF16),
                        pltpu.VMEM((nj, nb, CONV_W - 1, 2 * tn), F32)],
        compiler_params=_params(("arbitrary", "arbitrary")),
        name="conv_ffn",
    )(x, mod, mod, mod, g_pre2.reshape(1, D_MODEL), g_post2.reshape(1, D_MODEL), w_up, w_up,
      conv_w, conv_w, conv_b2, conv_b2, w_down, conv_in)
    return y, jnp.concatenate([co_v, co_g], axis=-1)


def _rotary_tables(pos):
    half = RET_DK // 2
    inv = ROPE_BASE ** (-jnp.arange(half, dtype=F32) / half)
    ang = pos.astype(F32)[:, None] * inv[None, :]
    cos, sin = jnp.cos(ang), jnp.sin(ang)
    return jnp.concatenate([cos, cos], axis=-1), jnp.concatenate([-sin, sin], axis=-1)


def kernel(x_prompt, x_sample, cache_att_k, cache_att_v, state_ret, state_conv, c_prompt, c_sample,
           w_ada, b_ada, g_pre1, w_in, rel_bias, w_br_ret, w_br_att, w_out, g_post1, g_pre2,
           w_up, conv_w, conv_b, w_down, g_post2):
    depth = w_ada.shape[0]
    Bp, Sp, _ = x_prompt.shape
    Bs, Ls, _ = x_sample.shape
    log_gamma = jnp.log(1.0 - 2.0 ** (-5.0 - jnp.arange(N_HEADS, dtype=F32)))
    cos_p, sin_p = _rotary_tables(jnp.arange(Sp))
    cos_s, sin_s = _rotary_tables(PAST_LEN + jnp.arange(Ls))
    cos_s, sin_s = jnp.tile(cos_s, (Bs, 1)), jnp.tile(sin_s, (Bs, 1))
    keep = min(ATT_REACH, Sp)
    tm_p, tm_wide = 512, 1024

    yp = x_prompt
    ys = x_sample.reshape(1, Bs * Ls, D_MODEL)
    outs = [[] for _ in range(8)]
    for l in range(depth):
        w_in_b, w_up_b, w_down_b = w_in[l].astype(BF16), w_up[l].astype(BF16), w_down[l].astype(BF16)
        w_br_ret_b, w_br_att_b, w_out_b = (w_br_ret[l].astype(BF16), w_br_att[l].astype(BF16),
                                           w_out[l].astype(BF16))
        mod = _mod(jnp.concatenate([c_prompt, c_sample], axis=0), w_ada[l], b_ada[l])
        mod_p = mod[:Bp].reshape(Bp, 1, 6 * D_MODEL)
        mod_s = jnp.repeat(mod[Bp:], Ls, axis=0).reshape(1, Bs * Ls, 6 * D_MODEL)

        proj = _in_proj(yp, mod_p, g_pre1[l], w_in_b, cos_p, sin_p, tm=tm_wide)
        zero_state = jnp.zeros((Bp, N_HEADS, RET_DK, RET_DV), F32)
        g_ret, r_p = _retention(proj, zero_state, log_gamma, RET_TILE, CHUNK)
        o_att = _attention_prompt(proj, rel_bias[l])
        x1 = _merge(yp, g_ret, o_att, proj, mod_p, g_post1[l], w_br_ret_b, w_br_att_b, w_out_b, tm=tm_p)
        conv0 = jnp.zeros((Bp, CONV_W - 1, 2 * D_FF), F32)
        yp, cv_p = _ffn(x1, mod_p, conv0, g_pre2[l], g_post2[l], w_up_b, conv_w[l], conv_b[l],
                        w_down_b, tm=tm_p, nb=1)
        cv_p = cv_p.reshape(Bp, Sp // tm_p, CONV_W - 1, 2 * D_FF)[:, -1]
        k_p = proj[:, Sp - keep:, OFF_AK:OFF_AV].astype(F32).reshape(Bp, keep, N_HEADS, ATT_DH)
        v_p = proj[:, Sp - keep:, OFF_AV:OFF_GR].astype(F32).reshape(Bp, keep, N_HEADS, ATT_DH)

        proj_s = _in_proj(ys, mod_s, g_pre1[l], w_in_b, cos_s, sin_s, tm=Bs * Ls)
        proj_s3 = proj_s.reshape(Bs, Ls, IN_WIDTH)
        g_ret_s, r_s = _retention(proj_s3, state_ret[l], log_gamma, Ls, Ls)
        o_att_s = _attention_sample(proj_s3, cache_att_k, cache_att_v, l, rel_bias[l])
        x1_s = _merge(ys, g_ret_s.reshape(1, Bs * Ls, -1), o_att_s.reshape(1, Bs * Ls, -1), proj_s,
                      mod_s, g_post1[l], w_br_ret_b, w_br_att_b, w_out_b, tm=Bs * Ls)
        ys, cv_s = _ffn(x1_s, mod_s, state_conv[l], g_pre2[l], g_post2[l], w_up_b, conv_w[l],
                        conv_b[l], w_down_b, tm=Bs * Ls, nb=Bs)
        k_s = proj_s3[:, :, OFF_AK:OFF_AV].astype(F32).reshape(Bs, Ls, N_HEADS, ATT_DH)
        v_s = proj_s3[:, :, OFF_AV:OFF_GR].astype(F32).reshape(Bs, Ls, N_HEADS, ATT_DH)

        for lst, val in zip(outs, (k_p, v_p, r_p, cv_p, k_s, v_s, r_s, cv_s)):
            lst.append(val)
    return (yp, ys.reshape(Bs, Ls, D_MODEL), *[jnp.stack(o) for o in outs])
```

```python
import functools
import math

import jax
import jax.numpy as jnp
from jax import lax
from jax.experimental import pallas as pl
from jax.experimental.pallas import tpu as pltpu

F32 = jnp.float32
BF16 = jnp.bfloat16

D_MODEL = 2048
PAST_LEN = 4096
CHUNK = 64
N_HEADS = 8
RET_DK = 128
RET_DV = 256
ATT_DH = 128
ATT_LEFT_CHUNKS = 8
ATT_REACH = ATT_LEFT_CHUNKS * CHUNK
REL_CLIP = 128
D_FF = 5632
CONV_W = 3
ROPE_BASE = 10000.0
EPS = 1e-6
IN_WIDTH = 13312
NEG = -1e30
LOG2E = math.log2(math.e)

OFF_RQ, OFF_RK, OFF_RV, OFF_RG = 0, 1024, 2048, 4096
OFF_AQ, OFF_AK, OFF_AV, OFF_GR, OFF_GA = 6144, 7168, 8192, 9216, 11264
QK_W = N_HEADS * RET_DK
RV_W = N_HEADS * RET_DV

LANE = 128
MXU_N = 256
VMEM_LIMIT = 56 * 1024 * 1024

PROJ_TN = 1024
RET_TILE = 256
ATT_TILE = 256
FFN_TN = 512
FFN_SUB = 128


def _params(sem):
    return pltpu.CompilerParams(dimension_semantics=sem, vmem_limit_bytes=VMEM_LIMIT)


def _rms(x):
    return x * lax.rsqrt(jnp.mean(x * x, axis=-1, keepdims=True) + EPS)


def _sigmoid(x):
    return 1.0 / (1.0 + jnp.exp2(x * (-LOG2E)))


def _gelu_tanh(x):
    k = -2.0 * LOG2E * math.sqrt(2.0 / math.pi)
    return x / (1.0 + jnp.exp2(x * (k + (k * 0.044715) * (x * x))))


def _const_spec(shape):
    zeros = (0,) * len(shape)
    return pl.BlockSpec(shape, lambda *_: zeros, pipeline_mode=pl.Buffered(1))


def _mod_kernel(c_ref, w_ref, b_ref, o_ref):
    a = jax.nn.silu(c_ref[...]).astype(BF16)
    o_ref[...] = jnp.dot(a, w_ref[...].astype(BF16), preferred_element_type=F32) + b_ref[...]


def _mod(c, w_ada, b_ada):
    n, tn = c.shape[0], 1024
    width = w_ada.shape[1]
    return pl.pallas_call(
        _mod_kernel,
        out_shape=jax.ShapeDtypeStruct((n, width), F32),
        grid=(width // tn,),
        in_specs=[pl.BlockSpec((n, D_MODEL), lambda j: (0, 0)),
                  pl.BlockSpec((D_MODEL, tn), lambda j: (0, j)),
                  pl.BlockSpec((1, tn), lambda j: (0, j))],
        out_specs=pl.BlockSpec((n, tn), lambda j: (0, j)),
        compiler_params=_params(("arbitrary",)),
        name="adaln_mod",
    )(c, w_ada, b_ada.reshape(1, width))


def _inproj_kernel(x_ref, sh_ref, sc_ref, g_ref, w_ref, cos_ref, sin_ref, o_ref, u_ref):
    j = pl.program_id(1)

    @pl.when(j == 0)
    def _():
        u = (_rms(x_ref[...]) * g_ref[...]) * (1.0 + sc_ref[...]) + sh_ref[...]
        u_ref[...] = u.astype(BF16)

    def tile(epilogue):
        w = 2 * MXU_N
        for c in range(PROJ_TN // w):
            cols = slice(c * w, (c + 1) * w)
            acc = jnp.dot(u_ref[...], w_ref[:, cols], preferred_element_type=F32)
            o_ref[:, cols] = epilogue(acc).astype(BF16)

    def rotary(acc):
        scale = jnp.where(j == OFF_RK // PROJ_TN, RET_DK ** -0.5, 1.0).astype(F32)
        cos, sin = cos_ref[...], sin_ref[...]
        heads = [acc[:, h * LANE:(h + 1) * LANE] for h in range(acc.shape[1] // LANE)]
        return jnp.concatenate(
            [(a * cos + pltpu.roll(a, LANE // 2, 1) * sin) * scale for a in heads], axis=-1)

    pl.when(j < OFF_RV // PROJ_TN)(lambda: tile(rotary))
    pl.when((j >= OFF_RG // PROJ_TN) & (j < OFF_AQ // PROJ_TN))(lambda: tile(lambda a: a * _sigmoid(a)))
    pl.when(j >= OFF_GR // PROJ_TN)(lambda: tile(_sigmoid))
    pl.when(((j >= OFF_RV // PROJ_TN) & (j < OFF_RG // PROJ_TN))
            | ((j >= OFF_AQ // PROJ_TN) & (j < OFF_GR // PROJ_TN)))(lambda: tile(lambda a: a))


def _in_proj(x, mod, g_pre1, w_in, cos, sin, tm):
    B, S, _ = x.shape
    R = mod.shape[1]
    tpb = S // tm
    assert S % tm == 0 and (R == 1 or tpb == 1)
    rm = R if R == 1 else tm
    return pl.pallas_call(
        _inproj_kernel,
        out_shape=jax.ShapeDtypeStruct((B, S, IN_WIDTH), BF16),
        grid=(B * tpb, IN_WIDTH // PROJ_TN),
        in_specs=[
            pl.BlockSpec((None, tm, D_MODEL), lambda i, j: (i // tpb, i % tpb, 0)),
            pl.BlockSpec((None, rm, D_MODEL), lambda i, j: (i // tpb, 0, 0)),
            pl.BlockSpec((None, rm, D_MODEL), lambda i, j: (i // tpb, 0, 1)),
            pl.BlockSpec((1, D_MODEL), lambda i, j: (0, 0)),
            pl.BlockSpec((D_MODEL, PROJ_TN), lambda i, j: (0, j)),
            pl.BlockSpec((tm, LANE), lambda i, j: (i % tpb, 0)),
            pl.BlockSpec((tm, LANE), lambda i, j: (i % tpb, 0)),
        ],
        out_specs=pl.BlockSpec((None, tm, PROJ_TN), lambda i, j: (i // tpb, i % tpb, j)),
        scratch_shapes=[pltpu.VMEM((tm, D_MODEL), BF16)],
        compiler_params=_params(("arbitrary", "arbitrary")),
        name="in_proj",
    )(x, mod, mod, g_pre1.reshape(1, D_MODEL), w_in, cos, sin)


def _retention_kernel(ds_ref, q_ref, k_ref, v_ref, rg_ref, dmat_ref, dq_ref, dk_ref, s0_ref,
                      o_ref, snew_ref, s_ref):
    t = pl.program_id(1)

    @pl.when(t == 0)
    def _():
        s_ref[...] = s0_ref[...]

    def dots(h):
        q = q_ref[:, h * RET_DK:(h + 1) * RET_DK]
        k = k_ref[:, h * RET_DK:(h + 1) * RET_DK]
        v = v_ref[:, h * RET_DV:(h + 1) * RET_DV]
        state = s_ref[h]
        s = lax.dot_general(q, k, (((1,), (1,)), ((), ())), preferred_element_type=F32)
        qs = jnp.dot(q, state.astype(BF16), preferred_element_type=F32)
        kd = (k.astype(F32) * dk_ref[h]).astype(BF16)
        kv = lax.dot_general(kd, v, (((0,), (0,)), ((), ())), preferred_element_type=F32)
        return s, qs, kv, state, v

    def finish(h, s, qs, kv, state, v):
        o = (jnp.dot((s * dmat_ref[h]).astype(BF16), v, preferred_element_type=F32)
             + qs * dq_ref[h])
        new_state = ds_ref[h] * state + kv
        s_ref[h] = new_state
        cols = slice(h * RET_DV, (h + 1) * RET_DV)
        o_ref[:, cols] = (rg_ref[:, cols].astype(F32) * _rms(o)).astype(BF16)

    pending = dots(0)
    for h in range(1, N_HEADS):
        nxt = dots(h)
        finish(h - 1, *pending)
        pending = nxt
    finish(N_HEADS - 1, *pending)

    @pl.when(t == pl.num_programs(1) - 1)
    def _():
        snew_ref[...] = s_ref[...]


def _retention(proj, state0, log_gamma, tile, chunk):
    B, S, _ = proj.shape
    nt = S // tile
    pos = jnp.arange(tile)
    ch = pos // chunk
    dist = (pos[:, None] - pos[None, :]).astype(F32)
    lg = log_gamma[:, None, None]
    same = ch[:, None] == ch[None, :]
    past = ch[None, :] < ch[:, None]
    dmat = jnp.where(same[None], jnp.exp(lg * jnp.abs(dist)[None]),
                     jnp.where(past[None], jnp.exp(lg * dist[None]), 0.0))
    tf = pos.astype(F32)
    dq = jnp.broadcast_to(jnp.exp((tf[None, :] + 1.0) * log_gamma[:, None])[:, :, None],
                          (N_HEADS, tile, RET_DV))
    dk = jnp.broadcast_to(jnp.exp((tile - 1.0 - tf)[None, :] * log_gamma[:, None])[:, :, None],
                          (N_HEADS, tile, RET_DK))
    ds = jnp.exp(tile * log_gamma)
    state_spec = pl.BlockSpec((None, N_HEADS, RET_DK, RET_DV), lambda b, t: (b, 0, 0, 0))
    return pl.pallas_call(
        _retention_kernel,
        out_shape=(jax.ShapeDtypeStruct((B, S, RV_W), BF16),
                   jax.ShapeDtypeStruct((B, N_HEADS, RET_DK, RET_DV), F32)),
        grid=(B, nt),
        in_specs=[
            pl.BlockSpec(memory_space=pltpu.SMEM),
            pl.BlockSpec((None, tile, QK_W), lambda b, t: (b, t, OFF_RQ // QK_W)),
            pl.BlockSpec((None, tile, QK_W), lambda b, t: (b, t, OFF_RK // QK_W)),
            pl.BlockSpec((None, tile, RV_W), lambda b, t: (b, t, OFF_RV // RV_W)),
            pl.BlockSpec((None, tile, RV_W), lambda b, t: (b, t, OFF_RG // RV_W)),
            _const_spec(dmat.shape), _const_spec(dq.shape), _const_spec(dk.shape),
            state_spec,
        ],
        out_specs=(pl.BlockSpec((None, tile, RV_W), lambda b, t: (b, t, 0)), state_spec),
        scratch_shapes=[pltpu.VMEM((N_HEADS, RET_DK, RET_DV), F32)],
        compiler_params=_params(("arbitrary", "arbitrary")),
        name="retention",
    )(ds, proj, proj, proj, proj, dmat, dq, dk, state0)


def _qk(q, k):
    s = lax.dot_general(q, k, (((1,), (1,)), ((), ())), preferred_element_type=F32)
    return s * ((ATT_DH ** -0.5) * LOG2E)


def _softmax_pv(scores, values):
    m = functools.reduce(jnp.maximum, [jnp.max(s, axis=-1, keepdims=True) for s in scores])
    ps = [jnp.exp2(s - m) for s in scores]
    l = functools.reduce(jnp.add, [jnp.sum(p, axis=-1, keepdims=True) for p in ps])
    o = functools.reduce(jnp.add, [jnp.dot(p.astype(BF16), v, preferred_element_type=F32)
                                   for p, v in zip(ps, values)])
    return o / l


def _per_head_pipelined(scores_fn, finish_fn):
    pending = scores_fn(0)
    for h in range(1, N_HEADS):
        nxt = scores_fn(h)
        finish_fn(h - 1, pending)
        pending = nxt
    finish_fn(N_HEADS - 1, pending)


def _bias_table_kernel(base_ref, o_ref):
    T, W = o_ref.shape
    n = base_ref.shape[-1]
    x = jnp.broadcast_to(base_ref[...], (T, n))
    x = pltpu.roll(x, n - T + 1, 1, stride=1, stride_axis=0)[:, :W]
    qc = lax.broadcasted_iota(jnp.int32, (T, W), 0) // CHUNK
    kc = lax.broadcasted_iota(jnp.int32, (T, W), 1) // CHUNK
    o_ref[...] = jnp.where((kc >= qc) & (kc <= qc + ATT_LEFT_CHUNKS), x * LOG2E, NEG)


def _bias_table(rel_bias):
    T = ATT_TILE
    W = (ATT_REACH // T + 1) * T
    n = 1024
    assert W + T - 1 <= n
    lo = (T - 1) + ATT_REACH - REL_CLIP
    hi = n - lo - (2 * REL_CLIP + 1)
    assert hi >= 0
    base = jnp.concatenate([jnp.broadcast_to(rel_bias[:, :1], (N_HEADS, lo)), rel_bias,
                            jnp.broadcast_to(rel_bias[:, -1:], (N_HEADS, hi))], axis=1)
    return pl.pallas_call(
        _bias_table_kernel,
        out_shape=jax.ShapeDtypeStruct((N_HEADS, T, W), F32),
        grid=(N_HEADS,),
        in_specs=[pl.BlockSpec((None, 1, n), lambda h: (h, 0, 0))],
        out_specs=pl.BlockSpec((None, T, W), lambda h: (h, 0, 0)),
        compiler_params=_params(("arbitrary",)),
        name="bias_table",
    )(base.reshape(N_HEADS, 1, n).astype(F32))


def _att_prompt_kernel(q_ref, k0_ref, k1_ref, k2_ref, v0_ref, v1_ref, v2_ref, tab_ref, o_ref):
    i = pl.program_id(1)
    T = ATT_TILE

    def scores(h):
        cols = slice(h * ATT_DH, (h + 1) * ATT_DH)
        q = q_ref[:, cols]
        s0 = _qk(q, k0_ref[:, cols]) + tab_ref[h, :, 0:T]
        s1 = _qk(q, k1_ref[:, cols]) + tab_ref[h, :, T:2 * T]
        s2 = _qk(q, k2_ref[:, cols]) + tab_ref[h, :, 2 * T:3 * T]
        return [jnp.where(i >= 2, s0, NEG), jnp.where(i >= 1, s1, NEG), s2]

    def finish(h, ss):
        cols = slice(h * ATT_DH, (h + 1) * ATT_DH)
        o = _softmax_pv(ss, [v0_ref[:, cols], v1_ref[:, cols], v2_ref[:, cols]])
        o_ref[:, cols] = o.astype(BF16)

    _per_head_pipelined(scores, finish)


def _attention_prompt(proj, rel_bias):
    B, S, _ = proj.shape
    T = ATT_TILE
    nq = S // T
    assert ATT_REACH // T + 1 == 3
    table = _bias_table(rel_bias)

    def blk(off, back):
        return pl.BlockSpec((None, T, QK_W), lambda b, i: (b, jnp.maximum(i - back, 0), off // QK_W))

    return pl.pallas_call(
        _att_prompt_kernel,
        out_shape=jax.ShapeDtypeStruct((B, S, QK_W), BF16),
        grid=(B, nq),
        in_specs=[blk(OFF_AQ, 0),
                  blk(OFF_AK, 2), blk(OFF_AK, 1), blk(OFF_AK, 0),
                  blk(OFF_AV, 2), blk(OFF_AV, 1), blk(OFF_AV, 0),
                  _const_spec(table.shape)],
        out_specs=pl.BlockSpec((None, T, QK_W), lambda b, i: (b, i, 0)),
        compiler_params=_params(("arbitrary", "arbitrary")),
        name="attention_prompt",
    )(proj, proj, proj, proj, proj, proj, proj, table)


def _att_sample_kernel(q_ref, kn_ref, vn_ref, kc_ref, vc_ref, tab_ref, o_ref):
    P = kc_ref.shape[0]

    def scores(h):
        cols = slice(h * ATT_DH, (h + 1) * ATT_DH)
        q = q_ref[:, cols]
        sc = _qk(q, kc_ref[:, h, :].astype(BF16)) + tab_ref[h, :, 0:P]
        sn = _qk(q, kn_ref[:, cols]) + tab_ref[h, :, P:]
        return [sc, sn]

    def finish(h, ss):
        cols = slice(h * ATT_DH, (h + 1) * ATT_DH)
        o = _softmax_pv(ss, [vc_ref[:, h, :].astype(BF16), vn_ref[:, cols]])
        o_ref[:, cols] = o.astype(BF16)

    _per_head_pipelined(scores, finish)


def _attention_sample(proj, cache_k, cache_v, layer, rel_bias):
    B, L, _ = proj.shape
    P = cache_k.shape[2]
    assert P == ATT_REACH and PAST_LEN >= ATT_REACH
    t = jnp.arange(L)
    rel = jnp.concatenate([jnp.arange(P)[None, :] - P - t[:, None], t[None, :] - t[:, None]], axis=1)
    table = rel_bias[:, jnp.clip(rel, -REL_CLIP, REL_CLIP) + REL_CLIP].astype(F32) * LOG2E
    cache_spec = pl.BlockSpec((None, None, P, N_HEADS, ATT_DH), lambda b: (layer, b, 0, 0, 0))
    return pl.pallas_call(
        _att_sample_kernel,
        out_shape=jax.ShapeDtypeStruct((B, L, QK_W), BF16),
        grid=(B,),
        in_specs=[pl.BlockSpec((None, L, QK_W), lambda b: (b, 0, OFF_AQ // QK_W)),
                  pl.BlockSpec((None, L, QK_W), lambda b: (b, 0, OFF_AK // QK_W)),
                  pl.BlockSpec((None, L, QK_W), lambda b: (b, 0, OFF_AV // QK_W)),
                  cache_spec, cache_spec,
                  _const_spec(table.shape)],
        out_specs=pl.BlockSpec((None, L, QK_W), lambda b: (b, 0, 0)),
        compiler_params=_params(("arbitrary",)),
        name="attention_sample",
    )(proj, proj, proj, cache_k, cache_v, table)


def _merge_kernel(gret_ref, oatt_ref, gr0_ref, gr1_ref, ga0_ref, ga1_ref, x_ref, gt_ref, g_ref,
                  wr_ref, wa_ref, wo_ref, o_ref, m_ref):
    y_ret = jnp.dot(gret_ref[...], wr_ref[...], preferred_element_type=F32)
    y_att = jnp.dot(oatt_ref[...], wa_ref[...], preferred_element_type=F32)
    half = D_MODEL // 2
    m_ref[:, :half] = (gr0_ref[...].astype(F32) * y_ret[:, :half]
                       + ga0_ref[...].astype(F32) * y_att[:, :half]).astype(BF16)
    m_ref[:, half:] = (gr1_ref[...].astype(F32) * y_ret[:, half:]
                       + ga1_ref[...].astype(F32) * y_att[:, half:]).astype(BF16)
    z = jnp.dot(m_ref[...], wo_ref[...], preferred_element_type=F32)
    o_ref[...] = x_ref[...] + gt_ref[...] * (_rms(z) * g_ref[...])


def _merge(x, g_ret, o_att, proj, mod, g_post1, w_br_ret, w_br_att, w_out, tm):
    B, S, _ = x.shape
    R = mod.shape[1]
    tpb = S // tm
    assert S % tm == 0 and (R == 1 or tpb == 1)
    rm = R if R == 1 else tm
    half = D_MODEL // 2
    row = lambda i: (i // tpb, i % tpb)

    def gate_spec(off):
        return pl.BlockSpec((None, tm, half), lambda i: (*row(i), off // half))

    return pl.pallas_call(
        _merge_kernel,
        out_shape=jax.ShapeDtypeStruct((B, S, D_MODEL), F32),
        grid=(B * tpb,),
        in_specs=[pl.BlockSpec((None, tm, RV_W), lambda i: (*row(i), 0)),
                  pl.BlockSpec((None, tm, QK_W), lambda i: (*row(i), 0)),
                  gate_spec(OFF_GR), gate_spec(OFF_GR + half),
                  gate_spec(OFF_GA), gate_spec(OFF_GA + half),
                  pl.BlockSpec((None, tm, D_MODEL), lambda i: (*row(i), 0)),
                  pl.BlockSpec((None, rm, D_MODEL), lambda i: (i // tpb, 0, 2)),
                  pl.BlockSpec((1, D_MODEL), lambda i: (0, 0)),
                  _const_spec(w_br_ret.shape), _const_spec(w_br_att.shape), _const_spec(w_out.shape)],
        out_specs=pl.BlockSpec((None, tm, D_MODEL), lambda i: (*row(i), 0)),
        scratch_shapes=[pltpu.VMEM((tm, D_MODEL), BF16)],
        compiler_params=_params(("arbitrary",)),
        name="merge_out",
    )(g_ret, o_att, proj, proj, proj, proj, x, mod, g_post1.reshape(1, D_MODEL),
      w_br_ret, w_br_att, w_out)


HALO = 8


def _ffn_kernel(x_ref, sh_ref, sc_ref, gt_ref, gpre_ref, gpost_ref, wv_ref, wg_ref, cwv_ref, cwg_ref,
                cbv_ref, cbg_ref, wd_ref, cin_ref, o_ref, cov_ref, cog_ref,
                u_ref, upa_ref, upb_ref, h_ref, carry_ref, *, nb, tpb, nj):
    i, j = pl.program_id(0), pl.program_id(1)
    tn = FFN_TN
    L = x_ref.shape[0] // nb
    ups = (upa_ref, upb_ref)
    base = [s * (L + HALO) + HALO for s in range(nb)]
    rb = min(L, 64)

    def up_phase(t):
        for half, w_ref in enumerate((wv_ref, wg_ref)):
            acc = jnp.dot(u_ref[...], w_ref[...], preferred_element_type=F32)
            for s in range(nb):
                ups[t % 2][base[s]:base[s] + L, half * tn:(half + 1) * tn] = acc[s * L:(s + 1) * L]

    def gate_down_phase(t):
        src = ups[t % 2]
        for s in range(nb):
            src[base[s] - 2:base[s], :] = carry_ref[j - 1, s]

        def conv(r0, cols, cw_ref, cb_ref, wc):
            cur = src[r0:r0 + rb, cols]
            p1 = src[r0 - 1:r0 - 1 + rb, cols]
            p2 = src[r0 - 2:r0 - 2 + rb, cols]
            return (cb_ref[:, wc] + cw_ref[2:3, wc] * cur + cw_ref[1:2, wc] * p1 + cw_ref[0:1, wc] * p2)

        for p in range(tn // FFN_SUB):
            wc = slice(p * FFN_SUB, (p + 1) * FFN_SUB)
            cg = slice(tn + p * FFN_SUB, tn + (p + 1) * FFN_SUB)
            for s in range(nb):
                for r in range(0, L, rb):
                    value = conv(base[s] + r, wc, cwv_ref, cbv_ref, wc)
                    gate = conv(base[s] + r, cg, cwg_ref, cbg_ref, wc)
                    h_ref[s * L + r:s * L + r + rb, wc] = (_gelu_tanh(gate) * value).astype(BF16)

        for s in range(nb):
            last = src[base[s] + L - 2:base[s] + L, :]
            carry_ref[j - 1, s] = last
            cov_ref[s] = last[:, :tn]
            cog_ref[s] = last[:, tn:]

        w = 512
        for n in range(D_MODEL // w):
            o_ref[:, n * w:(n + 1) * w] += jnp.dot(h_ref[...], wd_ref[:, n * w:(n + 1) * w],
                                                   preferred_element_type=F32)

    @pl.when(j == 0)
    def _():
        u = (_rms(x_ref[...]) * gpre_ref[...]) * (1.0 + sc_ref[...]) + sh_ref[...]
        u_ref[...] = u.astype(BF16)
        o_ref[...] = jnp.zeros(o_ref.shape, F32)

        @pl.when(i % tpb == 0)
        def _():
            for jj in range(nj):
                carry_ref[jj, :, :, :tn] = cin_ref[:, :, jj * tn:(jj + 1) * tn]
                carry_ref[jj, :, :, tn:] = cin_ref[:, :, D_FF + jj * tn:D_FF + (jj + 1) * tn]

        up_phase(0)

    def steady(parity):
        up_phase(parity)
        gate_down_phase(parity + 1)

    pl.when((j >= 1) & (j < nj) & (j % 2 == 0))(lambda: steady(0))
    pl.when((j >= 1) & (j < nj) & (j % 2 == 1))(lambda: steady(1))

    @pl.when(j == nj)
    def _():
        gate_down_phase(nj - 1)
        o_ref[...] = x_ref[...] + gt_ref[...] * (_rms(o_ref[...]) * gpost_ref[...])


def _ffn(x, mod, conv_in, g_pre2, g_post2, w_up, conv_w, conv_b, w_down, tm, nb):
    B, S, _ = x.shape
    R = mod.shape[1]
    tpb = S // tm
    assert S % tm == 0 and (R == 1 or tpb == 1) and (nb == 1 or tpb == 1)
    rm = R if R == 1 else tm
    tn = FFN_TN
    nj = D_FF // tn
    row = lambda i: (i // tpb, i % tpb)
    t_up = lambda j: jnp.minimum(j, nj - 1)
    t_gate = lambda j: jnp.maximum(j - 1, 0)
    conv_b2 = conv_b.reshape(1, 2 * D_FF)
    conv_out = jax.ShapeDtypeStruct((B * tpb * nb, CONV_W - 1, D_FF), F32)
    conv_out_spec = pl.BlockSpec((nb, CONV_W - 1, tn), lambda i, j: (i, 0, t_gate(j)))
    up_rows = nb * (tm // nb + HALO)
    y, co_v, co_g = pl.pallas_call(
        functools.partial(_ffn_kernel, nb=nb, tpb=tpb, nj=nj),
        out_shape=(jax.ShapeDtypeStruct((B, S, D_MODEL), F32), conv_out, conv_out),
        grid=(B * tpb, nj + 1),
        in_specs=[pl.BlockSpec((None, tm, D_MODEL), lambda i, j: (*row(i), 0)),
                  pl.BlockSpec((None, rm, D_MODEL), lambda i, j: (i // tpb, 0, 3)),
                  pl.BlockSpec((None, rm, D_MODEL), lambda i, j: (i // tpb, 0, 4)),
                  pl.BlockSpec((None, rm, D_MODEL), lambda i, j: (i // tpb, 0, 5)),
                  pl.BlockSpec((1, D_MODEL), lambda i, j: (0, 0)),
                  pl.BlockSpec((1, D_MODEL), lambda i, j: (0, 0)),
                  pl.BlockSpec((D_MODEL, tn), lambda i, j: (0, t_up(j))),
                  pl.BlockSpec((D_MODEL, tn), lambda i, j: (0, nj + t_up(j))),
                  pl.BlockSpec((CONV_W, tn), lambda i, j: (0, t_gate(j))),
                  pl.BlockSpec((CONV_W, tn), lambda i, j: (0, nj + t_gate(j))),
                  pl.BlockSpec((1, tn), lambda i, j: (0, t_gate(j))),
                  pl.BlockSpec((1, tn), lambda i, j: (0, nj + t_gate(j))),
                  pl.BlockSpec((tn, D_MODEL), lambda i, j: (t_gate(j), 0)),
                  pl.BlockSpec((nb, CONV_W - 1, 2 * D_FF), lambda i, j: (i // tpb, 0, 0))],
        out_specs=(pl.BlockSpec((None, tm, D_MODEL), lambda i, j: (*row(i), 0)),
                   conv_out_spec, conv_out_spec),
        scratch_shapes=[pltpu.VMEM((tm, D_MODEL), BF16),
                        pltpu.VMEM((up_rows, 2 * tn), F32),
                        pltpu.VMEM((up_rows, 2 * tn), F32),
                        pltpu.VMEM((tm, tn), BF16),
                        pltpu.VMEM((nj, nb, CONV_W - 1, 2 * tn), F32)],
        compiler_params=_params(("arbitrary", "arbitrary")),
        name="conv_ffn",
    )(x, mod, mod, mod, g_pre2.reshape(1, D_MODEL), g_post2.reshape(1, D_MODEL), w_up, w_up,
      conv_w, conv_w, conv_b2, conv_b2, w_down, conv_in)
    return y, jnp.concatenate([co_v, co_g], axis=-1)


def _rotary_tables(pos):
    half = RET_DK // 2
    inv = ROPE_BASE ** (-jnp.arange(half, dtype=F32) / half)
    ang = pos.astype(F32)[:, None] * inv[None, :]
    cos, sin = jnp.cos(ang), jnp.sin(ang)
    return jnp.concatenate([cos, cos], axis=-1), jnp.concatenate([-sin, sin], axis=-1)


def kernel(x_prompt, x_sample, cache_att_k, cache_att_v, state_ret, state_conv, c_prompt, c_sample,
           w_ada, b_ada, g_pre1, w_in, rel_bias, w_br_ret, w_br_att, w_out, g_post1, g_pre2,
           w_up, conv_w, conv_b, w_down, g_post2):
    depth = w_ada.shape[0]
    Bp, Sp, _ = x_prompt.shape
    Bs, Ls, _ = x_sample.shape
    log_gamma = jnp.log(1.0 - 2.0 ** (-5.0 - jnp.arange(N_HEADS, dtype=F32)))
    cos_p, sin_p = _rotary_tables(jnp.arange(Sp))
    cos_s, sin_s = _rotary_tables(PAST_LEN + jnp.arange(Ls))
    cos_s, sin_s = jnp.tile(cos_s, (Bs, 1)), jnp.tile(sin_s, (Bs, 1))
    keep = min(ATT_REACH, Sp)
    tm_p, tm_wide = 512, 1024

    yp = x_prompt
    ys = x_sample.reshape(1, Bs * Ls, D_MODEL)
    outs = [[] for _ in range(8)]
    for l in range(depth):
        w_in_b, w_up_b, w_down_b = w_in[l].astype(BF16), w_up[l].astype(BF16), w_down[l].astype(BF16)
        w_br_ret_b, w_br_att_b, w_out_b = (w_br_ret[l].astype(BF16), w_br_att[l].astype(BF16),
                                           w_out[l].astype(BF16))
        mod = _mod(jnp.concatenate([c_prompt, c_sample], axis=0), w_ada[l], b_ada[l])
        mod_p = mod[:Bp].reshape(Bp, 1, 6 * D_MODEL)
        mod_s = jnp.repeat(mod[Bp:], Ls, axis=0).reshape(1, Bs * Ls, 6 * D_MODEL)

        proj = _in_proj(yp, mod_p, g_pre1[l], w_in_b, cos_p, sin_p, tm=tm_wide)
        zero_state = jnp.zeros((Bp, N_HEADS, RET_DK, RET_DV), F32)
        g_ret, r_p = _retention(proj, zero_state, log_gamma, RET_TILE, CHUNK)
        o_att = _attention_prompt(proj, rel_bias[l])
        x1 = _merge(yp, g_ret, o_att, proj, mod_p, g_post1[l], w_br_ret_b, w_br_att_b, w_out_b, tm=tm_p)
        conv0 = jnp.zeros((Bp, CONV_W - 1, 2 * D_FF), F32)
        yp, cv_p = _ffn(x1, mod_p, conv0, g_pre2[l], g_post2[l], w_up_b, conv_w[l], conv_b[l],
                        w_down_b, tm=tm_p, nb=1)
        cv_p = cv_p.reshape(Bp, Sp // tm_p, CONV_W - 1, 2 * D_FF)[:, -1]
        k_p = proj[:, Sp - keep:, OFF_AK:OFF_AV].astype(F32).reshape(Bp, keep, N_HEADS, ATT_DH)
        v_p = proj[:, Sp - keep:, OFF_AV:OFF_GR].astype(F32).reshape(Bp, keep, N_HEADS, ATT_DH)

        proj_s = _in_proj(ys, mod_s, g_pre1[l], w_in_b, cos_s, sin_s, tm=Bs * Ls)
        proj_s3 = proj_s.reshape(Bs, Ls, IN_WIDTH)
        g_ret_s, r_s = _retention(proj_s3, state_ret[l], log_gamma, Ls, Ls)
        o_att_s = _attention_sample(proj_s3, cache_att_k, cache_att_v, l, rel_bias[l])
        x1_s = _merge(ys, g_ret_s.reshape(1, Bs * Ls, -1), o_att_s.reshape(1, Bs * Ls, -1), proj_s,
                      mod_s, g_post1[l], w_br_ret_b, w_br_att_b, w_out_b, tm=Bs * Ls)
        ys, cv_s = _ffn(x1_s, mod_s, state_conv[l], g_pre2[l], g_post2[l], w_up_b, conv_w[l],
                        conv_b[l], w_down_b, tm=Bs * Ls, nb=Bs)
        k_s = proj_s3[:, :, OFF_AK:OFF_AV].astype(F32).reshape(Bs, Ls, N_HEADS, ATT_DH)
        v_s = proj_s3[:, :, OFF_AV:OFF_GR].astype(F32).reshape(Bs, Ls, N_HEADS, ATT_DH)

        for lst, val in zip(outs, (k_p, v_p, r_p, cv_p, k_s, v_s, r_s, cv_s)):
            lst.append(val)
    return (yp, ys.reshape(Bs, Ls, D_MODEL), *[jnp.stack(o) for o in outs])
```

```python
import functools
import math

import jax
import jax.numpy as jnp
from jax import lax
from jax.experimental import pallas as pl
from jax.experimental.pallas import tpu as pltpu

F32 = jnp.float32
BF16 = jnp.bfloat16

D_MODEL = 2048
PAST_LEN = 4096
CHUNK = 64
N_HEADS = 8
RET_DK = 128
RET_DV = 256
ATT_DH = 128
ATT_LEFT_CHUNKS = 8
ATT_REACH = ATT_LEFT_CHUNKS * CHUNK
REL_CLIP = 128
D_FF = 5632
CONV_W = 3
ROPE_BASE = 10000.0
EPS = 1e-6
IN_WIDTH = 13312
NEG = -1e30
LOG2E = math.log2(math.e)

OFF_RQ, OFF_RK, OFF_RV, OFF_RG = 0, 1024, 2048, 4096
OFF_AQ, OFF_AK, OFF_AV, OFF_GR, OFF_GA = 6144, 7168, 8192, 9216, 11264
QK_W = N_HEADS * RET_DK
RV_W = N_HEADS * RET_DV

LANE = 128
MXU_N = 256
VMEM_LIMIT = 62 * 1024 * 1024

PROJ_TN = 1024
RET_TILE = 256
ATT_TILE = 256
FFN_TN = 512
FFN_SUB = 128


def _params(sem):
    return pltpu.CompilerParams(dimension_semantics=sem, vmem_limit_bytes=VMEM_LIMIT)


def _rms(x):
    return x * lax.rsqrt(jnp.mean(x * x, axis=-1, keepdims=True) + EPS)


def _sigmoid(x):
    return 1.0 / (1.0 + jnp.exp2(x * (-LOG2E)))


def _gelu_tanh(x):
    k = -2.0 * LOG2E * math.sqrt(2.0 / math.pi)
    return x / (1.0 + jnp.exp2(x * (k + (k * 0.044715) * (x * x))))


def _const_spec(shape):
    zeros = (0,) * len(shape)
    return pl.BlockSpec(shape, lambda *_: zeros, pipeline_mode=pl.Buffered(1))


def _mod_kernel(c_ref, w_ref, b_ref, o_ref):
    a = jax.nn.silu(c_ref[...]).astype(BF16)
    o_ref[...] = jnp.dot(a, w_ref[...].astype(BF16), preferred_element_type=F32) + b_ref[...]


def _mod(c, w_ada, b_ada):
    n, tn = c.shape[0], 1024
    width = w_ada.shape[1]
    return pl.pallas_call(
        _mod_kernel,
        out_shape=jax.ShapeDtypeStruct((n, width), F32),
        grid=(width // tn,),
        in_specs=[pl.BlockSpec((n, D_MODEL), lambda j: (0, 0)),
                  pl.BlockSpec((D_MODEL, tn), lambda j: (0, j)),
                  pl.BlockSpec((1, tn), lambda j: (0, j))],
        out_specs=pl.BlockSpec((n, tn), lambda j: (0, j)),
        compiler_params=_params(("arbitrary",)),
        name="adaln_mod",
    )(c, w_ada, b_ada.reshape(1, width))


def _inproj_kernel(x_ref, sh_ref, sc_ref, g_ref, w_ref, cos_ref, sin_ref, o_ref, u_ref):
    j = pl.program_id(1)

    @pl.when(j == 0)
    def _():
        u = (_rms(x_ref[...]) * g_ref[...]) * (1.0 + sc_ref[...]) + sh_ref[...]
        u_ref[...] = u.astype(BF16)

    def tile(epilogue):
        w = 2 * MXU_N
        for c in range(PROJ_TN // w):
            cols = slice(c * w, (c + 1) * w)
            acc = jnp.dot(u_ref[...], w_ref[:, cols], preferred_element_type=F32)
            o_ref[:, cols] = epilogue(acc).astype(BF16)

    def rotary(acc):
        scale = jnp.where(j == OFF_RK // PROJ_TN, RET_DK ** -0.5, 1.0).astype(F32)
        cos, sin = cos_ref[...], sin_ref[...]
        heads = [acc[:, h * LANE:(h + 1) * LANE] for h in range(acc.shape[1] // LANE)]
        return jnp.concatenate(
            [(a * cos + pltpu.roll(a, LANE // 2, 1) * sin) * scale for a in heads], axis=-1)

    pl.when(j < OFF_RV // PROJ_TN)(lambda: tile(rotary))
    pl.when((j >= OFF_RG // PROJ_TN) & (j < OFF_AQ // PROJ_TN))(lambda: tile(lambda a: a * _sigmoid(a)))
    pl.when(j >= OFF_GR // PROJ_TN)(lambda: tile(_sigmoid))
    pl.when(((j >= OFF_RV // PROJ_TN) & (j < OFF_RG // PROJ_TN))
            | ((j >= OFF_AQ // PROJ_TN) & (j < OFF_GR // PROJ_TN)))(lambda: tile(lambda a: a))


def _in_proj(x, mod, g_pre1, w_in, cos, sin, tm):
    B, S, _ = x.shape
    R = mod.shape[1]
    tpb = S // tm
    assert S % tm == 0 and (R == 1 or tpb == 1)
    rm = R if R == 1 else tm
    return pl.pallas_call(
        _inproj_kernel,
        out_shape=jax.ShapeDtypeStruct((B, S, IN_WIDTH), BF16),
        grid=(B * tpb, IN_WIDTH // PROJ_TN),
        in_specs=[
            pl.BlockSpec((None, tm, D_MODEL), lambda i, j: (i // tpb, i % tpb, 0)),
            pl.BlockSpec((None, rm, D_MODEL), lambda i, j: (i // tpb, 0, 0)),
            pl.BlockSpec((None, rm, D_MODEL), lambda i, j: (i // tpb, 0, 1)),
            pl.BlockSpec((1, D_MODEL), lambda i, j: (0, 0)),
            pl.BlockSpec((D_MODEL, PROJ_TN), lambda i, j: (0, j)),
            pl.BlockSpec((tm, LANE), lambda i, j: (i % tpb, 0)),
            pl.BlockSpec((tm, LANE), lambda i, j: (i % tpb, 0)),
        ],
        out_specs=pl.BlockSpec((None, tm, PROJ_TN), lambda i, j: (i // tpb, i % tpb, j)),
        scratch_shapes=[pltpu.VMEM((tm, D_MODEL), BF16)],
        compiler_params=_params(("arbitrary", "arbitrary")),
        name="in_proj",
    )(x, mod, mod, g_pre1.reshape(1, D_MODEL), w_in, cos, sin)


def _retention_kernel(ds_ref, q_ref, k_ref, v_ref, rg_ref, dmat_ref, dq_ref, dk_ref, s0_ref,
                      o_ref, snew_ref, s_ref):
    t = pl.program_id(1)

    @pl.when(t == 0)
    def _():
        s_ref[...] = s0_ref[...]

    def dots(h):
        q = q_ref[:, h * RET_DK:(h + 1) * RET_DK]
        k = k_ref[:, h * RET_DK:(h + 1) * RET_DK]
        v = v_ref[:, h * RET_DV:(h + 1) * RET_DV]
        state = s_ref[h]
        s = lax.dot_general(q, k, (((1,), (1,)), ((), ())), preferred_element_type=F32)
        qs = jnp.dot(q, state.astype(BF16), preferred_element_type=F32)
        kd = (k.astype(F32) * dk_ref[h]).astype(BF16)
        kv = lax.dot_general(kd, v, (((0,), (0,)), ((), ())), preferred_element_type=F32)
        return s, qs, kv, state, v

    def finish(h, s, qs, kv, state, v):
        o = (jnp.dot((s * dmat_ref[h]).astype(BF16), v, preferred_element_type=F32)
             + qs * dq_ref[h])
        new_state = ds_ref[h] * state + kv
        s_ref[h] = new_state
        cols = slice(h * RET_DV, (h + 1) * RET_DV)
        o_ref[:, cols] = (rg_ref[:, cols].astype(F32) * _rms(o)).astype(BF16)

    pending = dots(0)
    for h in range(1, N_HEADS):
        nxt = dots(h)
        finish(h - 1, *pending)
        pending = nxt
    finish(N_HEADS - 1, *pending)

    @pl.when(t == pl.num_programs(1) - 1)
    def _():
        snew_ref[...] = s_ref[...]


def _retention(proj, state0, log_gamma, tile, chunk):
    B, S, _ = proj.shape
    nt = S // tile
    pos = jnp.arange(tile)
    ch = pos // chunk
    dist = (pos[:, None] - pos[None, :]).astype(F32)
    lg = log_gamma[:, None, None]
    same = ch[:, None] == ch[None, :]
    past = ch[None, :] < ch[:, None]
    dmat = jnp.where(same[None], jnp.exp(lg * jnp.abs(dist)[None]),
                     jnp.where(past[None], jnp.exp(lg * dist[None]), 0.0))
    tf = pos.astype(F32)
    dq = jnp.broadcast_to(jnp.exp((tf[None, :] + 1.0) * log_gamma[:, None])[:, :, None],
                          (N_HEADS, tile, RET_DV))
    dk = jnp.broadcast_to(jnp.exp((tile - 1.0 - tf)[None, :] * log_gamma[:, None])[:, :, None],
                          (N_HEADS, tile, RET_DK))
    ds = jnp.exp(tile * log_gamma)
    state_spec = pl.BlockSpec((None, N_HEADS, RET_DK, RET_DV), lambda b, t: (b, 0, 0, 0))
    return pl.pallas_call(
        _retention_kernel,
        out_shape=(jax.ShapeDtypeStruct((B, S, RV_W), BF16),
                   jax.ShapeDtypeStruct((B, N_HEADS, RET_DK, RET_DV), F32)),
        grid=(B, nt),
        in_specs=[
            pl.BlockSpec(memory_space=pltpu.SMEM),
            pl.BlockSpec((None, tile, QK_W), lambda b, t: (b, t, OFF_RQ // QK_W)),
            pl.BlockSpec((None, tile, QK_W), lambda b, t: (b, t, OFF_RK // QK_W)),
            pl.BlockSpec((None, tile, RV_W), lambda b, t: (b, t, OFF_RV // RV_W)),
            pl.BlockSpec((None, tile, RV_W), lambda b, t: (b, t, OFF_RG // RV_W)),
            _const_spec(dmat.shape), _const_spec(dq.shape), _const_spec(dk.shape),
            state_spec,
        ],
        out_specs=(pl.BlockSpec((None, tile, RV_W), lambda b, t: (b, t, 0)), state_spec),
        scratch_shapes=[pltpu.VMEM((N_HEADS, RET_DK, RET_DV), F32)],
        compiler_params=_params(("arbitrary", "arbitrary")),
        name="retention",
    )(ds, proj, proj, proj, proj, dmat, dq, dk, state0)


def _qk(q, k):
    s = lax.dot_general(q, k, (((1,), (1,)), ((), ())), preferred_element_type=F32)
    return s * ((ATT_DH ** -0.5) * LOG2E)


def _softmax_pv(scores, values):
    m = functools.reduce(jnp.maximum, [jnp.max(s, axis=-1, keepdims=True) for s in scores])
    ps = [jnp.exp2(s - m) for s in scores]
    l = functools.reduce(jnp.add, [jnp.sum(p, axis=-1, keepdims=True) for p in ps])
    o = functools.reduce(jnp.add, [jnp.dot(p.astype(BF16), v, preferred_element_type=F32)
                                   for p, v in zip(ps, values)])
    return o / l


def _per_head_pipelined(scores_fn, finish_fn):
    pending = scores_fn(0)
    for h in range(1, N_HEADS):
        nxt = scores_fn(h)
        finish_fn(h - 1, pending)
        pending = nxt
    finish_fn(N_HEADS - 1, pending)


def _bias_table_kernel(base_ref, o_ref):
    T, W = o_ref.shape
    n = base_ref.shape[-1]
    x = jnp.broadcast_to(base_ref[...], (T, n))
    x = pltpu.roll(x, n - T + 1, 1, stride=1, stride_axis=0)[:, :W]
    qc = lax.broadcasted_iota(jnp.int32, (T, W), 0) // CHUNK
    kc = lax.broadcasted_iota(jnp.int32, (T, W), 1) // CHUNK
    o_ref[...] = jnp.where((kc >= qc) & (kc <= qc + ATT_LEFT_CHUNKS), x * LOG2E, NEG)


def _bias_table(rel_bias):
    T = ATT_TILE
    W = (ATT_REACH // T + 1) * T
    n = 1024
    assert W + T - 1 <= n
    lo = (T - 1) + ATT_REACH - REL_CLIP
    hi = n - lo - (2 * REL_CLIP + 1)
    assert hi >= 0
    base = jnp.concatenate([jnp.broadcast_to(rel_bias[:, :1], (N_HEADS, lo)), rel_bias,
                            jnp.broadcast_to(rel_bias[:, -1:], (N_HEADS, hi))], axis=1)
    return pl.pallas_call(
        _bias_table_kernel,
        out_shape=jax.ShapeDtypeStruct((N_HEADS, T, W), F32),
        grid=(N_HEADS,),
        in_specs=[pl.BlockSpec((None, 1, n), lambda h: (h, 0, 0))],
        out_specs=pl.BlockSpec((None, T, W), lambda h: (h, 0, 0)),
        compiler_params=_params(("arbitrary",)),
        name="bias_table",
    )(base.reshape(N_HEADS, 1, n).astype(F32))


def _att_prompt_kernel(q_ref, k0_ref, k1_ref, k2_ref, v0_ref, v1_ref, v2_ref, tab_ref, o_ref):
    i = pl.program_id(1)
    T = ATT_TILE

    def scores(h):
        cols = slice(h * ATT_DH, (h + 1) * ATT_DH)
        q = q_ref[:, cols]
        s0 = _qk(q, k0_ref[:, cols]) + tab_ref[h, :, 0:T]
        s1 = _qk(q, k1_ref[:, cols]) + tab_ref[h, :, T:2 * T]
        s2 = _qk(q, k2_ref[:, cols]) + tab_ref[h, :, 2 * T:3 * T]
        return [jnp.where(i >= 2, s0, NEG), jnp.where(i >= 1, s1, NEG), s2]

    def finish(h, ss):
        cols = slice(h * ATT_DH, (h + 1) * ATT_DH)
        o = _softmax_pv(ss, [v0_ref[:, cols], v1_ref[:, cols], v2_ref[:, cols]])
        o_ref[:, cols] = o.astype(BF16)

    _per_head_pipelined(scores, finish)


def _attention_prompt(proj, rel_bias):
    B, S, _ = proj.shape
    T = ATT_TILE
    nq = S // T
    assert ATT_REACH // T + 1 == 3
    table = _bias_table(rel_bias)

    def blk(off, back):
        return pl.BlockSpec((None, T, QK_W), lambda b, i: (b, jnp.maximum(i - back, 0), off // QK_W))

    return pl.pallas_call(
        _att_prompt_kernel,
        out_shape=jax.ShapeDtypeStruct((B, S, QK_W), BF16),
        grid=(B, nq),
        in_specs=[blk(OFF_AQ, 0),
                  blk(OFF_AK, 2), blk(OFF_AK, 1), blk(OFF_AK, 0),
                  blk(OFF_AV, 2), blk(OFF_AV, 1), blk(OFF_AV, 0),
                  _const_spec(table.shape)],
        out_specs=pl.BlockSpec((None, T, QK_W), lambda b, i: (b, i, 0)),
        compiler_params=_params(("arbitrary", "arbitrary")),
        name="attention_prompt",
    )(proj, proj, proj, proj, proj, proj, proj, table)


def _att_sample_kernel(q_ref, kn_ref, vn_ref, kc_ref, vc_ref, tab_ref, o_ref):
    P = kc_ref.shape[0]

    def scores(h):
        cols = slice(h * ATT_DH, (h + 1) * ATT_DH)
        q = q_ref[:, cols]
        sc = _qk(q, kc_ref[:, h, :].astype(BF16)) + tab_ref[h, :, 0:P]
        sn = _qk(q, kn_ref[:, cols]) + tab_ref[h, :, P:]
        return [sc, sn]

    def finish(h, ss):
        cols = slice(h * ATT_DH, (h + 1) * ATT_DH)
        o = _softmax_pv(ss, [vc_ref[:, h, :].astype(BF16), vn_ref[:, cols]])
        o_ref[:, cols] = o.astype(BF16)

    _per_head_pipelined(scores, finish)


def _attention_sample(proj, cache_k, cache_v, layer, rel_bias):
    B, L, _ = proj.shape
    P = cache_k.shape[2]
    assert P == ATT_REACH and PAST_LEN >= ATT_REACH
    t = jnp.arange(L)
    rel = jnp.concatenate([jnp.arange(P)[None, :] - P - t[:, None], t[None, :] - t[:, None]], axis=1)
    table = rel_bias[:, jnp.clip(rel, -REL_CLIP, REL_CLIP) + REL_CLIP].astype(F32) * LOG2E
    cache_spec = pl.BlockSpec((None, None, P, N_HEADS, ATT_DH), lambda b: (layer, b, 0, 0, 0))
    return pl.pallas_call(
        _att_sample_kernel,
        out_shape=jax.ShapeDtypeStruct((B, L, QK_W), BF16),
        grid=(B,),
        in_specs=[pl.BlockSpec((None, L, QK_W), lambda b: (b, 0, OFF_AQ // QK_W)),
                  pl.BlockSpec((None, L, QK_W), lambda b: (b, 0, OFF_AK // QK_W)),
                  pl.BlockSpec((None, L, QK_W), lambda b: (b, 0, OFF_AV // QK_W)),
                  cache_spec, cache_spec,
                  _const_spec(table.shape)],
        out_specs=pl.BlockSpec((None, L, QK_W), lambda b: (b, 0, 0)),
        compiler_params=_params(("arbitrary",)),
        name="attention_sample",
    )(proj, proj, proj, cache_k, cache_v, table)


def _merge_kernel(gret_ref, oatt_ref, gr0_ref, gr1_ref, ga0_ref, ga1_ref, x_ref, gt_ref, g_ref,
                  wr_ref, wa_ref, wo_ref, o_ref, m_ref):
    y_ret = jnp.dot(gret_ref[...], wr_ref[...], preferred_element_type=F32)
    y_att = jnp.dot(oatt_ref[...], wa_ref[...], preferred_element_type=F32)
    half = D_MODEL // 2
    m_ref[:, :half] = (gr0_ref[...].astype(F32) * y_ret[:, :half]
                       + ga0_ref[...].astype(F32) * y_att[:, :half]).astype(BF16)
    m_ref[:, half:] = (gr1_ref[...].astype(F32) * y_ret[:, half:]
                       + ga1_ref[...].astype(F32) * y_att[:, half:]).astype(BF16)
    z = jnp.dot(m_ref[...], wo_ref[...], preferred_element_type=F32)
    o_ref[...] = x_ref[...] + gt_ref[...] * (_rms(z) * g_ref[...])


def _merge(x, g_ret, o_att, proj, mod, g_post1, w_br_ret, w_br_att, w_out, tm):
    B, S, _ = x.shape
    R = mod.shape[1]
    tpb = S // tm
    assert S % tm == 0 and (R == 1 or tpb == 1)
    rm = R if R == 1 else tm
    half = D_MODEL // 2
    row = lambda i: (i // tpb, i % tpb)

    def gate_spec(off):
        return pl.BlockSpec((None, tm, half), lambda i: (*row(i), off // half))

    return pl.pallas_call(
        _merge_kernel,
        out_shape=jax.ShapeDtypeStruct((B, S, D_MODEL), F32),
        grid=(B * tpb,),
        in_specs=[pl.BlockSpec((None, tm, RV_W), lambda i: (*row(i), 0)),
                  pl.BlockSpec((None, tm, QK_W), lambda i: (*row(i), 0)),
                  gate_spec(OFF_GR), gate_spec(OFF_GR + half),
                  gate_spec(OFF_GA), gate_spec(OFF_GA + half),
                  pl.BlockSpec((None, tm, D_MODEL), lambda i: (*row(i), 0)),
                  pl.BlockSpec((None, rm, D_MODEL), lambda i: (i // tpb, 0, 2)),
                  pl.BlockSpec((1, D_MODEL), lambda i: (0, 0)),
                  _const_spec(w_br_ret.shape), _const_spec(w_br_att.shape), _const_spec(w_out.shape)],
        out_specs=pl.BlockSpec((None, tm, D_MODEL), lambda i: (*row(i), 0)),
        scratch_shapes=[pltpu.VMEM((tm, D_MODEL), BF16)],
        compiler_params=_params(("arbitrary",)),
        name="merge_out",
    )(g_ret, o_att, proj, proj, proj, proj, x, mod, g_post1.reshape(1, D_MODEL),
      w_br_ret, w_br_att, w_out)


HALO = 8


def _ffn_kernel(x_ref, sh_ref, sc_ref, gt_ref, gpre_ref, gpost_ref, wv_ref, wg_ref, cwv_ref, cwg_ref,
                cbv_ref, cbg_ref, wd_ref, cin_ref, o_ref, cov_ref, cog_ref,
                u_ref, up_ref, h_ref, carry_ref, *, nb, tpb, nj):
    i, j = pl.program_id(0), pl.program_id(1)
    tn = FFN_TN
    L = x_ref.shape[0] // nb
    base = [s * (L + HALO) + HALO for s in range(nb)]
    rb = min(L, 64)

    def up_phase():
        for half, w_ref in enumerate((wv_ref, wg_ref)):
            acc = jnp.dot(u_ref[...], w_ref[...], preferred_element_type=F32)
            for s in range(nb):
                up_ref[base[s]:base[s] + L, half * tn:(half + 1) * tn] = acc[s * L:(s + 1) * L]

    def gate_down_phase():
        src = up_ref
        for s in range(nb):
            src[base[s] - 2:base[s], :] = carry_ref[j - 1, s]

        def conv(r0, cols, cw_ref, cb_ref, wc):
            cur = src[r0:r0 + rb, cols]
            p1 = src[r0 - 1:r0 - 1 + rb, cols]
            p2 = src[r0 - 2:r0 - 2 + rb, cols]
            return (cb_ref[:, wc] + cw_ref[2:3, wc] * cur + cw_ref[1:2, wc] * p1 + cw_ref[0:1, wc] * p2)

        for p in range(tn // FFN_SUB):
            wc = slice(p * FFN_SUB, (p + 1) * FFN_SUB)
            cg = slice(tn + p * FFN_SUB, tn + (p + 1) * FFN_SUB)
            for s in range(nb):
                for r in range(0, L, rb):
                    value = conv(base[s] + r, wc, cwv_ref, cbv_ref, wc)
                    gate = conv(base[s] + r, cg, cwg_ref, cbg_ref, wc)
                    h_ref[s * L + r:s * L + r + rb, wc] = (_gelu_tanh(gate) * value).astype(BF16)

        for s in range(nb):
            last = src[base[s] + L - 2:base[s] + L, :]
            carry_ref[j - 1, s] = last
            cov_ref[s] = last[:, :tn]
            cog_ref[s] = last[:, tn:]

        w = 512
        for n in range(D_MODEL // w):
            o_ref[:, n * w:(n + 1) * w] += jnp.dot(h_ref[...], wd_ref[:, n * w:(n + 1) * w],
                                                   preferred_element_type=F32)

    @pl.when(j == 0)
    def _():
        u = (_rms(x_ref[...]) * gpre_ref[...]) * (1.0 + sc_ref[...]) + sh_ref[...]
        u_ref[...] = u.astype(BF16)
        o_ref[...] = jnp.zeros(o_ref.shape, F32)

        @pl.when(i % tpb == 0)
        def _():
            for jj in range(nj):
                carry_ref[jj, :, :, :tn] = cin_ref[:, :, jj * tn:(jj + 1) * tn]
                carry_ref[jj, :, :, tn:] = cin_ref[:, :, D_FF + jj * tn:D_FF + (jj + 1) * tn]

        up_phase()

    @pl.when((j >= 1) & (j < nj))
    def _():
        gate_down_phase()
        up_phase()

    @pl.when(j == nj)
    def _():
        gate_down_phase()
        o_ref[...] = x_ref[...] + gt_ref[...] * (_rms(o_ref[...]) * gpost_ref[...])


def _ffn(x, mod, conv_in, g_pre2, g_post2, w_up, conv_w, conv_b, w_down, tm, nb):
    B, S, _ = x.shape
    R = mod.shape[1]
    tpb = S // tm
    assert S % tm == 0 and (R == 1 or tpb == 1) and (nb == 1 or tpb == 1)
    rm = R if R == 1 else tm
    tn = FFN_TN
    nj = D_FF // tn
    row = lambda i: (i // tpb, i % tpb)
    t_up = lambda j: jnp.minimum(j, nj - 1)
    t_gate = lambda j: jnp.maximum(j - 1, 0)
    conv_b2 = conv_b.reshape(1, 2 * D_FF)
    conv_out = jax.ShapeDtypeStruct((B * tpb * nb, CONV_W - 1, D_FF), F32)
    conv_out_spec = pl.BlockSpec((nb, CONV_W - 1, tn), lambda i, j: (i, 0, t_gate(j)))
    up_rows = nb * (tm // nb + HALO)
    y, co_v, co_g = pl.pallas_call(
        functools.partial(_ffn_kernel, nb=nb, tpb=tpb, nj=nj),
        out_shape=(jax.ShapeDtypeStruct((B, S, D_MODEL), F32), conv_out, conv_out),
        grid=(B * tpb, nj + 1),
        in_specs=[pl.BlockSpec((None, tm, D_MODEL), lambda i, j: (*row(i), 0)),
                  pl.BlockSpec((None, rm, D_MODEL), lambda i, j: (i // tpb, 0, 3)),
                  pl.BlockSpec((None, rm, D_MODEL), lambda i, j: (i // tpb, 0, 4)),
                  pl.BlockSpec((None, rm, D_MODEL), lambda i, j: (i // tpb, 0, 5)),
                  pl.BlockSpec((1, D_MODEL), lambda i, j: (0, 0)),
                  pl.BlockSpec((1, D_MODEL), lambda i, j: (0, 0)),
                  pl.BlockSpec((D_MODEL, tn), lambda i, j: (0, t_up(j))),
                  pl.BlockSpec((D_MODEL, tn), lambda i, j: (0, nj + t_up(j))),
                  pl.BlockSpec((CONV_W, tn), lambda i, j: (0, t_gate(j))),
                  pl.BlockSpec((CONV_W, tn), lambda i, j: (0, nj + t_gate(j))),
                  pl.BlockSpec((1, tn), lambda i, j: (0, t_gate(j))),
                  pl.BlockSpec((1, tn), lambda i, j: (0, nj + t_gate(j))),
                  pl.BlockSpec((tn, D_MODEL), lambda i, j: (t_gate(j), 0)),
                  pl.BlockSpec((nb, CONV_W - 1, 2 * D_FF), lambda i, j: (i // tpb, 0, 0))],
        out_specs=(pl.BlockSpec((None, tm, D_MODEL), lambda i, j: (*row(i), 0)),
                   conv_out_spec, conv_out_spec),
        scratch_shapes=[pltpu.VMEM((tm, D_MODEL), BF16),
                        pltpu.VMEM((up_rows, 2 * tn), F32),
                        pltpu.VMEM((tm, tn), BF16),
                        pltpu.VMEM((nj, nb, CONV_W - 1, 2 * tn), F32)],
        compiler_params=_params(("arbitrary", "arbitrary")),
        name="conv_ffn",
    )(x, mod, mod, mod, g_pre2.reshape(1, D_MODEL), g_post2.reshape(1, D_MODEL), w_up, w_up,
      conv_w, conv_w, conv_b2, conv_b2, w_down, conv_in)
    return y, jnp.concatenate([co_v, co_g], axis=-1)


def _rotary_tables(pos):
    half = RET_DK // 2
    inv = ROPE_BASE ** (-jnp.arange(half, dtype=F32) / half)
    ang = pos.astype(F32)[:, None] * inv[None, :]
    cos, sin = jnp.cos(ang), jnp.sin(ang)
    return jnp.concatenate([cos, cos], axis=-1), jnp.concatenate([-sin, sin], axis=-1)


def kernel(x_prompt, x_sample, cache_att_k, cache_att_v, state_ret, state_conv, c_prompt, c_sample,
           w_ada, b_ada, g_pre1, w_in, rel_bias, w_br_ret, w_br_att, w_out, g_post1, g_pre2,
           w_up, conv_w, conv_b, w_down, g_post2):
    depth = w_ada.shape[0]
    Bp, Sp, _ = x_prompt.shape
    Bs, Ls, _ = x_sample.shape
    log_gamma = jnp.log(1.0 - 2.0 ** (-5.0 - jnp.arange(N_HEADS, dtype=F32)))
    cos_p, sin_p = _rotary_tables(jnp.arange(Sp))
    cos_s, sin_s = _rotary_tables(PAST_LEN + jnp.arange(Ls))
    cos_s, sin_s = jnp.tile(cos_s, (Bs, 1)), jnp.tile(sin_s, (Bs, 1))
    keep = min(ATT_REACH, Sp)
    tm_p, tm_wide = 512, 1024

    yp = x_prompt
    ys = x_sample.reshape(1, Bs * Ls, D_MODEL)
    outs = [[] for _ in range(8)]
    for l in range(depth):
        w_in_b, w_up_b, w_down_b = w_in[l].astype(BF16), w_up[l].astype(BF16), w_down[l].astype(BF16)
        w_br_ret_b, w_br_att_b, w_out_b = (w_br_ret[l].astype(BF16), w_br_att[l].astype(BF16),
                                           w_out[l].astype(BF16))
        mod = _mod(jnp.concatenate([c_prompt, c_sample], axis=0), w_ada[l], b_ada[l])
        mod_p = mod[:Bp].reshape(Bp, 1, 6 * D_MODEL)
        mod_s = jnp.repeat(mod[Bp:], Ls, axis=0).reshape(1, Bs * Ls, 6 * D_MODEL)

        proj = _in_proj(yp, mod_p, g_pre1[l], w_in_b, cos_p, sin_p, tm=tm_wide)
        zero_state = jnp.zeros((Bp, N_HEADS, RET_DK, RET_DV), F32)
        g_ret, r_p = _retention(proj, zero_state, log_gamma, RET_TILE, CHUNK)
        o_att = _attention_prompt(proj, rel_bias[l])
        x1 = _merge(yp, g_ret, o_att, proj, mod_p, g_post1[l], w_br_ret_b, w_br_att_b, w_out_b, tm=tm_p)
        conv0 = jnp.zeros((Bp, CONV_W - 1, 2 * D_FF), F32)
        yp, cv_p = _ffn(x1, mod_p, conv0, g_pre2[l], g_post2[l], w_up_b, conv_w[l], conv_b[l],
                        w_down_b, tm=tm_wide, nb=1)
        cv_p = cv_p.reshape(Bp, Sp // tm_wide, CONV_W - 1, 2 * D_FF)[:, -1]
        k_p = proj[:, Sp - keep:, OFF_AK:OFF_AV].astype(F32).reshape(Bp, keep, N_HEADS, ATT_DH)
        v_p = proj[:, Sp - keep:, OFF_AV:OFF_GR].astype(F32).reshape(Bp, keep, N_HEADS, ATT_DH)

        proj_s = _in_proj(ys, mod_s, g_pre1[l], w_in_b, cos_s, sin_s, tm=Bs * Ls)
        proj_s3 = proj_s.reshape(Bs, Ls, IN_WIDTH)
        g_ret_s, r_s = _retention(proj_s3, state_ret[l], log_gamma, Ls, Ls)
        o_att_s = _attention_sample(proj_s3, cache_att_k, cache_att_v, l, rel_bias[l])
        x1_s = _merge(ys, g_ret_s.reshape(1, Bs * Ls, -1), o_att_s.reshape(1, Bs * Ls, -1), proj_s,
                      mod_s, g_post1[l], w_br_ret_b, w_br_att_b, w_out_b, tm=Bs * Ls)
        ys, cv_s = _ffn(x1_s, mod_s, state_conv[l], g_pre2[l], g_post2[l], w_up_b, conv_w[l],
                        conv_b[l], w_down_b, tm=Bs * Ls, nb=Bs)
        k_s = proj_s3[:, :, OFF_AK:OFF_AV].astype(F32).reshape(Bs, Ls, N_HEADS, ATT_DH)
        v_s = proj_s3[:, :, OFF_AV:OFF_GR].astype(F32).reshape(Bs, Ls, N_HEADS, ATT_DH)

        for lst, val in zip(outs, (k_p, v_p, r_p, cv_p, k_s, v_s, r_s, cv_s)):
            lst.append(val)
    return (yp, ys.reshape(Bs, Ls, D_MODEL), *[jnp.stack(o) for o in outs])
```

```python
import functools
import math

import jax
import jax.numpy as jnp
from jax import lax
from jax.experimental import pallas as pl
from jax.experimental.pallas import tpu as pltpu

F32 = jnp.float32
BF16 = jnp.bfloat16

D_MODEL = 2048
PAST_LEN = 4096
CHUNK = 64
N_HEADS = 8
RET_DK = 128
RET_DV = 256
ATT_DH = 128
ATT_LEFT_CHUNKS = 8
ATT_REACH = ATT_LEFT_CHUNKS * CHUNK
REL_CLIP = 128
D_FF = 5632
CONV_W = 3
ROPE_BASE = 10000.0
EPS = 1e-6
IN_WIDTH = 13312
NEG = -1e30
LOG2E = math.log2(math.e)

OFF_RQ, OFF_RK, OFF_RV, OFF_RG = 0, 1024, 2048, 4096
OFF_AQ, OFF_AK, OFF_AV, OFF_GR, OFF_GA = 6144, 7168, 8192, 9216, 11264
QK_W = N_HEADS * RET_DK
RV_W = N_HEADS * RET_DV

LANE = 128
MXU_N = 256
VMEM_LIMIT = 62 * 1024 * 1024

PROJ_TN = 1024
RET_TILE = 256
ATT_TILE = 256
FFN_TN = 512
FFN_SUB = 128


def _params(sem):
    return pltpu.CompilerParams(dimension_semantics=sem, vmem_limit_bytes=VMEM_LIMIT)


def _rms(x):
    return x * lax.rsqrt(jnp.mean(x * x, axis=-1, keepdims=True) + EPS)


def _sigmoid(x):
    return 1.0 / (1.0 + jnp.exp2(x * (-LOG2E)))


def _gelu_tanh(x):
    k = -2.0 * LOG2E * math.sqrt(2.0 / math.pi)
    return x / (1.0 + jnp.exp2(x * (k + (k * 0.044715) * (x * x))))


def _const_spec(shape):
    zeros = (0,) * len(shape)
    return pl.BlockSpec(shape, lambda *_: zeros, pipeline_mode=pl.Buffered(1))


def _mod_kernel(c_ref, w_ref, b_ref, o_ref):
    a = jax.nn.silu(c_ref[...]).astype(BF16)
    o_ref[...] = jnp.dot(a, w_ref[...].astype(BF16), preferred_element_type=F32) + b_ref[...]


def _mod(c, w_ada, b_ada):
    n, tn = c.shape[0], 1024
    width = w_ada.shape[1]
    return pl.pallas_call(
        _mod_kernel,
        out_shape=jax.ShapeDtypeStruct((n, width), F32),
        grid=(width // tn,),
        in_specs=[pl.BlockSpec((n, D_MODEL), lambda j: (0, 0)),
                  pl.BlockSpec((D_MODEL, tn), lambda j: (0, j)),
                  pl.BlockSpec((1, tn), lambda j: (0, j))],
        out_specs=pl.BlockSpec((n, tn), lambda j: (0, j)),
        compiler_params=_params(("arbitrary",)),
        name="adaln_mod",
    )(c, w_ada, b_ada.reshape(1, width))


def _inproj_kernel(x_ref, sh_ref, sc_ref, g_ref, w_ref, cos_ref, sin_ref, o_ref, *rest):
    u_ref = rest[-1]
    j = pl.program_id(1)

    @pl.when(j == 0)
    def _():
        u = (_rms(x_ref[...]) * g_ref[...]) * (1.0 + sc_ref[...]) + sh_ref[...]
        u_ref[...] = u.astype(BF16)

    def tile(epilogue):
        w = 2 * MXU_N
        for c in range(PROJ_TN // w):
            cols = slice(c * w, (c + 1) * w)
            wc = w_ref[:, cols]
            if len(rest) == 2:
                wc = wc.astype(BF16)
                rest[0][:, cols] = wc
            acc = jnp.dot(u_ref[...], wc, preferred_element_type=F32)
            o_ref[:, cols] = epilogue(acc).astype(BF16)

    def rotary(acc):
        scale = jnp.where(j == OFF_RK // PROJ_TN, RET_DK ** -0.5, 1.0).astype(F32)
        cos, sin = cos_ref[...], sin_ref[...]
        heads = [acc[:, h * LANE:(h + 1) * LANE] for h in range(acc.shape[1] // LANE)]
        return jnp.concatenate(
            [(a * cos + pltpu.roll(a, LANE // 2, 1) * sin) * scale for a in heads], axis=-1)

    pl.when(j < OFF_RV // PROJ_TN)(lambda: tile(rotary))
    pl.when((j >= OFF_RG // PROJ_TN) & (j < OFF_AQ // PROJ_TN))(lambda: tile(lambda a: a * _sigmoid(a)))
    pl.when(j >= OFF_GR // PROJ_TN)(lambda: tile(_sigmoid))
    pl.when(((j >= OFF_RV // PROJ_TN) & (j < OFF_RG // PROJ_TN))
            | ((j >= OFF_AQ // PROJ_TN) & (j < OFF_GR // PROJ_TN)))(lambda: tile(lambda a: a))


def _in_proj(x, mod, g_pre1, w_in, cos, sin, tm):
    B, S, _ = x.shape
    R = mod.shape[1]
    tpb = S // tm
    assert S % tm == 0 and (R == 1 or tpb == 1)
    rm = R if R == 1 else tm
    emit_w = w_in.dtype == F32
    assert not emit_w or B * tpb == 1
    w_spec = pl.BlockSpec((D_MODEL, PROJ_TN), lambda i, j: (0, j))
    proj_shape = jax.ShapeDtypeStruct((B, S, IN_WIDTH), BF16)
    proj_spec = pl.BlockSpec((None, tm, PROJ_TN), lambda i, j: (i // tpb, i % tpb, j))
    return pl.pallas_call(
        _inproj_kernel,
        out_shape=(proj_shape, jax.ShapeDtypeStruct(w_in.shape, BF16)) if emit_w else proj_shape,
        grid=(B * tpb, IN_WIDTH // PROJ_TN),
        in_specs=[
            pl.BlockSpec((None, tm, D_MODEL), lambda i, j: (i // tpb, i % tpb, 0)),
            pl.BlockSpec((None, rm, D_MODEL), lambda i, j: (i // tpb, 0, 0)),
            pl.BlockSpec((None, rm, D_MODEL), lambda i, j: (i // tpb, 0, 1)),
            pl.BlockSpec((1, D_MODEL), lambda i, j: (0, 0)),
            w_spec,
            pl.BlockSpec((tm, LANE), lambda i, j: (i % tpb, 0)),
            pl.BlockSpec((tm, LANE), lambda i, j: (i % tpb, 0)),
        ],
        out_specs=(proj_spec, w_spec) if emit_w else proj_spec,
        scratch_shapes=[pltpu.VMEM((tm, D_MODEL), BF16)],
        compiler_params=_params(("arbitrary", "arbitrary")),
        name="in_proj",
    )(x, mod, mod, g_pre1.reshape(1, D_MODEL), w_in, cos, sin)


def _retention_kernel(ds_ref, q_ref, k_ref, v_ref, rg_ref, dmat_ref, dq_ref, dk_ref, s0_ref,
                      o_ref, snew_ref, s_ref):
    t = pl.program_id(1)

    @pl.when(t == 0)
    def _():
        s_ref[...] = s0_ref[...]

    def dots(h):
        q = q_ref[:, h * RET_DK:(h + 1) * RET_DK]
        k = k_ref[:, h * RET_DK:(h + 1) * RET_DK]
        v = v_ref[:, h * RET_DV:(h + 1) * RET_DV]
        state = s_ref[h]
        s = lax.dot_general(q, k, (((1,), (1,)), ((), ())), preferred_element_type=F32)
        qs = jnp.dot(q, state.astype(BF16), preferred_element_type=F32)
        kd = (k.astype(F32) * dk_ref[h]).astype(BF16)
        kv = lax.dot_general(kd, v, (((0,), (0,)), ((), ())), preferred_element_type=F32)
        return s, qs, kv, state, v

    def finish(h, s, qs, kv, state, v):
        o = (jnp.dot((s * dmat_ref[h]).astype(BF16), v, preferred_element_type=F32)
             + qs * dq_ref[h])
        new_state = ds_ref[h] * state + kv
        s_ref[h] = new_state
        cols = slice(h * RET_DV, (h + 1) * RET_DV)
        o_ref[:, cols] = (rg_ref[:, cols].astype(F32) * _rms(o)).astype(BF16)

    pending = dots(0)
    for h in range(1, N_HEADS):
        nxt = dots(h)
        finish(h - 1, *pending)
        pending = nxt
    finish(N_HEADS - 1, *pending)

    @pl.when(t == pl.num_programs(1) - 1)
    def _():
        snew_ref[...] = s_ref[...]


def _retention(proj, state0, log_gamma, tile, chunk):
    B, S, _ = proj.shape
    nt = S // tile
    pos = jnp.arange(tile)
    ch = pos // chunk
    dist = (pos[:, None] - pos[None, :]).astype(F32)
    lg = log_gamma[:, None, None]
    same = ch[:, None] == ch[None, :]
    past = ch[None, :] < ch[:, None]
    dmat = jnp.where(same[None], jnp.exp(lg * jnp.abs(dist)[None]),
                     jnp.where(past[None], jnp.exp(lg * dist[None]), 0.0))
    tf = pos.astype(F32)
    dq = jnp.broadcast_to(jnp.exp((tf[None, :] + 1.0) * log_gamma[:, None])[:, :, None],
                          (N_HEADS, tile, RET_DV))
    dk = jnp.broadcast_to(jnp.exp((tile - 1.0 - tf)[None, :] * log_gamma[:, None])[:, :, None],
                          (N_HEADS, tile, RET_DK))
    ds = jnp.exp(tile * log_gamma)
    state_spec = pl.BlockSpec((None, N_HEADS, RET_DK, RET_DV), lambda b, t: (b, 0, 0, 0))
    return pl.pallas_call(
        _retention_kernel,
        out_shape=(jax.ShapeDtypeStruct((B, S, RV_W), BF16),
                   jax.ShapeDtypeStruct((B, N_HEADS, RET_DK, RET_DV), F32)),
        grid=(B, nt),
        in_specs=[
            pl.BlockSpec(memory_space=pltpu.SMEM),
            pl.BlockSpec((None, tile, QK_W), lambda b, t: (b, t, OFF_RQ // QK_W)),
            pl.BlockSpec((None, tile, QK_W), lambda b, t: (b, t, OFF_RK // QK_W)),
            pl.BlockSpec((None, tile, RV_W), lambda b, t: (b, t, OFF_RV // RV_W)),
            pl.BlockSpec((None, tile, RV_W), lambda b, t: (b, t, OFF_RG // RV_W)),
            _const_spec(dmat.shape), _const_spec(dq.shape), _const_spec(dk.shape),
            state_spec,
        ],
        out_specs=(pl.BlockSpec((None, tile, RV_W), lambda b, t: (b, t, 0)), state_spec),
        scratch_shapes=[pltpu.VMEM((N_HEADS, RET_DK, RET_DV), F32)],
        compiler_params=_params(("arbitrary", "arbitrary")),
        name="retention",
    )(ds, proj, proj, proj, proj, dmat, dq, dk, state0)


def _qk(q, k):
    s = lax.dot_general(q, k, (((1,), (1,)), ((), ())), preferred_element_type=F32)
    return s * ((ATT_DH ** -0.5) * LOG2E)


def _softmax_pv(scores, values):
    m = functools.reduce(jnp.maximum, [jnp.max(s, axis=-1, keepdims=True) for s in scores])
    ps = [jnp.exp2(s - m) for s in scores]
    l = functools.reduce(jnp.add, [jnp.sum(p, axis=-1, keepdims=True) for p in ps])
    o = functools.reduce(jnp.add, [jnp.dot(p.astype(BF16), v, preferred_element_type=F32)
                                   for p, v in zip(ps, values)])
    return o / l


def _per_head_pipelined(scores_fn, finish_fn):
    pending = scores_fn(0)
    for h in range(1, N_HEADS):
        nxt = scores_fn(h)
        finish_fn(h - 1, pending)
        pending = nxt
    finish_fn(N_HEADS - 1, pending)


def _bias_table_kernel(base_ref, o_ref):
    T, W = o_ref.shape
    n = base_ref.shape[-1]
    x = jnp.broadcast_to(base_ref[...], (T, n))
    x = pltpu.roll(x, n - T + 1, 1, stride=1, stride_axis=0)[:, :W]
    qc = lax.broadcasted_iota(jnp.int32, (T, W), 0) // CHUNK
    kc = lax.broadcasted_iota(jnp.int32, (T, W), 1) // CHUNK
    o_ref[...] = jnp.where((kc >= qc) & (kc <= qc + ATT_LEFT_CHUNKS), x * LOG2E, NEG)


def _bias_table(rel_bias):
    T = ATT_TILE
    W = (ATT_REACH // T + 1) * T
    n = 1024
    assert W + T - 1 <= n
    lo = (T - 1) + ATT_REACH - REL_CLIP
    hi = n - lo - (2 * REL_CLIP + 1)
    assert hi >= 0
    base = jnp.concatenate([jnp.broadcast_to(rel_bias[:, :1], (N_HEADS, lo)), rel_bias,
                            jnp.broadcast_to(rel_bias[:, -1:], (N_HEADS, hi))], axis=1)
    return pl.pallas_call(
        _bias_table_kernel,
        out_shape=jax.ShapeDtypeStruct((N_HEADS, T, W), F32),
        grid=(N_HEADS,),
        in_specs=[pl.BlockSpec((None, 1, n), lambda h: (h, 0, 0))],
        out_specs=pl.BlockSpec((None, T, W), lambda h: (h, 0, 0)),
        compiler_params=_params(("arbitrary",)),
        name="bias_table",
    )(base.reshape(N_HEADS, 1, n).astype(F32))


def _att_prompt_kernel(q_ref, k0_ref, k1_ref, k2_ref, v0_ref, v1_ref, v2_ref, tab_ref, o_ref):
    i = pl.program_id(1)
    T = ATT_TILE

    def scores(h):
        cols = slice(h * ATT_DH, (h + 1) * ATT_DH)
        q = q_ref[:, cols]
        s0 = _qk(q, k0_ref[:, cols]) + tab_ref[h, :, 0:T]
        s1 = _qk(q, k1_ref[:, cols]) + tab_ref[h, :, T:2 * T]
        s2 = _qk(q, k2_ref[:, cols]) + tab_ref[h, :, 2 * T:3 * T]
        return [jnp.where(i >= 2, s0, NEG), jnp.where(i >= 1, s1, NEG), s2]

    def finish(h, ss):
        cols = slice(h * ATT_DH, (h + 1) * ATT_DH)
        o = _softmax_pv(ss, [v0_ref[:, cols], v1_ref[:, cols], v2_ref[:, cols]])
        o_ref[:, cols] = o.astype(BF16)

    _per_head_pipelined(scores, finish)


def _attention_prompt(proj, rel_bias):
    B, S, _ = proj.shape
    T = ATT_TILE
    nq = S // T
    assert ATT_REACH // T + 1 == 3
    table = _bias_table(rel_bias)

    def blk(off, back):
        return pl.BlockSpec((None, T, QK_W), lambda b, i: (b, jnp.maximum(i - back, 0), off // QK_W))

    return pl.pallas_call(
        _att_prompt_kernel,
        out_shape=jax.ShapeDtypeStruct((B, S, QK_W), BF16),
        grid=(B, nq),
        in_specs=[blk(OFF_AQ, 0),
                  blk(OFF_AK, 2), blk(OFF_AK, 1), blk(OFF_AK, 0),
                  blk(OFF_AV, 2), blk(OFF_AV, 1), blk(OFF_AV, 0),
                  _const_spec(table.shape)],
        out_specs=pl.BlockSpec((None, T, QK_W), lambda b, i: (b, i, 0)),
        compiler_params=_params(("arbitrary", "arbitrary")),
        name="attention_prompt",
    )(proj, proj, proj, proj, proj, proj, proj, table)


def _att_sample_kernel(q_ref, kn_ref, vn_ref, kc_ref, vc_ref, tab_ref, o_ref):
    P = kc_ref.shape[0]

    def scores(h):
        cols = slice(h * ATT_DH, (h + 1) * ATT_DH)
        q = q_ref[:, cols]
        sc = _qk(q, kc_ref[:, h, :].astype(BF16)) + tab_ref[h, :, 0:P]
        sn = _qk(q, kn_ref[:, cols]) + tab_ref[h, :, P:]
        return [sc, sn]

    def finish(h, ss):
        cols = slice(h * ATT_DH, (h + 1) * ATT_DH)
        o = _softmax_pv(ss, [vc_ref[:, h, :].astype(BF16), vn_ref[:, cols]])
        o_ref[:, cols] = o.astype(BF16)

    _per_head_pipelined(scores, finish)


def _attention_sample(proj, cache_k, cache_v, layer, rel_bias):
    B, L, _ = proj.shape
    P = cache_k.shape[2]
    assert P == ATT_REACH and PAST_LEN >= ATT_REACH
    t = jnp.arange(L)
    rel = jnp.concatenate([jnp.arange(P)[None, :] - P - t[:, None], t[None, :] - t[:, None]], axis=1)
    table = rel_bias[:, jnp.clip(rel, -REL_CLIP, REL_CLIP) + REL_CLIP].astype(F32) * LOG2E
    cache_spec = pl.BlockSpec((None, None, P, N_HEADS, ATT_DH), lambda b: (layer, b, 0, 0, 0))
    return pl.pallas_call(
        _att_sample_kernel,
        out_shape=jax.ShapeDtypeStruct((B, L, QK_W), BF16),
        grid=(B,),
        in_specs=[pl.BlockSpec((None, L, QK_W), lambda b: (b, 0, OFF_AQ // QK_W)),
                  pl.BlockSpec((None, L, QK_W), lambda b: (b, 0, OFF_AK // QK_W)),
                  pl.BlockSpec((None, L, QK_W), lambda b: (b, 0, OFF_AV // QK_W)),
                  cache_spec, cache_spec,
                  _const_spec(table.shape)],
        out_specs=pl.BlockSpec((None, L, QK_W), lambda b: (b, 0, 0)),
        compiler_params=_params(("arbitrary",)),
        name="attention_sample",
    )(proj, proj, proj, cache_k, cache_v, table)


def _merge_kernel(gret_ref, oatt_ref, gr0_ref, gr1_ref, ga0_ref, ga1_ref, x_ref, gt_ref, g_ref,
                  wr_ref, wa_ref, wo_ref, o_ref, m_ref):
    y_ret = jnp.dot(gret_ref[...], wr_ref[...], preferred_element_type=F32)
    y_att = jnp.dot(oatt_ref[...], wa_ref[...], preferred_element_type=F32)
    half = D_MODEL // 2
    m_ref[:, :half] = (gr0_ref[...].astype(F32) * y_ret[:, :half]
                       + ga0_ref[...].astype(F32) * y_att[:, :half]).astype(BF16)
    m_ref[:, half:] = (gr1_ref[...].astype(F32) * y_ret[:, half:]
                       + ga1_ref[...].astype(F32) * y_att[:, half:]).astype(BF16)
    z = jnp.dot(m_ref[...], wo_ref[...], preferred_element_type=F32)
    o_ref[...] = x_ref[...] + gt_ref[...] * (_rms(z) * g_ref[...])


def _merge(x, g_ret, o_att, proj, mod, g_post1, w_br_ret, w_br_att, w_out, tm):
    B, S, _ = x.shape
    R = mod.shape[1]
    tpb = S // tm
    assert S % tm == 0 and (R == 1 or tpb == 1)
    rm = R if R == 1 else tm
    half = D_MODEL // 2
    row = lambda i: (i // tpb, i % tpb)

    def gate_spec(off):
        return pl.BlockSpec((None, tm, half), lambda i: (*row(i), off // half))

    return pl.pallas_call(
        _merge_kernel,
        out_shape=jax.ShapeDtypeStruct((B, S, D_MODEL), F32),
        grid=(B * tpb,),
        in_specs=[pl.BlockSpec((None, tm, RV_W), lambda i: (*row(i), 0)),
                  pl.BlockSpec((None, tm, QK_W), lambda i: (*row(i), 0)),
                  gate_spec(OFF_GR), gate_spec(OFF_GR + half),
                  gate_spec(OFF_GA), gate_spec(OFF_GA + half),
                  pl.BlockSpec((None, tm, D_MODEL), lambda i: (*row(i), 0)),
                  pl.BlockSpec((None, rm, D_MODEL), lambda i: (i // tpb, 0, 2)),
                  pl.BlockSpec((1, D_MODEL), lambda i: (0, 0)),
                  _const_spec(w_br_ret.shape), _const_spec(w_br_att.shape), _const_spec(w_out.shape)],
        out_specs=pl.BlockSpec((None, tm, D_MODEL), lambda i: (*row(i), 0)),
        scratch_shapes=[pltpu.VMEM((tm, D_MODEL), BF16)],
        compiler_params=_params(("arbitrary",)),
        name="merge_out",
    )(g_ret, o_att, proj, proj, proj, proj, x, mod, g_post1.reshape(1, D_MODEL),
      w_br_ret, w_br_att, w_out)


HALO = 8


def _ffn_kernel(x_ref, sh_ref, sc_ref, gt_ref, gpre_ref, gpost_ref, wv_ref, wg_ref, cwv_ref, cwg_ref,
                cbv_ref, cbg_ref, wd_ref, cin_ref, o_ref, cov_ref, cog_ref, *rest, nb, tpb, nj):
    u_ref, up_ref, h_ref, carry_ref = rest[-4:]
    wvb_ref, wgb_ref, wdb_ref = rest[:-4] if len(rest) > 4 else (None, None, None)
    i, j = pl.program_id(0), pl.program_id(1)
    tn = FFN_TN
    L = x_ref.shape[0] // nb
    base = [s * (L + HALO) + HALO for s in range(nb)]
    rb = min(L, 64)

    def bf16_weights(w_ref, cast_ref, idx):
        w = w_ref[idx]
        if cast_ref is not None:
            w = w.astype(BF16)
            cast_ref[idx] = w
        return w

    def up_phase():
        for half, (w_ref, cast_ref) in enumerate(((wv_ref, wvb_ref), (wg_ref, wgb_ref))):
            acc = jnp.dot(u_ref[...], bf16_weights(w_ref, cast_ref, (slice(None), slice(None))),
                          preferred_element_type=F32)
            for s in range(nb):
                up_ref[base[s]:base[s] + L, half * tn:(half + 1) * tn] = acc[s * L:(s + 1) * L]

    def gate_down_phase():
        src = up_ref
        for s in range(nb):
            src[base[s] - 2:base[s], :] = carry_ref[j - 1, s]

        def conv(r0, cols, cw_ref, cb_ref, wc):
            cur = src[r0:r0 + rb, cols]
            p1 = src[r0 - 1:r0 - 1 + rb, cols]
            p2 = src[r0 - 2:r0 - 2 + rb, cols]
            return (cb_ref[:, wc] + cw_ref[2:3, wc] * cur + cw_ref[1:2, wc] * p1 + cw_ref[0:1, wc] * p2)

        for p in range(tn // FFN_SUB):
            wc = slice(p * FFN_SUB, (p + 1) * FFN_SUB)
            cg = slice(tn + p * FFN_SUB, tn + (p + 1) * FFN_SUB)
            for s in range(nb):
                for r in range(0, L, rb):
                    value = conv(base[s] + r, wc, cwv_ref, cbv_ref, wc)
                    gate = conv(base[s] + r, cg, cwg_ref, cbg_ref, wc)
                    h_ref[s * L + r:s * L + r + rb, wc] = (_gelu_tanh(gate) * value).astype(BF16)

        for s in range(nb):
            last = src[base[s] + L - 2:base[s] + L, :]
            carry_ref[j - 1, s] = last
            cov_ref[s] = last[:, :tn]
            cog_ref[s] = last[:, tn:]

        w = 512
        for n in range(D_MODEL // w):
            wd = bf16_weights(wd_ref, wdb_ref, (slice(None), slice(n * w, (n + 1) * w)))
            o_ref[:, n * w:(n + 1) * w] += jnp.dot(h_ref[...], wd, preferred_element_type=F32)

    @pl.when(j == 0)
    def _():
        u = (_rms(x_ref[...]) * gpre_ref[...]) * (1.0 + sc_ref[...]) + sh_ref[...]
        u_ref[...] = u.astype(BF16)
        o_ref[...] = jnp.zeros(o_ref.shape, F32)

        @pl.when(i % tpb == 0)
        def _():
            for jj in range(nj):
                carry_ref[jj, :, :, :tn] = cin_ref[:, :, jj * tn:(jj + 1) * tn]
                carry_ref[jj, :, :, tn:] = cin_ref[:, :, D_FF + jj * tn:D_FF + (jj + 1) * tn]

        up_phase()

    @pl.when((j >= 1) & (j < nj))
    def _():
        gate_down_phase()
        up_phase()

    @pl.when(j == nj)
    def _():
        gate_down_phase()
        o_ref[...] = x_ref[...] + gt_ref[...] * (_rms(o_ref[...]) * gpost_ref[...])


def _ffn(x, mod, conv_in, g_pre2, g_post2, w_up, conv_w, conv_b, w_down, tm, nb):
    B, S, _ = x.shape
    R = mod.shape[1]
    tpb = S // tm
    assert S % tm == 0 and (R == 1 or tpb == 1) and (nb == 1 or tpb == 1)
    rm = R if R == 1 else tm
    tn = FFN_TN
    nj = D_FF // tn
    row = lambda i: (i // tpb, i % tpb)
    t_up = lambda j: jnp.minimum(j, nj - 1)
    t_gate = lambda j: jnp.maximum(j - 1, 0)
    conv_b2 = conv_b.reshape(1, 2 * D_FF)
    conv_out = jax.ShapeDtypeStruct((B * tpb * nb, CONV_W - 1, D_FF), F32)
    conv_out_spec = pl.BlockSpec((nb, CONV_W - 1, tn), lambda i, j: (i, 0, t_gate(j)))
    up_rows = nb * (tm // nb + HALO)
    emit_w = not isinstance(w_up, tuple)
    assert not emit_w or (B * tpb == 1 and w_up.dtype == F32 and w_down.dtype == F32)
    w_v, w_g = (w_up, w_up) if emit_w else w_up
    gate_off = nj if emit_w else 0
    wv_spec = pl.BlockSpec((D_MODEL, tn), lambda i, j: (0, t_up(j)))
    wd_spec = pl.BlockSpec((tn, D_MODEL), lambda i, j: (t_gate(j), 0))
    half_shape = jax.ShapeDtypeStruct((D_MODEL, D_FF), BF16)
    outs = pl.pallas_call(
        functools.partial(_ffn_kernel, nb=nb, tpb=tpb, nj=nj),
        out_shape=(jax.ShapeDtypeStruct((B, S, D_MODEL), F32), conv_out, conv_out)
        + ((half_shape, half_shape, jax.ShapeDtypeStruct(w_down.shape, BF16)) if emit_w else ()),
        grid=(B * tpb, nj + 1),
        in_specs=[pl.BlockSpec((None, tm, D_MODEL), lambda i, j: (*row(i), 0)),
                  pl.BlockSpec((None, rm, D_MODEL), lambda i, j: (i // tpb, 0, 3)),
                  pl.BlockSpec((None, rm, D_MODEL), lambda i, j: (i // tpb, 0, 4)),
                  pl.BlockSpec((None, rm, D_MODEL), lambda i, j: (i // tpb, 0, 5)),
                  pl.BlockSpec((1, D_MODEL), lambda i, j: (0, 0)),
                  pl.BlockSpec((1, D_MODEL), lambda i, j: (0, 0)),
                  wv_spec,
                  pl.BlockSpec((D_MODEL, tn), lambda i, j: (0, gate_off + t_up(j))),
                  pl.BlockSpec((CONV_W, tn), lambda i, j: (0, t_gate(j))),
                  pl.BlockSpec((CONV_W, tn), lambda i, j: (0, nj + t_gate(j))),
                  pl.BlockSpec((1, tn), lambda i, j: (0, t_gate(j))),
                  pl.BlockSpec((1, tn), lambda i, j: (0, nj + t_gate(j))),
                  wd_spec,
                  pl.BlockSpec((nb, CONV_W - 1, 2 * D_FF), lambda i, j: (i // tpb, 0, 0))],
        out_specs=(pl.BlockSpec((None, tm, D_MODEL), lambda i, j: (*row(i), 0)),
                   conv_out_spec, conv_out_spec) + ((wv_spec, wv_spec, wd_spec) if emit_w else ()),
        scratch_shapes=[pltpu.VMEM((tm, D_MODEL), BF16),
                        pltpu.VMEM((up_rows, 2 * tn), F32),
                        pltpu.VMEM((tm, tn), BF16),
                        pltpu.VMEM((nj, nb, CONV_W - 1, 2 * tn), F32)],
        compiler_params=_params(("arbitrary", "arbitrary")),
        name="conv_ffn",
    )(x, mod, mod, mod, g_pre2.reshape(1, D_MODEL), g_post2.reshape(1, D_MODEL), w_v, w_g,
      conv_w, conv_w, conv_b2, conv_b2, w_down, conv_in)
    y, co_v, co_g = outs[:3]
    weights_b = ((outs[3], outs[4]), outs[5]) if emit_w else (w_up, w_down)
    return y, jnp.concatenate([co_v, co_g], axis=-1), weights_b


def _rotary_tables(pos):
    half = RET_DK // 2
    inv = ROPE_BASE ** (-jnp.arange(half, dtype=F32) / half)
    ang = pos.astype(F32)[:, None] * inv[None, :]
    cos, sin = jnp.cos(ang), jnp.sin(ang)
    return jnp.concatenate([cos, cos], axis=-1), jnp.concatenate([-sin, sin], axis=-1)


def kernel(x_prompt, x_sample, cache_att_k, cache_att_v, state_ret, state_conv, c_prompt, c_sample,
           w_ada, b_ada, g_pre1, w_in, rel_bias, w_br_ret, w_br_att, w_out, g_post1, g_pre2,
           w_up, conv_w, conv_b, w_down, g_post2):
    depth = w_ada.shape[0]
    Bp, Sp, _ = x_prompt.shape
    Bs, Ls, _ = x_sample.shape
    log_gamma = jnp.log(1.0 - 2.0 ** (-5.0 - jnp.arange(N_HEADS, dtype=F32)))
    cos_p, sin_p = _rotary_tables(jnp.arange(Sp))
    cos_s, sin_s = _rotary_tables(PAST_LEN + jnp.arange(Ls))
    cos_s, sin_s = jnp.tile(cos_s, (Bs, 1)), jnp.tile(sin_s, (Bs, 1))
    keep = min(ATT_REACH, Sp)
    tm_p, tm_wide = 512, 1024

    yp = x_prompt
    ys = x_sample.reshape(1, Bs * Ls, D_MODEL)
    outs = [[] for _ in range(8)]
    for l in range(depth):
        w_br_ret_b, w_br_att_b, w_out_b = (w_br_ret[l].astype(BF16), w_br_att[l].astype(BF16),
                                           w_out[l].astype(BF16))
        mod = _mod(jnp.concatenate([c_prompt, c_sample], axis=0), w_ada[l], b_ada[l])
        mod_p = mod[:Bp].reshape(Bp, 1, 6 * D_MODEL)
        mod_s = jnp.repeat(mod[Bp:], Ls, axis=0).reshape(1, Bs * Ls, 6 * D_MODEL)

        proj_s, w_in_b = _in_proj(ys, mod_s, g_pre1[l], w_in[l], cos_s, sin_s, tm=Bs * Ls)
        proj_s3 = proj_s.reshape(Bs, Ls, IN_WIDTH)
        g_ret_s, r_s = _retention(proj_s3, state_ret[l], log_gamma, Ls, Ls)
        o_att_s = _attention_sample(proj_s3, cache_att_k, cache_att_v, l, rel_bias[l])
        x1_s = _merge(ys, g_ret_s.reshape(1, Bs * Ls, -1), o_att_s.reshape(1, Bs * Ls, -1), proj_s,
                      mod_s, g_post1[l], w_br_ret_b, w_br_att_b, w_out_b, tm=Bs * Ls)
        ys, cv_s, (w_up_b, w_down_b) = _ffn(x1_s, mod_s, state_conv[l], g_pre2[l], g_post2[l], w_up[l],
                                            conv_w[l], conv_b[l], w_down[l], tm=Bs * Ls, nb=Bs)
        k_s = proj_s3[:, :, OFF_AK:OFF_AV].astype(F32).reshape(Bs, Ls, N_HEADS, ATT_DH)
        v_s = proj_s3[:, :, OFF_AV:OFF_GR].astype(F32).reshape(Bs, Ls, N_HEADS, ATT_DH)

        proj = _in_proj(yp, mod_p, g_pre1[l], w_in_b, cos_p, sin_p, tm=tm_wide)
        zero_state = jnp.zeros((Bp, N_HEADS, RET_DK, RET_DV), F32)
        g_ret, r_p = _retention(proj, zero_state, log_gamma, RET_TILE, CHUNK)
        o_att = _attention_prompt(proj, rel_bias[l])
        x1 = _merge(yp, g_ret, o_att, proj, mod_p, g_post1[l], w_br_ret_b, w_br_att_b, w_out_b, tm=tm_p)
        conv0 = jnp.zeros((Bp, CONV_W - 1, 2 * D_FF), F32)
        yp, cv_p, _ = _ffn(x1, mod_p, conv0, g_pre2[l], g_post2[l], w_up_b, conv_w[l], conv_b[l],
                           w_down_b, tm=tm_wide, nb=1)
        cv_p = cv_p.reshape(Bp, Sp // tm_wide, CONV_W - 1, 2 * D_FF)[:, -1]
        k_p = proj[:, Sp - keep:, OFF_AK:OFF_AV].astype(F32).reshape(Bp, keep, N_HEADS, ATT_DH)
        v_p = proj[:, Sp - keep:, OFF_AV:OFF_GR].astype(F32).reshape(Bp, keep, N_HEADS, ATT_DH)

        for lst, val in zip(outs, (k_p, v_p, r_p, cv_p, k_s, v_s, r_s, cv_s)):
            lst.append(val)
    return (yp, ys.reshape(Bs, Ls, D_MODEL), *[jnp.stack(o) for o in outs])
```

```python
import functools
import math

import jax
import jax.numpy as jnp
from jax import lax
from jax.experimental import pallas as pl
from jax.experimental.pallas import tpu as pltpu

F32 = jnp.float32
BF16 = jnp.bfloat16

D_MODEL = 2048
PAST_LEN = 4096
CHUNK = 64
N_HEADS = 8
RET_DK = 128
RET_DV = 256
ATT_DH = 128
ATT_LEFT_CHUNKS = 8
ATT_REACH = ATT_LEFT_CHUNKS * CHUNK
REL_CLIP = 128
D_FF = 5632
CONV_W = 3
ROPE_BASE = 10000.0
EPS = 1e-6
IN_WIDTH = 13312
NEG = -1e30
LOG2E = math.log2(math.e)

OFF_RQ, OFF_RK, OFF_RV, OFF_RG = 0, 1024, 2048, 4096
OFF_AQ, OFF_AK, OFF_AV, OFF_GR, OFF_GA = 6144, 7168, 8192, 9216, 11264
QK_W = N_HEADS * RET_DK
RV_W = N_HEADS * RET_DV

LANE = 128
MXU_N = 256
VMEM_LIMIT = 62 * 1024 * 1024

PROJ_TN = 1024
RET_TILE = 256
ATT_TILE = 256
FFN_TN = 512
FFN_SUB = 128


def _params(sem):
    return pltpu.CompilerParams(dimension_semantics=sem, vmem_limit_bytes=VMEM_LIMIT)


def _rms(x):
    return x * lax.rsqrt(jnp.mean(x * x, axis=-1, keepdims=True) + EPS)


def _sigmoid(x):
    return 1.0 / (1.0 + jnp.exp2(x * (-LOG2E)))


def _gelu_tanh(x):
    k = -2.0 * LOG2E * math.sqrt(2.0 / math.pi)
    return x / (1.0 + jnp.exp2(x * (k + (k * 0.044715) * (x * x))))


def _const_spec(shape):
    zeros = (0,) * len(shape)
    return pl.BlockSpec(shape, lambda *_: zeros, pipeline_mode=pl.Buffered(1))


def _mod_kernel(c_ref, w_ref, b_ref, o_ref):
    a = jax.nn.silu(c_ref[...]).astype(BF16)
    o_ref[...] = jnp.dot(a, w_ref[...].astype(BF16), preferred_element_type=F32) + b_ref[...]


def _mod(c, w_ada, b_ada):
    n, tn = c.shape[0], 1024
    width = w_ada.shape[1]
    return pl.pallas_call(
        _mod_kernel,
        out_shape=jax.ShapeDtypeStruct((n, width), F32),
        grid=(width // tn,),
        in_specs=[pl.BlockSpec((n, D_MODEL), lambda j: (0, 0)),
                  pl.BlockSpec((D_MODEL, tn), lambda j: (0, j)),
                  pl.BlockSpec((1, tn), lambda j: (0, j))],
        out_specs=pl.BlockSpec((n, tn), lambda j: (0, j)),
        compiler_params=_params(("arbitrary",)),
        name="adaln_mod",
    )(c, w_ada, b_ada.reshape(1, width))


def _inproj_kernel(x_ref, sh_ref, sc_ref, g_ref, w_ref, cos_ref, sin_ref, o_ref, *rest):
    u_ref = rest[-1]
    j = pl.program_id(1)

    @pl.when(j == 0)
    def _():
        u = (_rms(x_ref[...]) * g_ref[...]) * (1.0 + sc_ref[...]) + sh_ref[...]
        u_ref[...] = u.astype(BF16)

    def tile(epilogue):
        w = 2 * MXU_N
        for c in range(PROJ_TN // w):
            cols = slice(c * w, (c + 1) * w)
            wc = w_ref[:, cols]
            if len(rest) == 2:
                wc = wc.astype(BF16)
                rest[0][:, cols] = wc
            acc = jnp.dot(u_ref[...], wc, preferred_element_type=F32)
            o_ref[:, cols] = epilogue(acc).astype(BF16)

    def rotary(acc):
        scale = jnp.where(j == OFF_RK // PROJ_TN, RET_DK ** -0.5, 1.0).astype(F32)
        cos, sin = cos_ref[...], sin_ref[...]
        heads = [acc[:, h * LANE:(h + 1) * LANE] for h in range(acc.shape[1] // LANE)]
        return jnp.concatenate(
            [(a * cos + pltpu.roll(a, LANE // 2, 1) * sin) * scale for a in heads], axis=-1)

    pl.when(j < OFF_RV // PROJ_TN)(lambda: tile(rotary))
    pl.when((j >= OFF_RG // PROJ_TN) & (j < OFF_AQ // PROJ_TN))(lambda: tile(lambda a: a * _sigmoid(a)))
    pl.when(j >= OFF_GR // PROJ_TN)(lambda: tile(_sigmoid))
    pl.when(((j >= OFF_RV // PROJ_TN) & (j < OFF_RG // PROJ_TN))
            | ((j >= OFF_AQ // PROJ_TN) & (j < OFF_GR // PROJ_TN)))(lambda: tile(lambda a: a))


def _in_proj(x, mod, g_pre1, w_in, cos, sin, tm):
    B, S, _ = x.shape
    R = mod.shape[1]
    tpb = S // tm
    assert S % tm == 0 and (R == 1 or tpb == 1)
    rm = R if R == 1 else tm
    emit_w = w_in.dtype == F32
    assert not emit_w or B * tpb == 1
    w_spec = pl.BlockSpec((D_MODEL, PROJ_TN), lambda i, j: (0, j))
    proj_shape = jax.ShapeDtypeStruct((B, S, IN_WIDTH), BF16)
    proj_spec = pl.BlockSpec((None, tm, PROJ_TN), lambda i, j: (i // tpb, i % tpb, j))
    return pl.pallas_call(
        _inproj_kernel,
        out_shape=(proj_shape, jax.ShapeDtypeStruct(w_in.shape, BF16)) if emit_w else proj_shape,
        grid=(B * tpb, IN_WIDTH // PROJ_TN),
        in_specs=[
            pl.BlockSpec((None, tm, D_MODEL), lambda i, j: (i // tpb, i % tpb, 0)),
            pl.BlockSpec((None, rm, D_MODEL), lambda i, j: (i // tpb, 0, 0)),
            pl.BlockSpec((None, rm, D_MODEL), lambda i, j: (i // tpb, 0, 1)),
            pl.BlockSpec((1, D_MODEL), lambda i, j: (0, 0)),
            w_spec,
            pl.BlockSpec((tm, LANE), lambda i, j: (i % tpb, 0)),
            pl.BlockSpec((tm, LANE), lambda i, j: (i % tpb, 0)),
        ],
        out_specs=(proj_spec, w_spec) if emit_w else proj_spec,
        scratch_shapes=[pltpu.VMEM((tm, D_MODEL), BF16)],
        compiler_params=_params(("arbitrary", "arbitrary")),
        name="in_proj",
    )(x, mod, mod, g_pre1.reshape(1, D_MODEL), w_in, cos, sin)


def _retention_kernel(ds_ref, q_ref, k_ref, v_ref, rg_ref, dmat_ref, dq_ref, dk_ref, s0_ref,
                      o_ref, snew_ref, s_ref):
    t = pl.program_id(1)

    @pl.when(t == 0)
    def _():
        s_ref[...] = s0_ref[...]

    def dots(h):
        q = q_ref[:, h * RET_DK:(h + 1) * RET_DK]
        k = k_ref[:, h * RET_DK:(h + 1) * RET_DK]
        v = v_ref[:, h * RET_DV:(h + 1) * RET_DV]
        state = s_ref[h]
        s = lax.dot_general(q, k, (((1,), (1,)), ((), ())), preferred_element_type=F32)
        qs = jnp.dot(q, state.astype(BF16), preferred_element_type=F32)
        kd = (k.astype(F32) * dk_ref[h]).astype(BF16)
        kv = lax.dot_general(kd, v, (((0,), (0,)), ((), ())), preferred_element_type=F32)
        return s, qs, kv, state, v

    def finish(h, s, qs, kv, state, v):
        o = (jnp.dot((s * dmat_ref[h]).astype(BF16), v, preferred_element_type=F32)
             + qs * dq_ref[h])
        new_state = ds_ref[h] * state + kv
        s_ref[h] = new_state
        cols = slice(h * RET_DV, (h + 1) * RET_DV)
        o_ref[:, cols] = (rg_ref[:, cols].astype(F32) * _rms(o)).astype(BF16)

    pending = dots(0)
    for h in range(1, N_HEADS):
        nxt = dots(h)
        finish(h - 1, *pending)
        pending = nxt
    finish(N_HEADS - 1, *pending)

    @pl.when(t == pl.num_programs(1) - 1)
    def _():
        snew_ref[...] = s_ref[...]


def _retention(proj, state0, log_gamma, tile, chunk):
    B, S, _ = proj.shape
    nt = S // tile
    pos = jnp.arange(tile)
    ch = pos // chunk
    dist = (pos[:, None] - pos[None, :]).astype(F32)
    lg = log_gamma[:, None, None]
    same = ch[:, None] == ch[None, :]
    past = ch[None, :] < ch[:, None]
    dmat = jnp.where(same[None], jnp.exp(lg * jnp.abs(dist)[None]),
                     jnp.where(past[None], jnp.exp(lg * dist[None]), 0.0))
    tf = pos.astype(F32)
    dq = jnp.broadcast_to(jnp.exp((tf[None, :] + 1.0) * log_gamma[:, None])[:, :, None],
                          (N_HEADS, tile, RET_DV))
    dk = jnp.broadcast_to(jnp.exp((tile - 1.0 - tf)[None, :] * log_gamma[:, None])[:, :, None],
                          (N_HEADS, tile, RET_DK))
    ds = jnp.exp(tile * log_gamma)
    state_spec = pl.BlockSpec((None, N_HEADS, RET_DK, RET_DV), lambda b, t: (b, 0, 0, 0))
    return pl.pallas_call(
        _retention_kernel,
        out_shape=(jax.ShapeDtypeStruct((B, S, RV_W), BF16),
                   jax.ShapeDtypeStruct((B, N_HEADS, RET_DK, RET_DV), F32)),
        grid=(B, nt),
        in_specs=[
            pl.BlockSpec(memory_space=pltpu.SMEM),
            pl.BlockSpec((None, tile, QK_W), lambda b, t: (b, t, OFF_RQ // QK_W)),
            pl.BlockSpec((None, tile, QK_W), lambda b, t: (b, t, OFF_RK // QK_W)),
            pl.BlockSpec((None, tile, RV_W), lambda b, t: (b, t, OFF_RV // RV_W)),
            pl.BlockSpec((None, tile, RV_W), lambda b, t: (b, t, OFF_RG // RV_W)),
            _const_spec(dmat.shape), _const_spec(dq.shape), _const_spec(dk.shape),
            state_spec,
        ],
        out_specs=(pl.BlockSpec((None, tile, RV_W), lambda b, t: (b, t, 0)), state_spec),
        scratch_shapes=[pltpu.VMEM((N_HEADS, RET_DK, RET_DV), F32)],
        compiler_params=_params(("arbitrary", "arbitrary")),
        name="retention",
    )(ds, proj, proj, proj, proj, dmat, dq, dk, state0)


def _qk(q, k):
    s = lax.dot_general(q, k, (((1,), (1,)), ((), ())), preferred_element_type=F32)
    return s * ((ATT_DH ** -0.5) * LOG2E)


def _softmax_pv(scores, values):
    m = functools.reduce(jnp.maximum, [jnp.max(s, axis=-1, keepdims=True) for s in scores])
    ps = [jnp.exp2(s - m) for s in scores]
    l = functools.reduce(jnp.add, [jnp.sum(p, axis=-1, keepdims=True) for p in ps])
    o = functools.reduce(jnp.add, [jnp.dot(p.astype(BF16), v, preferred_element_type=F32)
                                   for p, v in zip(ps, values)])
    return o / l


def _per_head_pipelined(scores_fn, finish_fn):
    pending = scores_fn(0)
    for h in range(1, N_HEADS):
        nxt = scores_fn(h)
        finish_fn(h - 1, pending)
        pending = nxt
    finish_fn(N_HEADS - 1, pending)


def _bias_table_kernel(base_ref, o_ref):
    T, W = o_ref.shape
    n = base_ref.shape[-1]
    x = jnp.broadcast_to(base_ref[...], (T, n))
    x = pltpu.roll(x, n - T + 1, 1, stride=1, stride_axis=0)[:, :W]
    qc = lax.broadcasted_iota(jnp.int32, (T, W), 0) // CHUNK
    kc = lax.broadcasted_iota(jnp.int32, (T, W), 1) // CHUNK
    o_ref[...] = jnp.where((kc >= qc) & (kc <= qc + ATT_LEFT_CHUNKS), x * LOG2E, NEG)


def _bias_table(rel_bias):
    T = ATT_TILE
    W = (ATT_REACH // T + 1) * T
    n = 1024
    assert W + T - 1 <= n
    lo = (T - 1) + ATT_REACH - REL_CLIP
    hi = n - lo - (2 * REL_CLIP + 1)
    assert hi >= 0
    base = jnp.concatenate([jnp.broadcast_to(rel_bias[:, :1], (N_HEADS, lo)), rel_bias,
                            jnp.broadcast_to(rel_bias[:, -1:], (N_HEADS, hi))], axis=1)
    return pl.pallas_call(
        _bias_table_kernel,
        out_shape=jax.ShapeDtypeStruct((N_HEADS, T, W), F32),
        grid=(N_HEADS,),
        in_specs=[pl.BlockSpec((None, 1, n), lambda h: (h, 0, 0))],
        out_specs=pl.BlockSpec((None, T, W), lambda h: (h, 0, 0)),
        compiler_params=_params(("arbitrary",)),
        name="bias_table",
    )(base.reshape(N_HEADS, 1, n).astype(F32))


def _att_prompt_kernel(q_ref, k0_ref, k1_ref, k2_ref, v0_ref, v1_ref, v2_ref, tab_ref, o_ref):
    i = pl.program_id(1)
    T = ATT_TILE

    def scores(h):
        cols = slice(h * ATT_DH, (h + 1) * ATT_DH)
        q = q_ref[:, cols]
        s0 = _qk(q, k0_ref[:, cols]) + tab_ref[h, :, 0:T]
        s1 = _qk(q, k1_ref[:, cols]) + tab_ref[h, :, T:2 * T]
        s2 = _qk(q, k2_ref[:, cols]) + tab_ref[h, :, 2 * T:3 * T]
        return [jnp.where(i >= 2, s0, NEG), jnp.where(i >= 1, s1, NEG), s2]

    def finish(h, ss):
        cols = slice(h * ATT_DH, (h + 1) * ATT_DH)
        o = _softmax_pv(ss, [v0_ref[:, cols], v1_ref[:, cols], v2_ref[:, cols]])
        o_ref[:, cols] = o.astype(BF16)

    _per_head_pipelined(scores, finish)


def _attention_prompt(proj, rel_bias):
    B, S, _ = proj.shape
    T = ATT_TILE
    nq = S // T
    assert ATT_REACH // T + 1 == 3
    table = _bias_table(rel_bias)

    def blk(off, back):
        return pl.BlockSpec((None, T, QK_W), lambda b, i: (b, jnp.maximum(i - back, 0), off // QK_W))

    return pl.pallas_call(
        _att_prompt_kernel,
        out_shape=jax.ShapeDtypeStruct((B, S, QK_W), BF16),
        grid=(B, nq),
        in_specs=[blk(OFF_AQ, 0),
                  blk(OFF_AK, 2), blk(OFF_AK, 1), blk(OFF_AK, 0),
                  blk(OFF_AV, 2), blk(OFF_AV, 1), blk(OFF_AV, 0),
                  _const_spec(table.shape)],
        out_specs=pl.BlockSpec((None, T, QK_W), lambda b, i: (b, i, 0)),
        compiler_params=_params(("arbitrary", "arbitrary")),
        name="attention_prompt",
    )(proj, proj, proj, proj, proj, proj, proj, table)


def _att_sample_kernel(q_ref, kn_ref, vn_ref, kc_ref, vc_ref, tabc_ref, tabn_ref, o_ref):
    L = q_ref.shape[0]
    head = lambda h: slice(h * ATT_DH, (h + 1) * ATT_DH)
    rows = lambda h: slice(h * L, (h + 1) * L)
    q = jnp.concatenate([q_ref[:, head(h)] for h in range(N_HEADS)], axis=0)
    sc = _qk(q, kc_ref[...].astype(BF16)) + tabc_ref[...]
    sn = jnp.concatenate([_qk(q_ref[:, head(h)], kn_ref[:, head(h)]) + tabn_ref[h]
                          for h in range(N_HEADS)], axis=0)
    m = jnp.maximum(jnp.max(sc, axis=-1, keepdims=True), jnp.max(sn, axis=-1, keepdims=True))
    pc, pn = jnp.exp2(sc - m), jnp.exp2(sn - m)
    l = jnp.sum(pc, axis=-1, keepdims=True) + jnp.sum(pn, axis=-1, keepdims=True)
    pn = pn.astype(BF16)
    o = jnp.dot(pc.astype(BF16), vc_ref[...].astype(BF16), preferred_element_type=F32)
    o = o + jnp.concatenate([jnp.dot(pn[rows(h)], vn_ref[:, head(h)], preferred_element_type=F32)
                             for h in range(N_HEADS)], axis=0)
    o = (o / l).astype(BF16)
    for h in range(N_HEADS):
        o_ref[:, head(h)] = o[rows(h)]


def _attention_sample(proj, cache_k, cache_v, layer, rel_bias):
    B, L, _ = proj.shape
    P = cache_k.shape[2]
    assert P == ATT_REACH and PAST_LEN >= ATT_REACH
    t = jnp.arange(L)
    rel = jnp.concatenate([jnp.arange(P)[None, :] - P - t[:, None], t[None, :] - t[:, None]], axis=1)
    table = rel_bias[:, jnp.clip(rel, -REL_CLIP, REL_CLIP) + REL_CLIP].astype(F32) * LOG2E
    same_head = jnp.eye(N_HEADS, dtype=bool)[:, None, None, :]
    table_c = jnp.where(same_head, table[:, :, :P, None], NEG).reshape(N_HEADS * L, P * N_HEADS)
    table_n = table[:, :, P:]
    depth = cache_k.shape[0]
    cache_k = cache_k.reshape(depth, B, P * N_HEADS, ATT_DH)
    cache_v = cache_v.reshape(depth, B, P * N_HEADS, ATT_DH)
    cache_spec = pl.BlockSpec((None, None, P * N_HEADS, ATT_DH), lambda b: (layer, b, 0, 0))
    return pl.pallas_call(
        _att_sample_kernel,
        out_shape=jax.ShapeDtypeStruct((B, L, QK_W), BF16),
        grid=(B,),
        in_specs=[pl.BlockSpec((None, L, QK_W), lambda b: (b, 0, OFF_AQ // QK_W)),
                  pl.BlockSpec((None, L, QK_W), lambda b: (b, 0, OFF_AK // QK_W)),
                  pl.BlockSpec((None, L, QK_W), lambda b: (b, 0, OFF_AV // QK_W)),
                  cache_spec, cache_spec,
                  _const_spec(table_c.shape), _const_spec(table_n.shape)],
        out_specs=pl.BlockSpec((None, L, QK_W), lambda b: (b, 0, 0)),
        compiler_params=_params(("arbitrary",)),
        name="attention_sample",
    )(proj, proj, proj, cache_k, cache_v, table_c, table_n)


def _merge_kernel(gret_ref, oatt_ref, gr0_ref, gr1_ref, ga0_ref, ga1_ref, x_ref, gt_ref, g_ref,
                  wr_ref, wa_ref, wo_ref, o_ref, m_ref):
    y_ret = jnp.dot(gret_ref[...], wr_ref[...], preferred_element_type=F32)
    y_att = jnp.dot(oatt_ref[...], wa_ref[...], preferred_element_type=F32)
    half = D_MODEL // 2
    m_ref[:, :half] = (gr0_ref[...].astype(F32) * y_ret[:, :half]
                       + ga0_ref[...].astype(F32) * y_att[:, :half]).astype(BF16)
    m_ref[:, half:] = (gr1_ref[...].astype(F32) * y_ret[:, half:]
                       + ga1_ref[...].astype(F32) * y_att[:, half:]).astype(BF16)
    z = jnp.dot(m_ref[...], wo_ref[...], preferred_element_type=F32)
    o_ref[...] = x_ref[...] + gt_ref[...] * (_rms(z) * g_ref[...])


def _merge(x, g_ret, o_att, proj, mod, g_post1, w_br_ret, w_br_att, w_out, tm):
    B, S, _ = x.shape
    R = mod.shape[1]
    tpb = S // tm
    assert S % tm == 0 and (R == 1 or tpb == 1)
    rm = R if R == 1 else tm
    half = D_MODEL // 2
    row = lambda i: (i // tpb, i % tpb)

    def gate_spec(off):
        return pl.BlockSpec((None, tm, half), lambda i: (*row(i), off // half))

    return pl.pallas_call(
        _merge_kernel,
        out_shape=jax.ShapeDtypeStruct((B, S, D_MODEL), F32),
        grid=(B * tpb,),
        in_specs=[pl.BlockSpec((None, tm, RV_W), lambda i: (*row(i), 0)),
                  pl.BlockSpec((None, tm, QK_W), lambda i: (*row(i), 0)),
                  gate_spec(OFF_GR), gate_spec(OFF_GR + half),
                  gate_spec(OFF_GA), gate_spec(OFF_GA + half),
                  pl.BlockSpec((None, tm, D_MODEL), lambda i: (*row(i), 0)),
                  pl.BlockSpec((None, rm, D_MODEL), lambda i: (i // tpb, 0, 2)),
                  pl.BlockSpec((1, D_MODEL), lambda i: (0, 0)),
                  _const_spec(w_br_ret.shape), _const_spec(w_br_att.shape), _const_spec(w_out.shape)],
        out_specs=pl.BlockSpec((None, tm, D_MODEL), lambda i: (*row(i), 0)),
        scratch_shapes=[pltpu.VMEM((tm, D_MODEL), BF16)],
        compiler_params=_params(("arbitrary",)),
        name="merge_out",
    )(g_ret, o_att, proj, proj, proj, proj, x, mod, g_post1.reshape(1, D_MODEL),
      w_br_ret, w_br_att, w_out)


HALO = 8


def _ffn_kernel(x_ref, sh_ref, sc_ref, gt_ref, gpre_ref, gpost_ref, wv_ref, wg_ref, cwv_ref, cwg_ref,
                cbv_ref, cbg_ref, wd_ref, cin_ref, o_ref, cov_ref, cog_ref, *rest, nb, tpb, nj):
    u_ref, up_ref, h_ref, carry_ref = rest[-4:]
    wvb_ref, wgb_ref, wdb_ref = rest[:-4] if len(rest) > 4 else (None, None, None)
    i, j = pl.program_id(0), pl.program_id(1)
    tn = FFN_TN
    L = x_ref.shape[0] // nb
    base = [s * (L + HALO) + HALO for s in range(nb)]
    rb = min(L, 64)

    def bf16_weights(w_ref, cast_ref, idx):
        w = w_ref[idx]
        if cast_ref is not None:
            w = w.astype(BF16)
            cast_ref[idx] = w
        return w

    def up_phase():
        for half, (w_ref, cast_ref) in enumerate(((wv_ref, wvb_ref), (wg_ref, wgb_ref))):
            acc = jnp.dot(u_ref[...], bf16_weights(w_ref, cast_ref, (slice(None), slice(None))),
                          preferred_element_type=F32)
            for s in range(nb):
                up_ref[base[s]:base[s] + L, half * tn:(half + 1) * tn] = acc[s * L:(s + 1) * L]

    def gate_down_phase():
        src = up_ref
        for s in range(nb):
            src[base[s] - 2:base[s], :] = carry_ref[j - 1, s]

        def conv(r0, cols, cw_ref, cb_ref, wc):
            cur = src[r0:r0 + rb, cols]
            p1 = src[r0 - 1:r0 - 1 + rb, cols]
            p2 = src[r0 - 2:r0 - 2 + rb, cols]
            return (cb_ref[:, wc] + cw_ref[2:3, wc] * cur + cw_ref[1:2, wc] * p1 + cw_ref[0:1, wc] * p2)

        for p in range(tn // FFN_SUB):
            wc = slice(p * FFN_SUB, (p + 1) * FFN_SUB)
            cg = slice(tn + p * FFN_SUB, tn + (p + 1) * FFN_SUB)
            for s in range(nb):
                for r in range(0, L, rb):
                    value = conv(base[s] + r, wc, cwv_ref, cbv_ref, wc)
                    gate = conv(base[s] + r, cg, cwg_ref, cbg_ref, wc)
                    h_ref[s * L + r:s * L + r + rb, wc] = (_gelu_tanh(gate) * value).astype(BF16)

        for s in range(nb):
            last = src[base[s] + L - 2:base[s] + L, :]
            carry_ref[j - 1, s] = last
            cov_ref[s] = last[:, :tn]
            cog_ref[s] = last[:, tn:]

        w = 512
        for n in range(D_MODEL // w):
            wd = bf16_weights(wd_ref, wdb_ref, (slice(None), slice(n * w, (n + 1) * w)))
            o_ref[:, n * w:(n + 1) * w] += jnp.dot(h_ref[...], wd, preferred_element_type=F32)

    @pl.when(j == 0)
    def _():
        u = (_rms(x_ref[...]) * gpre_ref[...]) * (1.0 + sc_ref[...]) + sh_ref[...]
        u_ref[...] = u.astype(BF16)
        o_ref[...] = jnp.zeros(o_ref.shape, F32)

        @pl.when(i % tpb == 0)
        def _():
            for jj in range(nj):
                carry_ref[jj, :, :, :tn] = cin_ref[:, :, jj * tn:(jj + 1) * tn]
                carry_ref[jj, :, :, tn:] = cin_ref[:, :, D_FF + jj * tn:D_FF + (jj + 1) * tn]

        up_phase()

    @pl.when((j >= 1) & (j < nj))
    def _():
        gate_down_phase()
        up_phase()

    @pl.when(j == nj)
    def _():
        gate_down_phase()
        o_ref[...] = x_ref[...] + gt_ref[...] * (_rms(o_ref[...]) * gpost_ref[...])


def _ffn(x, mod, conv_in, g_pre2, g_post2, w_up, conv_w, conv_b, w_down, tm, nb):
    B, S, _ = x.shape
    R = mod.shape[1]
    tpb = S // tm
    assert S % tm == 0 and (R == 1 or tpb == 1) and (nb == 1 or tpb == 1)
    rm = R if R == 1 else tm
    tn = FFN_TN
    nj = D_FF // tn
    row = lambda i: (i // tpb, i % tpb)
    t_up = lambda j: jnp.minimum(j, nj - 1)
    t_gate = lambda j: jnp.maximum(j - 1, 0)
    conv_b2 = conv_b.reshape(1, 2 * D_FF)
    conv_out = jax.ShapeDtypeStruct((B * tpb * nb, CONV_W - 1, D_FF), F32)
    conv_out_spec = pl.BlockSpec((nb, CONV_W - 1, tn), lambda i, j: (i, 0, t_gate(j)))
    up_rows = nb * (tm // nb + HALO)
    emit_w = not isinstance(w_up, tuple)
    assert not emit_w or (B * tpb == 1 and w_up.dtype == F32 and w_down.dtype == F32)
    w_v, w_g = (w_up, w_up) if emit_w else w_up
    gate_off = nj if emit_w else 0
    wv_spec = pl.BlockSpec((D_MODEL, tn), lambda i, j: (0, t_up(j)))
    wd_spec = pl.BlockSpec((tn, D_MODEL), lambda i, j: (t_gate(j), 0))
    half_shape = jax.ShapeDtypeStruct((D_MODEL, D_FF), BF16)
    outs = pl.pallas_call(
        functools.partial(_ffn_kernel, nb=nb, tpb=tpb, nj=nj),
        out_shape=(jax.ShapeDtypeStruct((B, S, D_MODEL), F32), conv_out, conv_out)
        + ((half_shape, half_shape, jax.ShapeDtypeStruct(w_down.shape, BF16)) if emit_w else ()),
        grid=(B * tpb, nj + 1),
        in_specs=[pl.BlockSpec((None, tm, D_MODEL), lambda i, j: (*row(i), 0)),
                  pl.BlockSpec((None, rm, D_MODEL), lambda i, j: (i // tpb, 0, 3)),
                  pl.BlockSpec((None, rm, D_MODEL), lambda i, j: (i // tpb, 0, 4)),
                  pl.BlockSpec((None, rm, D_MODEL), lambda i, j: (i // tpb, 0, 5)),
                  pl.BlockSpec((1, D_MODEL), lambda i, j: (0, 0)),
                  pl.BlockSpec((1, D_MODEL), lambda i, j: (0, 0)),
                  wv_spec,
                  pl.BlockSpec((D_MODEL, tn), lambda i, j: (0, gate_off + t_up(j))),
                  pl.BlockSpec((CONV_W, tn), lambda i, j: (0, t_gate(j))),
                  pl.BlockSpec((CONV_W, tn), lambda i, j: (0, nj + t_gate(j))),
                  pl.BlockSpec((1, tn), lambda i, j: (0, t_gate(j))),
                  pl.BlockSpec((1, tn), lambda i, j: (0, nj + t_gate(j))),
                  wd_spec,
                  pl.BlockSpec((nb, CONV_W - 1, 2 * D_FF), lambda i, j: (i // tpb, 0, 0))],
        out_specs=(pl.BlockSpec((None, tm, D_MODEL), lambda i, j: (*row(i), 0)),
                   conv_out_spec, conv_out_spec) + ((wv_spec, wv_spec, wd_spec) if emit_w else ()),
        scratch_shapes=[pltpu.VMEM((tm, D_MODEL), BF16),
                        pltpu.VMEM((up_rows, 2 * tn), F32),
                        pltpu.VMEM((tm, tn), BF16),
                        pltpu.VMEM((nj, nb, CONV_W - 1, 2 * tn), F32)],
        compiler_params=_params(("arbitrary", "arbitrary")),
        name="conv_ffn",
    )(x, mod, mod, mod, g_pre2.reshape(1, D_MODEL), g_post2.reshape(1, D_MODEL), w_v, w_g,
      conv_w, conv_w, conv_b2, conv_b2, w_down, conv_in)
    y, co_v, co_g = outs[:3]
    weights_b = ((outs[3], outs[4]), outs[5]) if emit_w else (w_up, w_down)
    return y, jnp.concatenate([co_v, co_g], axis=-1), weights_b


def _rotary_tables(pos):
    half = RET_DK // 2
    inv = ROPE_BASE ** (-jnp.arange(half, dtype=F32) / half)
    ang = pos.astype(F32)[:, None] * inv[None, :]
    cos, sin = jnp.cos(ang), jnp.sin(ang)
    return jnp.concatenate([cos, cos], axis=-1), jnp.concatenate([-sin, sin], axis=-1)


def kernel(x_prompt, x_sample, cache_att_k, cache_att_v, state_ret, state_conv, c_prompt, c_sample,
           w_ada, b_ada, g_pre1, w_in, rel_bias, w_br_ret, w_br_att, w_out, g_post1, g_pre2,
           w_up, conv_w, conv_b, w_down, g_post2):
    depth = w_ada.shape[0]
    Bp, Sp, _ = x_prompt.shape
    Bs, Ls, _ = x_sample.shape
    log_gamma = jnp.log(1.0 - 2.0 ** (-5.0 - jnp.arange(N_HEADS, dtype=F32)))
    cos_p, sin_p = _rotary_tables(jnp.arange(Sp))
    cos_s, sin_s = _rotary_tables(PAST_LEN + jnp.arange(Ls))
    cos_s, sin_s = jnp.tile(cos_s, (Bs, 1)), jnp.tile(sin_s, (Bs, 1))
    keep = min(ATT_REACH, Sp)
    tm_p, tm_wide = 512, 1024

    yp = x_prompt
    ys = x_sample.reshape(1, Bs * Ls, D_MODEL)
    outs = [[] for _ in range(8)]
    for l in range(depth):
        w_br_ret_b, w_br_att_b, w_out_b = (w_br_ret[l].astype(BF16), w_br_att[l].astype(BF16),
                                           w_out[l].astype(BF16))
        mod = _mod(jnp.concatenate([c_prompt, c_sample], axis=0), w_ada[l], b_ada[l])
        mod_p = mod[:Bp].reshape(Bp, 1, 6 * D_MODEL)
        mod_s = jnp.repeat(mod[Bp:], Ls, axis=0).reshape(1, Bs * Ls, 6 * D_MODEL)

        proj_s, w_in_b = _in_proj(ys, mod_s, g_pre1[l], w_in[l], cos_s, sin_s, tm=Bs * Ls)
        proj_s3 = proj_s.reshape(Bs, Ls, IN_WIDTH)
        g_ret_s, r_s = _retention(proj_s3, state_ret[l], log_gamma, Ls, Ls)
        o_att_s = _attention_sample(proj_s3, cache_att_k, cache_att_v, l, rel_bias[l])
        x1_s = _merge(ys, g_ret_s.reshape(1, Bs * Ls, -1), o_att_s.reshape(1, Bs * Ls, -1), proj_s,
                      mod_s, g_post1[l], w_br_ret_b, w_br_att_b, w_out_b, tm=Bs * Ls)
        ys, cv_s, (w_up_b, w_down_b) = _ffn(x1_s, mod_s, state_conv[l], g_pre2[l], g_post2[l], w_up[l],
                                            conv_w[l], conv_b[l], w_down[l], tm=Bs * Ls, nb=Bs)
        k_s = proj_s3[:, :, OFF_AK:OFF_AV].astype(F32).reshape(Bs, Ls, N_HEADS, ATT_DH)
        v_s = proj_s3[:, :, OFF_AV:OFF_GR].astype(F32).reshape(Bs, Ls, N_HEADS, ATT_DH)

        proj = _in_proj(yp, mod_p, g_pre1[l], w_in_b, cos_p, sin_p, tm=tm_wide)
        zero_state = jnp.zeros((Bp, N_HEADS, RET_DK, RET_DV), F32)
        g_ret, r_p = _retention(proj, zero_state, log_gamma, RET_TILE, CHUNK)
        o_att = _attention_prompt(proj, rel_bias[l])
        x1 = _merge(yp, g_ret, o_att, proj, mod_p, g_post1[l], w_br_ret_b, w_br_att_b, w_out_b, tm=tm_p)
        conv0 = jnp.zeros((Bp, CONV_W - 1, 2 * D_FF), F32)
        yp, cv_p, _ = _ffn(x1, mod_p, conv0, g_pre2[l], g_post2[l], w_up_b, conv_w[l], conv_b[l],
                           w_down_b, tm=tm_wide, nb=1)
        cv_p = cv_p.reshape(Bp, Sp // tm_wide, CONV_W - 1, 2 * D_FF)[:, -1]
        k_p = proj[:, Sp - keep:, OFF_AK:OFF_AV].astype(F32).reshape(Bp, keep, N_HEADS, ATT_DH)
        v_p = proj[:, Sp - keep:, OFF_AV:OFF_GR].astype(F32).reshape(Bp, keep, N_HEADS, ATT_DH)

        for lst, val in zip(outs, (k_p, v_p, r_p, cv_p, k_s, v_s, r_s, cv_s)):
            lst.append(val)
    return (yp, ys.reshape(Bs, Ls, D_MODEL), *[jnp.stack(o) for o in outs])
```

```python
import functools
import math

import jax
import jax.numpy as jnp
from jax import lax
from jax.experimental import pallas as pl
from jax.experimental.pallas import tpu as pltpu

F32 = jnp.float32
BF16 = jnp.bfloat16

D_MODEL = 2048
PAST_LEN = 4096
CHUNK = 64
N_HEADS = 8
RET_DK = 128
RET_DV = 256
ATT_DH = 128
ATT_LEFT_CHUNKS = 8
ATT_REACH = ATT_LEFT_CHUNKS * CHUNK
REL_CLIP = 128
D_FF = 5632
CONV_W = 3
ROPE_BASE = 10000.0
EPS = 1e-6
IN_WIDTH = 13312
NEG = -1e30
LOG2E = math.log2(math.e)

OFF_RQ, OFF_RK, OFF_RV, OFF_RG = 0, 1024, 2048, 4096
OFF_AQ, OFF_AK, OFF_AV, OFF_GR, OFF_GA = 6144, 7168, 8192, 9216, 11264
QK_W = N_HEADS * RET_DK
RV_W = N_HEADS * RET_DV

LANE = 128
MXU_N = 256
VMEM_LIMIT = 62 * 1024 * 1024

PROJ_TN = 1024
RET_TILE = 256
ATT_TILE = 256
FFN_TN = 512
FFN_SUB = 128
ROW_CHUNK = 16


def _params(sem):
    return pltpu.CompilerParams(dimension_semantics=sem, vmem_limit_bytes=VMEM_LIMIT)


def _rms(x):
    return x * lax.rsqrt(jnp.mean(x * x, axis=-1, keepdims=True) + EPS)


def _rows(ref, r, n):
    return ref[...] if ref.shape[0] == 1 else ref[r:r + n, :]


def _norm_modulate(x_ref, g_ref, sc_ref, sh_ref, u_ref):
    for r in range(0, x_ref.shape[0], ROW_CHUNK):
        u = ((_rms(x_ref[r:r + ROW_CHUNK, :]) * g_ref[...]) * (1.0 + _rows(sc_ref, r, ROW_CHUNK))
             + _rows(sh_ref, r, ROW_CHUNK))
        u_ref[r:r + ROW_CHUNK, :] = u.astype(BF16)


def _sigmoid(x):
    return 1.0 / (1.0 + jnp.exp2(x * (-LOG2E)))


def _gelu_tanh(x):
    k = -2.0 * LOG2E * math.sqrt(2.0 / math.pi)
    return x / (1.0 + jnp.exp2(x * (k + (k * 0.044715) * (x * x))))


def _const_spec(shape):
    zeros = (0,) * len(shape)
    return pl.BlockSpec(shape, lambda *_: zeros, pipeline_mode=pl.Buffered(1))


def _mod_kernel(c_ref, w_ref, b_ref, o_ref):
    a = jax.nn.silu(c_ref[...]).astype(BF16)
    o_ref[...] = jnp.dot(a, w_ref[...].astype(BF16), preferred_element_type=F32) + b_ref[...]


def _mod(c, w_ada, b_ada):
    n, tn = c.shape[0], 1024
    width = w_ada.shape[1]
    return pl.pallas_call(
        _mod_kernel,
        out_shape=jax.ShapeDtypeStruct((n, width), F32),
        grid=(width // tn,),
        in_specs=[pl.BlockSpec((n, D_MODEL), lambda j: (0, 0)),
                  pl.BlockSpec((D_MODEL, tn), lambda j: (0, j)),
                  pl.BlockSpec((1, tn), lambda j: (0, j))],
        out_specs=pl.BlockSpec((n, tn), lambda j: (0, j)),
        compiler_params=_params(("arbitrary",)),
        name="adaln_mod",
    )(c, w_ada, b_ada.reshape(1, width))


def _inproj_kernel(x_ref, sh_ref, sc_ref, g_ref, w_ref, cos_ref, sin_ref, o_ref, *rest):
    u_ref = rest[-1]
    j = pl.program_id(1)

    @pl.when(j == 0)
    def _():
        _norm_modulate(x_ref, g_ref, sc_ref, sh_ref, u_ref)

    def tile(epilogue):
        w = 2 * MXU_N
        for c in range(PROJ_TN // w):
            cols = slice(c * w, (c + 1) * w)
            wc = w_ref[:, cols]
            if len(rest) == 2:
                wc = wc.astype(BF16)
                rest[0][:, cols] = wc
            acc = jnp.dot(u_ref[...], wc, preferred_element_type=F32)
            o_ref[:, cols] = epilogue(acc).astype(BF16)

    def rotary(acc):
        scale = jnp.where(j == OFF_RK // PROJ_TN, RET_DK ** -0.5, 1.0).astype(F32)
        cos, sin = cos_ref[...], sin_ref[...]
        heads = [acc[:, h * LANE:(h + 1) * LANE] for h in range(acc.shape[1] // LANE)]
        return jnp.concatenate(
            [(a * cos + pltpu.roll(a, LANE // 2, 1) * sin) * scale for a in heads], axis=-1)

    pl.when(j < OFF_RV // PROJ_TN)(lambda: tile(rotary))
    pl.when((j >= OFF_RG // PROJ_TN) & (j < OFF_AQ // PROJ_TN))(lambda: tile(lambda a: a * _sigmoid(a)))
    pl.when(j >= OFF_GR // PROJ_TN)(lambda: tile(_sigmoid))
    pl.when(((j >= OFF_RV // PROJ_TN) & (j < OFF_RG // PROJ_TN))
            | ((j >= OFF_AQ // PROJ_TN) & (j < OFF_GR // PROJ_TN)))(lambda: tile(lambda a: a))


def _in_proj(x, mod, g_pre1, w_in, cos, sin, tm):
    B, S, _ = x.shape
    R = mod.shape[1]
    tpb = S // tm
    assert S % tm == 0 and (R == 1 or tpb == 1)
    rm = R if R == 1 else tm
    emit_w = w_in.dtype == F32
    assert not emit_w or B * tpb == 1
    w_spec = pl.BlockSpec((D_MODEL, PROJ_TN), lambda i, j: (0, j))
    proj_shape = jax.ShapeDtypeStruct((B, S, IN_WIDTH), BF16)
    proj_spec = pl.BlockSpec((None, tm, PROJ_TN), lambda i, j: (i // tpb, i % tpb, j))
    return pl.pallas_call(
        _inproj_kernel,
        out_shape=(proj_shape, jax.ShapeDtypeStruct(w_in.shape, BF16)) if emit_w else proj_shape,
        grid=(B * tpb, IN_WIDTH // PROJ_TN),
        in_specs=[
            pl.BlockSpec((None, tm, D_MODEL), lambda i, j: (i // tpb, i % tpb, 0)),
            pl.BlockSpec((None, rm, D_MODEL), lambda i, j: (i // tpb, 0, 0)),
            pl.BlockSpec((None, rm, D_MODEL), lambda i, j: (i // tpb, 0, 1)),
            pl.BlockSpec((1, D_MODEL), lambda i, j: (0, 0)),
            w_spec,
            pl.BlockSpec((tm, LANE), lambda i, j: (i % tpb, 0)),
            pl.BlockSpec((tm, LANE), lambda i, j: (i % tpb, 0)),
        ],
        out_specs=(proj_spec, w_spec) if emit_w else proj_spec,
        scratch_shapes=[pltpu.VMEM((tm, D_MODEL), BF16)],
        compiler_params=_params(("arbitrary", "arbitrary")),
        name="in_proj",
    )(x, mod, mod, g_pre1.reshape(1, D_MODEL), w_in, cos, sin)


def _retention_kernel(ds_ref, q_ref, k_ref, v_ref, rg_ref, dmat_ref, dq_ref, dk_ref, s0_ref,
                      o_ref, snew_ref, s_ref):
    t = pl.program_id(1)

    @pl.when(t == 0)
    def _():
        s_ref[...] = s0_ref[...]

    def dots(h):
        q = q_ref[:, h * RET_DK:(h + 1) * RET_DK]
        k = k_ref[:, h * RET_DK:(h + 1) * RET_DK]
        v = v_ref[:, h * RET_DV:(h + 1) * RET_DV]
        state = s_ref[h]
        s = lax.dot_general(q, k, (((1,), (1,)), ((), ())), preferred_element_type=F32)
        qs = jnp.dot(q, state.astype(BF16), preferred_element_type=F32)
        kd = (k.astype(F32) * dk_ref[h]).astype(BF16)
        kv = lax.dot_general(kd, v, (((0,), (0,)), ((), ())), preferred_element_type=F32)
        return s, qs, kv, state, v

    def finish(h, s, qs, kv, state, v):
        o = (jnp.dot((s * dmat_ref[h]).astype(BF16), v, preferred_element_type=F32)
             + qs * dq_ref[h])
        new_state = ds_ref[h] * state + kv
        s_ref[h] = new_state
        cols = slice(h * RET_DV, (h + 1) * RET_DV)
        o_ref[:, cols] = (rg_ref[:, cols].astype(F32) * _rms(o)).astype(BF16)

    pending = dots(0)
    for h in range(1, N_HEADS):
        nxt = dots(h)
        finish(h - 1, *pending)
        pending = nxt
    finish(N_HEADS - 1, *pending)

    @pl.when(t == pl.num_programs(1) - 1)
    def _():
        snew_ref[...] = s_ref[...]


def _retention(proj, state0, log_gamma, tile, chunk):
    B, S, _ = proj.shape
    nt = S // tile
    pos = jnp.arange(tile)
    ch = pos // chunk
    dist = (pos[:, None] - pos[None, :]).astype(F32)
    lg = log_gamma[:, None, None]
    same = ch[:, None] == ch[None, :]
    past = ch[None, :] < ch[:, None]
    dmat = jnp.where(same[None], jnp.exp(lg * jnp.abs(dist)[None]),
                     jnp.where(past[None], jnp.exp(lg * dist[None]), 0.0))
    tf = pos.astype(F32)
    dq = jnp.broadcast_to(jnp.exp((tf[None, :] + 1.0) * log_gamma[:, None])[:, :, None],
                          (N_HEADS, tile, RET_DV))
    dk = jnp.broadcast_to(jnp.exp((tile - 1.0 - tf)[None, :] * log_gamma[:, None])[:, :, None],
                          (N_HEADS, tile, RET_DK))
    ds = jnp.exp(tile * log_gamma)
    state_spec = pl.BlockSpec((None, N_HEADS, RET_DK, RET_DV), lambda b, t: (b, 0, 0, 0))
    return pl.pallas_call(
        _retention_kernel,
        out_shape=(jax.ShapeDtypeStruct((B, S, RV_W), BF16),
                   jax.ShapeDtypeStruct((B, N_HEADS, RET_DK, RET_DV), F32)),
        grid=(B, nt),
        in_specs=[
            pl.BlockSpec(memory_space=pltpu.SMEM),
            pl.BlockSpec((None, tile, QK_W), lambda b, t: (b, t, OFF_RQ // QK_W)),
            pl.BlockSpec((None, tile, QK_W), lambda b, t: (b, t, OFF_RK // QK_W)),
            pl.BlockSpec((None, tile, RV_W), lambda b, t: (b, t, OFF_RV // RV_W)),
            pl.BlockSpec((None, tile, RV_W), lambda b, t: (b, t, OFF_RG // RV_W)),
            _const_spec(dmat.shape), _const_spec(dq.shape), _const_spec(dk.shape),
            state_spec,
        ],
        out_specs=(pl.BlockSpec((None, tile, RV_W), lambda b, t: (b, t, 0)), state_spec),
        scratch_shapes=[pltpu.VMEM((N_HEADS, RET_DK, RET_DV), F32)],
        compiler_params=_params(("arbitrary", "arbitrary")),
        name="retention",
    )(ds, proj, proj, proj, proj, dmat, dq, dk, state0)


def _qk(q, k):
    s = lax.dot_general(q, k, (((1,), (1,)), ((), ())), preferred_element_type=F32)
    return s * ((ATT_DH ** -0.5) * LOG2E)


def _softmax_pv(scores, values):
    m = functools.reduce(jnp.maximum, [jnp.max(s, axis=-1, keepdims=True) for s in scores])
    ps = [jnp.exp2(s - m) for s in scores]
    l = functools.reduce(jnp.add, [jnp.sum(p, axis=-1, keepdims=True) for p in ps])
    o = functools.reduce(jnp.add, [jnp.dot(p.astype(BF16), v, preferred_element_type=F32)
                                   for p, v in zip(ps, values)])
    return o / l


def _per_head_pipelined(scores_fn, finish_fn):
    pending = scores_fn(0)
    for h in range(1, N_HEADS):
        nxt = scores_fn(h)
        finish_fn(h - 1, pending)
        pending = nxt
    finish_fn(N_HEADS - 1, pending)


def _bias_table_kernel(base_ref, o_ref):
    T, W = o_ref.shape
    n = base_ref.shape[-1]
    x = jnp.broadcast_to(base_ref[...], (T, n))
    x = pltpu.roll(x, n - T + 1, 1, stride=1, stride_axis=0)[:, :W]
    qc = lax.broadcasted_iota(jnp.int32, (T, W), 0) // CHUNK
    kc = lax.broadcasted_iota(jnp.int32, (T, W), 1) // CHUNK
    o_ref[...] = jnp.where((kc >= qc) & (kc <= qc + ATT_LEFT_CHUNKS), x * LOG2E, NEG)


def _bias_table(rel_bias):
    T = ATT_TILE
    W = (ATT_REACH // T + 1) * T
    n = 1024
    assert W + T - 1 <= n
    lo = (T - 1) + ATT_REACH - REL_CLIP
    hi = n - lo - (2 * REL_CLIP + 1)
    assert hi >= 0
    base = jnp.concatenate([jnp.broadcast_to(rel_bias[:, :1], (N_HEADS, lo)), rel_bias,
                            jnp.broadcast_to(rel_bias[:, -1:], (N_HEADS, hi))], axis=1)
    return pl.pallas_call(
        _bias_table_kernel,
        out_shape=jax.ShapeDtypeStruct((N_HEADS, T, W), F32),
        grid=(N_HEADS,),
        in_specs=[pl.BlockSpec((None, 1, n), lambda h: (h, 0, 0))],
        out_specs=pl.BlockSpec((None, T, W), lambda h: (h, 0, 0)),
        compiler_params=_params(("arbitrary",)),
        name="bias_table",
    )(base.reshape(N_HEADS, 1, n).astype(F32))


def _att_prompt_kernel(q_ref, k0_ref, k1_ref, k2_ref, v0_ref, v1_ref, v2_ref, tab_ref, o_ref):
    i = pl.program_id(1)
    T = ATT_TILE

    def scores(h):
        cols = slice(h * ATT_DH, (h + 1) * ATT_DH)
        q = q_ref[:, cols]
        s0 = _qk(q, k0_ref[:, cols]) + tab_ref[h, :, 0:T]
        s1 = _qk(q, k1_ref[:, cols]) + tab_ref[h, :, T:2 * T]
        s2 = _qk(q, k2_ref[:, cols]) + tab_ref[h, :, 2 * T:3 * T]
        return [jnp.where(i >= 2, s0, NEG), jnp.where(i >= 1, s1, NEG), s2]

    def finish(h, ss):
        cols = slice(h * ATT_DH, (h + 1) * ATT_DH)
        o = _softmax_pv(ss, [v0_ref[:, cols], v1_ref[:, cols], v2_ref[:, cols]])
        o_ref[:, cols] = o.astype(BF16)

    _per_head_pipelined(scores, finish)


def _attention_prompt(proj, rel_bias):
    B, S, _ = proj.shape
    T = ATT_TILE
    nq = S // T
    assert ATT_REACH // T + 1 == 3
    table = _bias_table(rel_bias)

    def blk(off, back):
        return pl.BlockSpec((None, T, QK_W), lambda b, i: (b, jnp.maximum(i - back, 0), off // QK_W))

    return pl.pallas_call(
        _att_prompt_kernel,
        out_shape=jax.ShapeDtypeStruct((B, S, QK_W), BF16),
        grid=(B, nq),
        in_specs=[blk(OFF_AQ, 0),
                  blk(OFF_AK, 2), blk(OFF_AK, 1), blk(OFF_AK, 0),
                  blk(OFF_AV, 2), blk(OFF_AV, 1), blk(OFF_AV, 0),
                  _const_spec(table.shape)],
        out_specs=pl.BlockSpec((None, T, QK_W), lambda b, i: (b, i, 0)),
        compiler_params=_params(("arbitrary", "arbitrary")),
        name="attention_prompt",
    )(proj, proj, proj, proj, proj, proj, proj, table)


def _att_sample_kernel(q_ref, kn_ref, vn_ref, kc_ref, vc_ref, tabc_ref, tabn_ref, o_ref):
    L = q_ref.shape[0]
    head = lambda h: slice(h * ATT_DH, (h + 1) * ATT_DH)
    rows = lambda h: slice(h * L, (h + 1) * L)
    q = jnp.concatenate([q_ref[:, head(h)] for h in range(N_HEADS)], axis=0)
    sc = _qk(q, kc_ref[...].astype(BF16)) + tabc_ref[...]
    sn = jnp.concatenate([_qk(q_ref[:, head(h)], kn_ref[:, head(h)]) + tabn_ref[h]
                          for h in range(N_HEADS)], axis=0)
    m = jnp.maximum(jnp.max(sc, axis=-1, keepdims=True), jnp.max(sn, axis=-1, keepdims=True))
    pc, pn = jnp.exp2(sc - m), jnp.exp2(sn - m)
    l = jnp.sum(pc, axis=-1, keepdims=True) + jnp.sum(pn, axis=-1, keepdims=True)
    pn = pn.astype(BF16)
    o = jnp.dot(pc.astype(BF16), vc_ref[...].astype(BF16), preferred_element_type=F32)
    o = o + jnp.concatenate([jnp.dot(pn[rows(h)], vn_ref[:, head(h)], preferred_element_type=F32)
                             for h in range(N_HEADS)], axis=0)
    o = (o / l).astype(BF16)
    for h in range(N_HEADS):
        o_ref[:, head(h)] = o[rows(h)]


def _attention_sample(proj, cache_k, cache_v, layer, rel_bias):
    B, L, _ = proj.shape
    P = cache_k.shape[2]
    assert P == ATT_REACH and PAST_LEN >= ATT_REACH
    t = jnp.arange(L)
    rel = jnp.concatenate([jnp.arange(P)[None, :] - P - t[:, None], t[None, :] - t[:, None]], axis=1)
    table = rel_bias[:, jnp.clip(rel, -REL_CLIP, REL_CLIP) + REL_CLIP].astype(F32) * LOG2E
    same_head = jnp.eye(N_HEADS, dtype=bool)[:, None, None, :]
    table_c = jnp.where(same_head, table[:, :, :P, None], NEG).reshape(N_HEADS * L, P * N_HEADS)
    table_n = table[:, :, P:]
    depth = cache_k.shape[0]
    cache_k = cache_k.reshape(depth, B, P * N_HEADS, ATT_DH)
    cache_v = cache_v.reshape(depth, B, P * N_HEADS, ATT_DH)
    cache_spec = pl.BlockSpec((None, None, P * N_HEADS, ATT_DH), lambda b: (layer, b, 0, 0))
    return pl.pallas_call(
        _att_sample_kernel,
        out_shape=jax.ShapeDtypeStruct((B, L, QK_W), BF16),
        grid=(B,),
        in_specs=[pl.BlockSpec((None, L, QK_W), lambda b: (b, 0, OFF_AQ // QK_W)),
                  pl.BlockSpec((None, L, QK_W), lambda b: (b, 0, OFF_AK // QK_W)),
                  pl.BlockSpec((None, L, QK_W), lambda b: (b, 0, OFF_AV // QK_W)),
                  cache_spec, cache_spec,
                  _const_spec(table_c.shape), _const_spec(table_n.shape)],
        out_specs=pl.BlockSpec((None, L, QK_W), lambda b: (b, 0, 0)),
        compiler_params=_params(("arbitrary",)),
        name="attention_sample",
    )(proj, proj, proj, cache_k, cache_v, table_c, table_n)


def _merge_kernel(gret_ref, oatt_ref, gr0_ref, gr1_ref, ga0_ref, ga1_ref, x_ref, gt_ref, g_ref,
                  wr_ref, wa_ref, wo_ref, o_ref, m_ref):
    y_ret = jnp.dot(gret_ref[...], wr_ref[...], preferred_element_type=F32)
    y_att = jnp.dot(oatt_ref[...], wa_ref[...], preferred_element_type=F32)
    half = D_MODEL // 2
    m_ref[:, :half] = (gr0_ref[...].astype(F32) * y_ret[:, :half]
                       + ga0_ref[...].astype(F32) * y_att[:, :half]).astype(BF16)
    m_ref[:, half:] = (gr1_ref[...].astype(F32) * y_ret[:, half:]
                       + ga1_ref[...].astype(F32) * y_att[:, half:]).astype(BF16)
    z = jnp.dot(m_ref[...], wo_ref[...], preferred_element_type=F32)
    o_ref[...] = x_ref[...] + gt_ref[...] * (_rms(z) * g_ref[...])


def _merge(x, g_ret, o_att, proj, mod, g_post1, w_br_ret, w_br_att, w_out, tm):
    B, S, _ = x.shape
    R = mod.shape[1]
    tpb = S // tm
    assert S % tm == 0 and (R == 1 or tpb == 1)
    rm = R if R == 1 else tm
    half = D_MODEL // 2
    row = lambda i: (i // tpb, i % tpb)

    def gate_spec(off):
        return pl.BlockSpec((None, tm, half), lambda i: (*row(i), off // half))

    return pl.pallas_call(
        _merge_kernel,
        out_shape=jax.ShapeDtypeStruct((B, S, D_MODEL), F32),
        grid=(B * tpb,),
        in_specs=[pl.BlockSpec((None, tm, RV_W), lambda i: (*row(i), 0)),
                  pl.BlockSpec((None, tm, QK_W), lambda i: (*row(i), 0)),
                  gate_spec(OFF_GR), gate_spec(OFF_GR + half),
                  gate_spec(OFF_GA), gate_spec(OFF_GA + half),
                  pl.BlockSpec((None, tm, D_MODEL), lambda i: (*row(i), 0)),
                  pl.BlockSpec((None, rm, D_MODEL), lambda i: (i // tpb, 0, 2)),
                  pl.BlockSpec((1, D_MODEL), lambda i: (0, 0)),
                  _const_spec(w_br_ret.shape), _const_spec(w_br_att.shape), _const_spec(w_out.shape)],
        out_specs=pl.BlockSpec((None, tm, D_MODEL), lambda i: (*row(i), 0)),
        scratch_shapes=[pltpu.VMEM((tm, D_MODEL), BF16)],
        compiler_params=_params(("arbitrary",)),
        name="merge_out",
    )(g_ret, o_att, proj, proj, proj, proj, x, mod, g_post1.reshape(1, D_MODEL),
      w_br_ret, w_br_att, w_out)


HALO = 8


def _ffn_kernel(x_ref, sh_ref, sc_ref, gt_ref, gpre_ref, gpost_ref, wv_ref, wg_ref, cwv_ref, cwg_ref,
                cbv_ref, cbg_ref, wd_ref, cin_ref, o_ref, cov_ref, cog_ref, *rest, nb, tpb, nj):
    u_ref, up_ref, h_ref, carry_ref = rest[-4:]
    wvb_ref, wgb_ref, wdb_ref = rest[:-4] if len(rest) > 4 else (None, None, None)
    i, j = pl.program_id(0), pl.program_id(1)
    tn = FFN_TN
    L = x_ref.shape[0] // nb
    base = [s * (L + HALO) + HALO for s in range(nb)]
    rb = min(L, 64)

    def bf16_weights(w_ref, cast_ref, idx):
        w = w_ref[idx]
        if cast_ref is not None:
            w = w.astype(BF16)
            cast_ref[idx] = w
        return w

    def up_phase():
        for half, (w_ref, cast_ref) in enumerate(((wv_ref, wvb_ref), (wg_ref, wgb_ref))):
            acc = jnp.dot(u_ref[...], bf16_weights(w_ref, cast_ref, (slice(None), slice(None))),
                          preferred_element_type=F32)
            for s in range(nb):
                up_ref[base[s]:base[s] + L, half * tn:(half + 1) * tn] = acc[s * L:(s + 1) * L]

    def gate_down_phase():
        src = up_ref
        for s in range(nb):
            src[base[s] - 2:base[s], :] = carry_ref[j - 1, s]

        def conv(r0, cols, cw_ref, cb_ref, wc):
            cur = src[r0:r0 + rb, cols]
            p1 = src[r0 - 1:r0 - 1 + rb, cols]
            p2 = src[r0 - 2:r0 - 2 + rb, cols]
            return (cb_ref[:, wc] + cw_ref[2:3, wc] * cur + cw_ref[1:2, wc] * p1 + cw_ref[0:1, wc] * p2)

        for p in range(tn // FFN_SUB):
            wc = slice(p * FFN_SUB, (p + 1) * FFN_SUB)
            cg = slice(tn + p * FFN_SUB, tn + (p + 1) * FFN_SUB)
            for s in range(nb):
                for r in range(0, L, rb):
                    value = conv(base[s] + r, wc, cwv_ref, cbv_ref, wc)
                    gate = conv(base[s] + r, cg, cwg_ref, cbg_ref, wc)
                    h_ref[s * L + r:s * L + r + rb, wc] = (_gelu_tanh(gate) * value).astype(BF16)

        for s in range(nb):
            last = src[base[s] + L - 2:base[s] + L, :]
            carry_ref[j - 1, s] = last
            cov_ref[s] = last[:, :tn]
            cog_ref[s] = last[:, tn:]

        w = 512
        for n in range(D_MODEL // w):
            wd = bf16_weights(wd_ref, wdb_ref, (slice(None), slice(n * w, (n + 1) * w)))
            o_ref[:, n * w:(n + 1) * w] += jnp.dot(h_ref[...], wd, preferred_element_type=F32)

    @pl.when(j == 0)
    def _():
        _norm_modulate(x_ref, gpre_ref, sc_ref, sh_ref, u_ref)
        o_ref[...] = jnp.zeros(o_ref.shape, F32)

        @pl.when(i % tpb == 0)
        def _():
            for jj in range(nj):
                carry_ref[jj, :, :, :tn] = cin_ref[:, :, jj * tn:(jj + 1) * tn]
                carry_ref[jj, :, :, tn:] = cin_ref[:, :, D_FF + jj * tn:D_FF + (jj + 1) * tn]

        up_phase()

    @pl.when((j >= 1) & (j < nj))
    def _():
        gate_down_phase()
        up_phase()

    @pl.when(j == nj)
    def _():
        gate_down_phase()
        o_ref[...] = x_ref[...] + gt_ref[...] * (_rms(o_ref[...]) * gpost_ref[...])


def _ffn(x, mod, conv_in, g_pre2, g_post2, w_up, conv_w, conv_b, w_down, tm, nb):
    B, S, _ = x.shape
    R = mod.shape[1]
    tpb = S // tm
    assert S % tm == 0 and (R == 1 or tpb == 1) and (nb == 1 or tpb == 1)
    rm = R if R == 1 else tm
    tn = FFN_TN
    nj = D_FF // tn
    row = lambda i: (i // tpb, i % tpb)
    t_up = lambda j: jnp.minimum(j, nj - 1)
    t_gate = lambda j: jnp.maximum(j - 1, 0)
    conv_b2 = conv_b.reshape(1, 2 * D_FF)
    conv_out = jax.ShapeDtypeStruct((B * tpb * nb, CONV_W - 1, D_FF), F32)
    conv_out_spec = pl.BlockSpec((nb, CONV_W - 1, tn), lambda i, j: (i, 0, t_gate(j)))
    up_rows = nb * (tm // nb + HALO)
    emit_w = not isinstance(w_up, tuple)
    assert not emit_w or (B * tpb == 1 and w_up.dtype == F32 and w_down.dtype == F32)
    w_v, w_g = (w_up, w_up) if emit_w else w_up
    gate_off = nj if emit_w else 0
    wv_spec = pl.BlockSpec((D_MODEL, tn), lambda i, j: (0, t_up(j)))
    wd_spec = pl.BlockSpec((tn, D_MODEL), lambda i, j: (t_gate(j), 0))
    half_shape = jax.ShapeDtypeStruct((D_MODEL, D_FF), BF16)
    outs = pl.pallas_call(
        functools.partial(_ffn_kernel, nb=nb, tpb=tpb, nj=nj),
        out_shape=(jax.ShapeDtypeStruct((B, S, D_MODEL), F32), conv_out, conv_out)
        + ((half_shape, half_shape, jax.ShapeDtypeStruct(w_down.shape, BF16)) if emit_w else ()),
        grid=(B * tpb, nj + 1),
        in_specs=[pl.BlockSpec((None, tm, D_MODEL), lambda i, j: (*row(i), 0)),
                  pl.BlockSpec((None, rm, D_MODEL), lambda i, j: (i // tpb, 0, 3)),
                  pl.BlockSpec((None, rm, D_MODEL), lambda i, j: (i // tpb, 0, 4)),
                  pl.BlockSpec((None, rm, D_MODEL), lambda i, j: (i // tpb, 0, 5)),
                  pl.BlockSpec((1, D_MODEL), lambda i, j: (0, 0)),
                  pl.BlockSpec((1, D_MODEL), lambda i, j: (0, 0)),
                  wv_spec,
                  pl.BlockSpec((D_MODEL, tn), lambda i, j: (0, gate_off + t_up(j))),
                  pl.BlockSpec((CONV_W, tn), lambda i, j: (0, t_gate(j))),
                  pl.BlockSpec((CONV_W, tn), lambda i, j: (0, nj + t_gate(j))),
                  pl.BlockSpec((1, tn), lambda i, j: (0, t_gate(j))),
                  pl.BlockSpec((1, tn), lambda i, j: (0, nj + t_gate(j))),
                  wd_spec,
                  pl.BlockSpec((nb, CONV_W - 1, 2 * D_FF), lambda i, j: (i // tpb, 0, 0))],
        out_specs=(pl.BlockSpec((None, tm, D_MODEL), lambda i, j: (*row(i), 0)),
                   conv_out_spec, conv_out_spec) + ((wv_spec, wv_spec, wd_spec) if emit_w else ()),
        scratch_shapes=[pltpu.VMEM((tm, D_MODEL), BF16),
                        pltpu.VMEM((up_rows, 2 * tn), F32),
                        pltpu.VMEM((tm, tn), BF16),
                        pltpu.VMEM((nj, nb, CONV_W - 1, 2 * tn), F32)],
        compiler_params=_params(("arbitrary", "arbitrary")),
        name="conv_ffn",
    )(x, mod, mod, mod, g_pre2.reshape(1, D_MODEL), g_post2.reshape(1, D_MODEL), w_v, w_g,
      conv_w, conv_w, conv_b2, conv_b2, w_down, conv_in)
    y, co_v, co_g = outs[:3]
    weights_b = ((outs[3], outs[4]), outs[5]) if emit_w else (w_up, w_down)
    return y, jnp.concatenate([co_v, co_g], axis=-1), weights_b


def _rotary_tables(pos):
    half = RET_DK // 2
    inv = ROPE_BASE ** (-jnp.arange(half, dtype=F32) / half)
    ang = pos.astype(F32)[:, None] * inv[None, :]
    cos, sin = jnp.cos(ang), jnp.sin(ang)
    return jnp.concatenate([cos, cos], axis=-1), jnp.concatenate([-sin, sin], axis=-1)


def kernel(x_prompt, x_sample, cache_att_k, cache_att_v, state_ret, state_conv, c_prompt, c_sample,
           w_ada, b_ada, g_pre1, w_in, rel_bias, w_br_ret, w_br_att, w_out, g_post1, g_pre2,
           w_up, conv_w, conv_b, w_down, g_post2):
    depth = w_ada.shape[0]
    Bp, Sp, _ = x_prompt.shape
    Bs, Ls, _ = x_sample.shape
    log_gamma = jnp.log(1.0 - 2.0 ** (-5.0 - jnp.arange(N_HEADS, dtype=F32)))
    cos_p, sin_p = _rotary_tables(jnp.arange(Sp))
    cos_s, sin_s = _rotary_tables(PAST_LEN + jnp.arange(Ls))
    cos_s, sin_s = jnp.tile(cos_s, (Bs, 1)), jnp.tile(sin_s, (Bs, 1))
    keep = min(ATT_REACH, Sp)
    tm_p, tm_wide = 512, 1024

    yp = x_prompt
    ys = x_sample.reshape(1, Bs * Ls, D_MODEL)
    outs = [[] for _ in range(8)]
    for l in range(depth):
        w_br_ret_b, w_br_att_b, w_out_b = (w_br_ret[l].astype(BF16), w_br_att[l].astype(BF16),
                                           w_out[l].astype(BF16))
        mod = _mod(jnp.concatenate([c_prompt, c_sample], axis=0), w_ada[l], b_ada[l])
        mod_p = mod[:Bp].reshape(Bp, 1, 6 * D_MODEL)
        mod_s = jnp.repeat(mod[Bp:], Ls, axis=0).reshape(1, Bs * Ls, 6 * D_MODEL)

        proj_s, w_in_b = _in_proj(ys, mod_s, g_pre1[l], w_in[l], cos_s, sin_s, tm=Bs * Ls)
        proj_s3 = proj_s.reshape(Bs, Ls, IN_WIDTH)
        g_ret_s, r_s = _retention(proj_s3, state_ret[l], log_gamma, Ls, Ls)
        o_att_s = _attention_sample(proj_s3, cache_att_k, cache_att_v, l, rel_bias[l])
        x1_s = _merge(ys, g_ret_s.reshape(1, Bs * Ls, -1), o_att_s.reshape(1, Bs * Ls, -1), proj_s,
                      mod_s, g_post1[l], w_br_ret_b, w_br_att_b, w_out_b, tm=Bs * Ls)
        ys, cv_s, (w_up_b, w_down_b) = _ffn(x1_s, mod_s, state_conv[l], g_pre2[l], g_post2[l], w_up[l],
                                            conv_w[l], conv_b[l], w_down[l], tm=Bs * Ls, nb=Bs)
        k_s = proj_s3[:, :, OFF_AK:OFF_AV].astype(F32).reshape(Bs, Ls, N_HEADS, ATT_DH)
        v_s = proj_s3[:, :, OFF_AV:OFF_GR].astype(F32).reshape(Bs, Ls, N_HEADS, ATT_DH)

        proj = _in_proj(yp, mod_p, g_pre1[l], w_in_b, cos_p, sin_p, tm=tm_wide)
        zero_state = jnp.zeros((Bp, N_HEADS, RET_DK, RET_DV), F32)
        g_ret, r_p = _retention(proj, zero_state, log_gamma, RET_TILE, CHUNK)
        o_att = _attention_prompt(proj, rel_bias[l])
        x1 = _merge(yp, g_ret, o_att, proj, mod_p, g_post1[l], w_br_ret_b, w_br_att_b, w_out_b, tm=tm_p)
        conv0 = jnp.zeros((Bp, CONV_W - 1, 2 * D_FF), F32)
        yp, cv_p, _ = _ffn(x1, mod_p, conv0, g_pre2[l], g_post2[l], w_up_b, conv_w[l], conv_b[l],
                           w_down_b, tm=tm_wide, nb=1)
        cv_p = cv_p.reshape(Bp, Sp // tm_wide, CONV_W - 1, 2 * D_FF)[:, -1]
        k_p = proj[:, Sp - keep:, OFF_AK:OFF_AV].astype(F32).reshape(Bp, keep, N_HEADS, ATT_DH)
        v_p = proj[:, Sp - keep:, OFF_AV:OFF_GR].astype(F32).reshape(Bp, keep, N_HEADS, ATT_DH)

        for lst, val in zip(outs, (k_p, v_p, r_p, cv_p, k_s, v_s, r_s, cv_s)):
            lst.append(val)
    return (yp, ys.reshape(Bs, Ls, D_MODEL), *[jnp.stack(o) for o in outs])
```

```python
import functools
import math

import jax
import jax.numpy as jnp
from jax import lax
from jax.experimental import pallas as pl
from jax.experimental.pallas import tpu as pltpu

F32 = jnp.float32
BF16 = jnp.bfloat16

D_MODEL = 2048
PAST_LEN = 4096
CHUNK = 64
N_HEADS = 8
RET_DK = 128
RET_DV = 256
ATT_DH = 128
ATT_LEFT_CHUNKS = 8
ATT_REACH = ATT_LEFT_CHUNKS * CHUNK
REL_CLIP = 128
D_FF = 5632
CONV_W = 3
ROPE_BASE = 10000.0
EPS = 1e-6
IN_WIDTH = 13312
NEG = -1e30
LOG2E = math.log2(math.e)

OFF_RQ, OFF_RK, OFF_RV, OFF_RG = 0, 1024, 2048, 4096
OFF_AQ, OFF_AK, OFF_AV, OFF_GR, OFF_GA = 6144, 7168, 8192, 9216, 11264
QK_W = N_HEADS * RET_DK
RV_W = N_HEADS * RET_DV

LANE = 128
MXU_N = 256
VMEM_LIMIT = 62 * 1024 * 1024

PROJ_TN = 1024
RET_TILE = 256
ATT_TILE = 256
FFN_TN = 512
FFN_SUB = 128
ROW_CHUNK = 16


def _params(sem):
    return pltpu.CompilerParams(dimension_semantics=sem, vmem_limit_bytes=VMEM_LIMIT)


def _rms(x):
    return x * lax.rsqrt(jnp.mean(x * x, axis=-1, keepdims=True) + EPS)


def _rows(ref, r, n):
    return ref[...] if ref.shape[0] == 1 else ref[r:r + n, :]


def _norm_modulate(x_ref, g_ref, sc_ref, sh_ref, u_ref):
    for r in range(0, x_ref.shape[0], ROW_CHUNK):
        u = ((_rms(x_ref[r:r + ROW_CHUNK, :]) * g_ref[...]) * (1.0 + _rows(sc_ref, r, ROW_CHUNK))
             + _rows(sh_ref, r, ROW_CHUNK))
        u_ref[r:r + ROW_CHUNK, :] = u.astype(BF16)


def _sigmoid(x):
    return 1.0 / (1.0 + jnp.exp2(x * (-LOG2E)))


def _gelu_tanh(x):
    k = -2.0 * LOG2E * math.sqrt(2.0 / math.pi)
    return x / (1.0 + jnp.exp2(x * (k + (k * 0.044715) * (x * x))))


def _const_spec(shape):
    zeros = (0,) * len(shape)
    return pl.BlockSpec(shape, lambda *_: zeros, pipeline_mode=pl.Buffered(1))


def _mod_kernel(c_ref, w_ref, b_ref, o_ref):
    a = jax.nn.silu(c_ref[...]).astype(BF16)
    o_ref[...] = jnp.dot(a, w_ref[...].astype(BF16), preferred_element_type=F32) + b_ref[...]


def _mod(c, w_ada, b_ada):
    n, tn = c.shape[0], 1024
    width = w_ada.shape[1]
    return pl.pallas_call(
        _mod_kernel,
        out_shape=jax.ShapeDtypeStruct((n, width), F32),
        grid=(width // tn,),
        in_specs=[pl.BlockSpec((n, D_MODEL), lambda j: (0, 0)),
                  pl.BlockSpec((D_MODEL, tn), lambda j: (0, j)),
                  pl.BlockSpec((1, tn), lambda j: (0, j))],
        out_specs=pl.BlockSpec((n, tn), lambda j: (0, j)),
        compiler_params=_params(("arbitrary",)),
        name="adaln_mod",
    )(c, w_ada, b_ada.reshape(1, width))


def _inproj_kernel(x_ref, sh_ref, sc_ref, g_ref, w_ref, cos_ref, sin_ref, o_ref, *rest):
    u_ref = rest[-1]
    j = pl.program_id(1)

    @pl.when(j == 0)
    def _():
        _norm_modulate(x_ref, g_ref, sc_ref, sh_ref, u_ref)

    def tile(epilogue):
        w = 2 * MXU_N
        for c in range(PROJ_TN // w):
            cols = slice(c * w, (c + 1) * w)
            wc = w_ref[:, cols]
            if len(rest) == 2:
                wc = wc.astype(BF16)
                rest[0][:, cols] = wc
            acc = jnp.dot(u_ref[...], wc, preferred_element_type=F32)
            o_ref[:, cols] = epilogue(acc).astype(BF16)

    def rotary(acc):
        scale = jnp.where(j == OFF_RK // PROJ_TN, RET_DK ** -0.5, 1.0).astype(F32)
        cos, sin = cos_ref[...], sin_ref[...]
        heads = [acc[:, h * LANE:(h + 1) * LANE] for h in range(acc.shape[1] // LANE)]
        return jnp.concatenate(
            [(a * cos + pltpu.roll(a, LANE // 2, 1) * sin) * scale for a in heads], axis=-1)

    pl.when(j < OFF_RV // PROJ_TN)(lambda: tile(rotary))
    pl.when((j >= OFF_RG // PROJ_TN) & (j < OFF_AQ // PROJ_TN))(lambda: tile(lambda a: a * _sigmoid(a)))
    pl.when(j >= OFF_GR // PROJ_TN)(lambda: tile(_sigmoid))
    pl.when(((j >= OFF_RV // PROJ_TN) & (j < OFF_RG // PROJ_TN))
            | ((j >= OFF_AQ // PROJ_TN) & (j < OFF_GR // PROJ_TN)))(lambda: tile(lambda a: a))


def _in_proj(x, mod, g_pre1, w_in, cos, sin, tm):
    B, S, _ = x.shape
    R = mod.shape[1]
    tpb = S // tm
    assert S % tm == 0 and (R == 1 or tpb == 1)
    rm = R if R == 1 else tm
    emit_w = w_in.dtype == F32
    assert not emit_w or B * tpb == 1
    w_spec = pl.BlockSpec((D_MODEL, PROJ_TN), lambda i, j: (0, j))
    proj_shape = jax.ShapeDtypeStruct((B, S, IN_WIDTH), BF16)
    proj_spec = pl.BlockSpec((None, tm, PROJ_TN), lambda i, j: (i // tpb, i % tpb, j))
    return pl.pallas_call(
        _inproj_kernel,
        out_shape=(proj_shape, jax.ShapeDtypeStruct(w_in.shape, BF16)) if emit_w else proj_shape,
        grid=(B * tpb, IN_WIDTH // PROJ_TN),
        in_specs=[
            pl.BlockSpec((None, tm, D_MODEL), lambda i, j: (i // tpb, i % tpb, 0)),
            pl.BlockSpec((None, rm, D_MODEL), lambda i, j: (i // tpb, 0, 0)),
            pl.BlockSpec((None, rm, D_MODEL), lambda i, j: (i // tpb, 0, 1)),
            pl.BlockSpec((1, D_MODEL), lambda i, j: (0, 0)),
            w_spec,
            pl.BlockSpec((tm, LANE), lambda i, j: (i % tpb, 0)),
            pl.BlockSpec((tm, LANE), lambda i, j: (i % tpb, 0)),
        ],
        out_specs=(proj_spec, w_spec) if emit_w else proj_spec,
        scratch_shapes=[pltpu.VMEM((tm, D_MODEL), BF16)],
        compiler_params=_params(("arbitrary", "arbitrary")),
        name="in_proj",
    )(x, mod, mod, g_pre1.reshape(1, D_MODEL), w_in, cos, sin)


def _retention_kernel(ds_ref, q_ref, k_ref, v_ref, rg_ref, dmat_ref, dq_ref, dk_ref, s0_ref,
                      o_ref, snew_ref, s_ref):
    t = pl.program_id(1)

    @pl.when(t == 0)
    def _():
        s_ref[...] = s0_ref[...]

    def dots(h):
        q = q_ref[:, h * RET_DK:(h + 1) * RET_DK]
        k = k_ref[:, h * RET_DK:(h + 1) * RET_DK]
        v = v_ref[:, h * RET_DV:(h + 1) * RET_DV]
        state = s_ref[h]
        s = lax.dot_general(q, k, (((1,), (1,)), ((), ())), preferred_element_type=F32)
        qs = jnp.dot(q, state.astype(BF16), preferred_element_type=F32)
        kd = (k.astype(F32) * dk_ref[h]).astype(BF16)
        kv = lax.dot_general(kd, v, (((0,), (0,)), ((), ())), preferred_element_type=F32)
        return s, qs, kv, state, v

    def finish(h, s, qs, kv, state, v):
        o = (jnp.dot((s * dmat_ref[h]).astype(BF16), v, preferred_element_type=F32)
             + qs * dq_ref[h])
        new_state = ds_ref[h] * state + kv
        s_ref[h] = new_state
        cols = slice(h * RET_DV, (h + 1) * RET_DV)
        o_ref[:, cols] = (rg_ref[:, cols].astype(F32) * _rms(o)).astype(BF16)

    pending = dots(0)
    for h in range(1, N_HEADS):
        nxt = dots(h)
        finish(h - 1, *pending)
        pending = nxt
    finish(N_HEADS - 1, *pending)

    @pl.when(t == pl.num_programs(1) - 1)
    def _():
        snew_ref[...] = s_ref[...]


def _retention(proj, state0, log_gamma, tile, chunk):
    B, S, _ = proj.shape
    nt = S // tile
    pos = jnp.arange(tile)
    ch = pos // chunk
    dist = (pos[:, None] - pos[None, :]).astype(F32)
    lg = log_gamma[:, None, None]
    same = ch[:, None] == ch[None, :]
    past = ch[None, :] < ch[:, None]
    dmat = jnp.where(same[None], jnp.exp(lg * jnp.abs(dist)[None]),
                     jnp.where(past[None], jnp.exp(lg * dist[None]), 0.0))
    tf = pos.astype(F32)
    dq = jnp.broadcast_to(jnp.exp((tf[None, :] + 1.0) * log_gamma[:, None])[:, :, None],
                          (N_HEADS, tile, RET_DV))
    dk = jnp.broadcast_to(jnp.exp((tile - 1.0 - tf)[None, :] * log_gamma[:, None])[:, :, None],
                          (N_HEADS, tile, RET_DK))
    ds = jnp.exp(tile * log_gamma)
    state_spec = pl.BlockSpec((None, N_HEADS, RET_DK, RET_DV), lambda b, t: (b, 0, 0, 0))
    return pl.pallas_call(
        _retention_kernel,
        out_shape=(jax.ShapeDtypeStruct((B, S, RV_W), BF16),
                   jax.ShapeDtypeStruct((B, N_HEADS, RET_DK, RET_DV), F32)),
        grid=(B, nt),
        in_specs=[
            pl.BlockSpec(memory_space=pltpu.SMEM),
            pl.BlockSpec((None, tile, QK_W), lambda b, t: (b, t, OFF_RQ // QK_W)),
            pl.BlockSpec((None, tile, QK_W), lambda b, t: (b, t, OFF_RK // QK_W)),
            pl.BlockSpec((None, tile, RV_W), lambda b, t: (b, t, OFF_RV // RV_W)),
            pl.BlockSpec((None, tile, RV_W), lambda b, t: (b, t, OFF_RG // RV_W)),
            _const_spec(dmat.shape), _const_spec(dq.shape), _const_spec(dk.shape),
            state_spec,
        ],
        out_specs=(pl.BlockSpec((None, tile, RV_W), lambda b, t: (b, t, 0)), state_spec),
        scratch_shapes=[pltpu.VMEM((N_HEADS, RET_DK, RET_DV), F32)],
        compiler_params=_params(("arbitrary", "arbitrary")),
        name="retention",
    )(ds, proj, proj, proj, proj, dmat, dq, dk, state0)


def _qk(q, k):
    s = lax.dot_general(q, k, (((1,), (1,)), ((), ())), preferred_element_type=F32)
    return s * ((ATT_DH ** -0.5) * LOG2E)


def _softmax_pv(scores, values):
    m = functools.reduce(jnp.maximum, [jnp.max(s, axis=-1, keepdims=True) for s in scores])
    ps = [jnp.exp2(s - m) for s in scores]
    l = functools.reduce(jnp.add, [jnp.sum(p, axis=-1, keepdims=True) for p in ps])
    o = functools.reduce(jnp.add, [jnp.dot(p.astype(BF16), v, preferred_element_type=F32)
                                   for p, v in zip(ps, values)])
    return o / l


def _per_head_pipelined(scores_fn, finish_fn):
    pending = scores_fn(0)
    for h in range(1, N_HEADS):
        nxt = scores_fn(h)
        finish_fn(h - 1, pending)
        pending = nxt
    finish_fn(N_HEADS - 1, pending)


def _bias_table_kernel(base_ref, o_ref):
    T, W = o_ref.shape
    n = base_ref.shape[-1]
    x = jnp.broadcast_to(base_ref[...], (T, n))
    x = pltpu.roll(x, n - T + 1, 1, stride=1, stride_axis=0)[:, :W]
    qc = lax.broadcasted_iota(jnp.int32, (T, W), 0) // CHUNK
    kc = lax.broadcasted_iota(jnp.int32, (T, W), 1) // CHUNK
    o_ref[...] = jnp.where((kc >= qc) & (kc <= qc + ATT_LEFT_CHUNKS), x * LOG2E, NEG)


def _bias_table(rel_bias):
    T = ATT_TILE
    W = (ATT_REACH // T + 1) * T
    n = 1024
    assert W + T - 1 <= n
    lo = (T - 1) + ATT_REACH - REL_CLIP
    hi = n - lo - (2 * REL_CLIP + 1)
    assert hi >= 0
    base = jnp.concatenate([jnp.broadcast_to(rel_bias[:, :1], (N_HEADS, lo)), rel_bias,
                            jnp.broadcast_to(rel_bias[:, -1:], (N_HEADS, hi))], axis=1)
    return pl.pallas_call(
        _bias_table_kernel,
        out_shape=jax.ShapeDtypeStruct((N_HEADS, T, W), F32),
        grid=(N_HEADS,),
        in_specs=[pl.BlockSpec((None, 1, n), lambda h: (h, 0, 0))],
        out_specs=pl.BlockSpec((None, T, W), lambda h: (h, 0, 0)),
        compiler_params=_params(("arbitrary",)),
        name="bias_table",
    )(base.reshape(N_HEADS, 1, n).astype(F32))


def _att_prompt_kernel(q_ref, k0_ref, k1_ref, k2_ref, v0_ref, v1_ref, v2_ref, tab_ref, o_ref):
    i = pl.program_id(1)
    T = ATT_TILE

    def scores(h):
        cols = slice(h * ATT_DH, (h + 1) * ATT_DH)
        q = q_ref[:, cols]
        s0 = _qk(q, k0_ref[:, cols]) + tab_ref[h, :, 0:T]
        s1 = _qk(q, k1_ref[:, cols]) + tab_ref[h, :, T:2 * T]
        s2 = _qk(q, k2_ref[:, cols]) + tab_ref[h, :, 2 * T:3 * T]
        return [jnp.where(i >= 2, s0, NEG), jnp.where(i >= 1, s1, NEG), s2]

    def finish(h, ss):
        cols = slice(h * ATT_DH, (h + 1) * ATT_DH)
        o = _softmax_pv(ss, [v0_ref[:, cols], v1_ref[:, cols], v2_ref[:, cols]])
        o_ref[:, cols] = o.astype(BF16)

    _per_head_pipelined(scores, finish)


def _attention_prompt(proj, rel_bias):
    B, S, _ = proj.shape
    T = ATT_TILE
    nq = S // T
    assert ATT_REACH // T + 1 == 3
    table = _bias_table(rel_bias)

    def blk(off, back):
        return pl.BlockSpec((None, T, QK_W), lambda b, i: (b, jnp.maximum(i - back, 0), off // QK_W))

    return pl.pallas_call(
        _att_prompt_kernel,
        out_shape=jax.ShapeDtypeStruct((B, S, QK_W), BF16),
        grid=(B, nq),
        in_specs=[blk(OFF_AQ, 0),
                  blk(OFF_AK, 2), blk(OFF_AK, 1), blk(OFF_AK, 0),
                  blk(OFF_AV, 2), blk(OFF_AV, 1), blk(OFF_AV, 0),
                  _const_spec(table.shape)],
        out_specs=pl.BlockSpec((None, T, QK_W), lambda b, i: (b, i, 0)),
        compiler_params=_params(("arbitrary", "arbitrary")),
        name="attention_prompt",
    )(proj, proj, proj, proj, proj, proj, proj, table)


def _att_sample_kernel(q_ref, kn_ref, vn_ref, kc_ref, vc_ref, tabc_ref, tabn_ref, o_ref):
    L = q_ref.shape[0]
    head = lambda h: slice(h * ATT_DH, (h + 1) * ATT_DH)
    rows = lambda h: slice(h * L, (h + 1) * L)
    q = jnp.concatenate([q_ref[:, head(h)] for h in range(N_HEADS)], axis=0)
    sc = _qk(q, kc_ref[...].astype(BF16)) + tabc_ref[...]
    sn = jnp.concatenate([_qk(q_ref[:, head(h)], kn_ref[:, head(h)]) + tabn_ref[h]
                          for h in range(N_HEADS)], axis=0)
    m = jnp.maximum(jnp.max(sc, axis=-1, keepdims=True), jnp.max(sn, axis=-1, keepdims=True))
    pc, pn = jnp.exp2(sc - m), jnp.exp2(sn - m)
    l = jnp.sum(pc, axis=-1, keepdims=True) + jnp.sum(pn, axis=-1, keepdims=True)
    pn = pn.astype(BF16)
    o = jnp.dot(pc.astype(BF16), vc_ref[...].astype(BF16), preferred_element_type=F32)
    o = o + jnp.concatenate([jnp.dot(pn[rows(h)], vn_ref[:, head(h)], preferred_element_type=F32)
                             for h in range(N_HEADS)], axis=0)
    o = (o / l).astype(BF16)
    for h in range(N_HEADS):
        o_ref[:, head(h)] = o[rows(h)]


def _attention_sample(proj, cache_k, cache_v, layer, rel_bias):
    B, L, _ = proj.shape
    P = cache_k.shape[2]
    assert P == ATT_REACH and PAST_LEN >= ATT_REACH
    t = jnp.arange(L)
    rel = jnp.concatenate([jnp.arange(P)[None, :] - P - t[:, None], t[None, :] - t[:, None]], axis=1)
    table = rel_bias[:, jnp.clip(rel, -REL_CLIP, REL_CLIP) + REL_CLIP].astype(F32) * LOG2E
    same_head = jnp.eye(N_HEADS, dtype=bool)[:, None, None, :]
    table_c = jnp.where(same_head, table[:, :, :P, None], NEG).reshape(N_HEADS * L, P * N_HEADS)
    table_n = table[:, :, P:]
    depth = cache_k.shape[0]
    cache_k = cache_k.reshape(depth, B, P * N_HEADS, ATT_DH)
    cache_v = cache_v.reshape(depth, B, P * N_HEADS, ATT_DH)
    cache_spec = pl.BlockSpec((None, None, P * N_HEADS, ATT_DH), lambda b: (layer, b, 0, 0))
    return pl.pallas_call(
        _att_sample_kernel,
        out_shape=jax.ShapeDtypeStruct((B, L, QK_W), BF16),
        grid=(B,),
        in_specs=[pl.BlockSpec((None, L, QK_W), lambda b: (b, 0, OFF_AQ // QK_W)),
                  pl.BlockSpec((None, L, QK_W), lambda b: (b, 0, OFF_AK // QK_W)),
                  pl.BlockSpec((None, L, QK_W), lambda b: (b, 0, OFF_AV // QK_W)),
                  cache_spec, cache_spec,
                  _const_spec(table_c.shape), _const_spec(table_n.shape)],
        out_specs=pl.BlockSpec((None, L, QK_W), lambda b: (b, 0, 0)),
        compiler_params=_params(("arbitrary",)),
        name="attention_sample",
    )(proj, proj, proj, cache_k, cache_v, table_c, table_n)


def _merge_kernel(gret_ref, oatt_ref, gr0_ref, gr1_ref, ga0_ref, ga1_ref, x_ref, gt_ref, g_ref,
                  wr_ref, wa_ref, wo_ref, o_ref, m_ref):
    y_ret = jnp.dot(gret_ref[...], wr_ref[...], preferred_element_type=F32)
    y_att = jnp.dot(oatt_ref[...], wa_ref[...], preferred_element_type=F32)
    half = D_MODEL // 2
    m_ref[:, :half] = (gr0_ref[...].astype(F32) * y_ret[:, :half]
                       + ga0_ref[...].astype(F32) * y_att[:, :half]).astype(BF16)
    m_ref[:, half:] = (gr1_ref[...].astype(F32) * y_ret[:, half:]
                       + ga1_ref[...].astype(F32) * y_att[:, half:]).astype(BF16)
    z = jnp.dot(m_ref[...], wo_ref[...], preferred_element_type=F32)
    o_ref[...] = x_ref[...] + gt_ref[...] * (_rms(z) * g_ref[...])


def _merge(x, g_ret, o_att, proj, mod, g_post1, w_br_ret, w_br_att, w_out, tm):
    B, S, _ = x.shape
    R = mod.shape[1]
    tpb = S // tm
    assert S % tm == 0 and (R == 1 or tpb == 1)
    rm = R if R == 1 else tm
    half = D_MODEL // 2
    row = lambda i: (i // tpb, i % tpb)

    def gate_spec(off):
        return pl.BlockSpec((None, tm, half), lambda i: (*row(i), off // half))

    return pl.pallas_call(
        _merge_kernel,
        out_shape=jax.ShapeDtypeStruct((B, S, D_MODEL), F32),
        grid=(B * tpb,),
        in_specs=[pl.BlockSpec((None, tm, RV_W), lambda i: (*row(i), 0)),
                  pl.BlockSpec((None, tm, QK_W), lambda i: (*row(i), 0)),
                  gate_spec(OFF_GR), gate_spec(OFF_GR + half),
                  gate_spec(OFF_GA), gate_spec(OFF_GA + half),
                  pl.BlockSpec((None, tm, D_MODEL), lambda i: (*row(i), 0)),
                  pl.BlockSpec((None, rm, D_MODEL), lambda i: (i // tpb, 0, 2)),
                  pl.BlockSpec((1, D_MODEL), lambda i: (0, 0)),
                  _const_spec(w_br_ret.shape), _const_spec(w_br_att.shape), _const_spec(w_out.shape)],
        out_specs=pl.BlockSpec((None, tm, D_MODEL), lambda i: (*row(i), 0)),
        scratch_shapes=[pltpu.VMEM((tm, D_MODEL), BF16)],
        compiler_params=_params(("arbitrary",)),
        name="merge_out",
    )(g_ret, o_att, proj, proj, proj, proj, x, mod, g_post1.reshape(1, D_MODEL),
      w_br_ret, w_br_att, w_out)


HALO = 8


def _ffn_kernel(x_ref, sh_ref, sc_ref, gt_ref, gpre_ref, gpost_ref, wv_ref, wg_ref, cw_ref, cb_ref,
                wd_ref, cin_ref, o_ref, cout_ref, *rest, nb, tpb, nj):
    u_ref, up_ref, h_ref, carry_ref = rest[-4:]
    wvb_ref, wgb_ref, wdb_ref = rest[:-4] if len(rest) > 4 else (None, None, None)
    i, j = pl.program_id(0), pl.program_id(1)
    tn = FFN_TN
    L = x_ref.shape[0] // nb
    base = [s * (L + HALO) + HALO for s in range(nb)]
    rb = min(L, 64)

    def bf16_weights(w_ref, cast_ref, idx):
        w = w_ref[idx]
        if cast_ref is not None:
            w = w.astype(BF16)
            cast_ref[idx] = w
        return w

    def up_phase():
        for half, (w_ref, cast_ref) in enumerate(((wv_ref, wvb_ref), (wg_ref, wgb_ref))):
            acc = jnp.dot(u_ref[...], bf16_weights(w_ref, cast_ref, (slice(None), slice(None))),
                          preferred_element_type=F32)
            for s in range(nb):
                up_ref[base[s]:base[s] + L, half * tn:(half + 1) * tn] = acc[s * L:(s + 1) * L]

    def gate_down_phase():
        src = up_ref
        for s in range(nb):
            src[base[s] - 2:base[s], :] = carry_ref[j - 1, s]

        tile0 = pl.multiple_of((j - 1) * tn, tn)

        def conv(r0, cols, wcols):
            cur = src[r0:r0 + rb, cols]
            p1 = src[r0 - 1:r0 - 1 + rb, cols]
            p2 = src[r0 - 2:r0 - 2 + rb, cols]
            return (cb_ref[:, wcols] + cw_ref[2:3, wcols] * cur + cw_ref[1:2, wcols] * p1
                    + cw_ref[0:1, wcols] * p2)

        for p in range(tn // FFN_SUB):
            wc = slice(p * FFN_SUB, (p + 1) * FFN_SUB)
            cg = slice(tn + p * FFN_SUB, tn + (p + 1) * FFN_SUB)
            wv = pl.ds(tile0 + p * FFN_SUB, FFN_SUB)
            wg = pl.ds(tile0 + D_FF + p * FFN_SUB, FFN_SUB)
            for s in range(nb):
                for r in range(0, L, rb):
                    value = conv(base[s] + r, wc, wv)
                    gate = conv(base[s] + r, cg, wg)
                    h_ref[s * L + r:s * L + r + rb, wc] = (_gelu_tanh(gate) * value).astype(BF16)

        for s in range(nb):
            last = src[base[s] + L - 2:base[s] + L, :]
            carry_ref[j - 1, s] = last
            cout_ref[2 * s:2 * s + 2, pl.ds(tile0, tn)] = last[:, :tn]
            cout_ref[2 * s:2 * s + 2, pl.ds(tile0 + D_FF, tn)] = last[:, tn:]

        w = 512
        for n in range(D_MODEL // w):
            wd = bf16_weights(wd_ref, wdb_ref, (slice(None), slice(n * w, (n + 1) * w)))
            o_ref[:, n * w:(n + 1) * w] += jnp.dot(h_ref[...], wd, preferred_element_type=F32)

    @pl.when(j == 0)
    def _():
        _norm_modulate(x_ref, gpre_ref, sc_ref, sh_ref, u_ref)
        o_ref[...] = jnp.zeros(o_ref.shape, F32)

        @pl.when(i % tpb == 0)
        def _():
            for jj in range(nj):
                carry_ref[jj, :, :, :tn] = cin_ref[:, :, jj * tn:(jj + 1) * tn]
                carry_ref[jj, :, :, tn:] = cin_ref[:, :, D_FF + jj * tn:D_FF + (jj + 1) * tn]

        up_phase()

    @pl.when((j >= 1) & (j < nj))
    def _():
        gate_down_phase()
        up_phase()

    @pl.when(j == nj)
    def _():
        gate_down_phase()
        o_ref[...] = x_ref[...] + gt_ref[...] * (_rms(o_ref[...]) * gpost_ref[...])


def _ffn(x, mod, conv_in, g_pre2, g_post2, w_up, conv_w, conv_b, w_down, tm, nb):
    B, S, _ = x.shape
    R = mod.shape[1]
    tpb = S // tm
    assert S % tm == 0 and (R == 1 or tpb == 1) and (nb == 1 or tpb == 1)
    rm = R if R == 1 else tm
    tn = FFN_TN
    nj = D_FF // tn
    row = lambda i: (i // tpb, i % tpb)
    t_up = lambda j: jnp.minimum(j, nj - 1)
    t_gate = lambda j: jnp.maximum(j - 1, 0)
    conv_b2 = conv_b.reshape(1, 2 * D_FF)
    conv_out = jax.ShapeDtypeStruct((B * tpb, nb * (CONV_W - 1), 2 * D_FF), F32)
    conv_out_spec = pl.BlockSpec((None, nb * (CONV_W - 1), 2 * D_FF), lambda i, j: (i, 0, 0))
    up_rows = nb * (tm // nb + HALO)
    emit_w = not isinstance(w_up, tuple)
    assert not emit_w or (B * tpb == 1 and w_up.dtype == F32 and w_down.dtype == F32)
    w_v, w_g = (w_up, w_up) if emit_w else w_up
    gate_off = nj if emit_w else 0
    wv_spec = pl.BlockSpec((D_MODEL, tn), lambda i, j: (0, t_up(j)))
    wd_spec = pl.BlockSpec((tn, D_MODEL), lambda i, j: (t_gate(j), 0))
    half_shape = jax.ShapeDtypeStruct((D_MODEL, D_FF), BF16)
    outs = pl.pallas_call(
        functools.partial(_ffn_kernel, nb=nb, tpb=tpb, nj=nj),
        out_shape=(jax.ShapeDtypeStruct((B, S, D_MODEL), F32), conv_out)
        + ((half_shape, half_shape, jax.ShapeDtypeStruct(w_down.shape, BF16)) if emit_w else ()),
        grid=(B * tpb, nj + 1),
        in_specs=[pl.BlockSpec((None, tm, D_MODEL), lambda i, j: (*row(i), 0)),
                  pl.BlockSpec((None, rm, D_MODEL), lambda i, j: (i // tpb, 0, 3)),
                  pl.BlockSpec((None, rm, D_MODEL), lambda i, j: (i // tpb, 0, 4)),
                  pl.BlockSpec((None, rm, D_MODEL), lambda i, j: (i // tpb, 0, 5)),
                  pl.BlockSpec((1, D_MODEL), lambda i, j: (0, 0)),
                  pl.BlockSpec((1, D_MODEL), lambda i, j: (0, 0)),
                  wv_spec,
                  pl.BlockSpec((D_MODEL, tn), lambda i, j: (0, gate_off + t_up(j))),
                  _const_spec((CONV_W, 2 * D_FF)),
                  _const_spec((1, 2 * D_FF)),
                  wd_spec,
                  pl.BlockSpec((nb, CONV_W - 1, 2 * D_FF), lambda i, j: (i // tpb, 0, 0))],
        out_specs=(pl.BlockSpec((None, tm, D_MODEL), lambda i, j: (*row(i), 0)), conv_out_spec)
        + ((wv_spec, wv_spec, wd_spec) if emit_w else ()),
        scratch_shapes=[pltpu.VMEM((tm, D_MODEL), BF16),
                        pltpu.VMEM((up_rows, 2 * tn), F32),
                        pltpu.VMEM((tm, tn), BF16),
                        pltpu.VMEM((nj, nb, CONV_W - 1, 2 * tn), F32)],
        compiler_params=_params(("arbitrary", "arbitrary")),
        name="conv_ffn",
    )(x, mod, mod, mod, g_pre2.reshape(1, D_MODEL), g_post2.reshape(1, D_MODEL), w_v, w_g,
      conv_w, conv_b2, w_down, conv_in)
    weights_b = ((outs[2], outs[3]), outs[4]) if emit_w else (w_up, w_down)
    return outs[0], outs[1].reshape(B * tpb * nb, CONV_W - 1, 2 * D_FF), weights_b


def _rotary_tables(pos):
    half = RET_DK // 2
    inv = ROPE_BASE ** (-jnp.arange(half, dtype=F32) / half)
    ang = pos.astype(F32)[:, None] * inv[None, :]
    cos, sin = jnp.cos(ang), jnp.sin(ang)
    return jnp.concatenate([cos, cos], axis=-1), jnp.concatenate([-sin, sin], axis=-1)


def kernel(x_prompt, x_sample, cache_att_k, cache_att_v, state_ret, state_conv, c_prompt, c_sample,
           w_ada, b_ada, g_pre1, w_in, rel_bias, w_br_ret, w_br_att, w_out, g_post1, g_pre2,
           w_up, conv_w, conv_b, w_down, g_post2):
    depth = w_ada.shape[0]
    Bp, Sp, _ = x_prompt.shape
    Bs, Ls, _ = x_sample.shape
    log_gamma = jnp.log(1.0 - 2.0 ** (-5.0 - jnp.arange(N_HEADS, dtype=F32)))
    cos_p, sin_p = _rotary_tables(jnp.arange(Sp))
    cos_s, sin_s = _rotary_tables(PAST_LEN + jnp.arange(Ls))
    cos_s, sin_s = jnp.tile(cos_s, (Bs, 1)), jnp.tile(sin_s, (Bs, 1))
    keep = min(ATT_REACH, Sp)
    tm_p, tm_wide = 512, 1024

    yp = x_prompt
    ys = x_sample.reshape(1, Bs * Ls, D_MODEL)
    outs = [[] for _ in range(8)]
    for l in range(depth):
        w_br_ret_b, w_br_att_b, w_out_b = (w_br_ret[l].astype(BF16), w_br_att[l].astype(BF16),
                                           w_out[l].astype(BF16))
        mod = _mod(jnp.concatenate([c_prompt, c_sample], axis=0), w_ada[l], b_ada[l])
        mod_p = mod[:Bp].reshape(Bp, 1, 6 * D_MODEL)
        mod_s = jnp.repeat(mod[Bp:], Ls, axis=0).reshape(1, Bs * Ls, 6 * D_MODEL)

        proj_s, w_in_b = _in_proj(ys, mod_s, g_pre1[l], w_in[l], cos_s, sin_s, tm=Bs * Ls)
        proj_s3 = proj_s.reshape(Bs, Ls, IN_WIDTH)
        g_ret_s, r_s = _retention(proj_s3, state_ret[l], log_gamma, Ls, Ls)
        o_att_s = _attention_sample(proj_s3, cache_att_k, cache_att_v, l, rel_bias[l])
        x1_s = _merge(ys, g_ret_s.reshape(1, Bs * Ls, -1), o_att_s.reshape(1, Bs * Ls, -1), proj_s,
                      mod_s, g_post1[l], w_br_ret_b, w_br_att_b, w_out_b, tm=Bs * Ls)
        ys, cv_s, (w_up_b, w_down_b) = _ffn(x1_s, mod_s, state_conv[l], g_pre2[l], g_post2[l], w_up[l],
                                            conv_w[l], conv_b[l], w_down[l], tm=Bs * Ls, nb=Bs)
        k_s = proj_s3[:, :, OFF_AK:OFF_AV].astype(F32).reshape(Bs, Ls, N_HEADS, ATT_DH)
        v_s = proj_s3[:, :, OFF_AV:OFF_GR].astype(F32).reshape(Bs, Ls, N_HEADS, ATT_DH)

        proj = _in_proj(yp, mod_p, g_pre1[l], w_in_b, cos_p, sin_p, tm=tm_wide)
        zero_state = jnp.zeros((Bp, N_HEADS, RET_DK, RET_DV), F32)
        g_ret, r_p = _retention(proj, zero_state, log_gamma, RET_TILE, CHUNK)
        o_att = _attention_prompt(proj, rel_bias[l])
        x1 = _merge(yp, g_ret, o_att, proj, mod_p, g_post1[l], w_br_ret_b, w_br_att_b, w_out_b, tm=tm_p)
        conv0 = jnp.zeros((Bp, CONV_W - 1, 2 * D_FF), F32)
        yp, cv_p, _ = _ffn(x1, mod_p, conv0, g_pre2[l], g_post2[l], w_up_b, conv_w[l], conv_b[l],
                           w_down_b, tm=tm_wide, nb=1)
        cv_p = cv_p.reshape(Bp, Sp // tm_wide, CONV_W - 1, 2 * D_FF)[:, -1]
        k_p = proj[:, Sp - keep:, OFF_AK:OFF_AV].astype(F32).reshape(Bp, keep, N_HEADS, ATT_DH)
        v_p = proj[:, Sp - keep:, OFF_AV:OFF_GR].astype(F32).reshape(Bp, keep, N_HEADS, ATT_DH)

        for lst, val in zip(outs, (k_p, v_p, r_p, cv_p, k_s, v_s, r_s, cv_s)):
            lst.append(val)
    return (yp, ys.reshape(Bs, Ls, D_MODEL), *[jnp.stack(o) for o in outs])
```

```python
import functools
import math

import jax
import jax.numpy as jnp
from jax import lax
from jax.experimental import pallas as pl
from jax.experimental.pallas import tpu as pltpu

F32 = jnp.float32
BF16 = jnp.bfloat16

D_MODEL = 2048
PAST_LEN = 4096
CHUNK = 64
N_HEADS = 8
RET_DK = 128
RET_DV = 256
ATT_DH = 128
ATT_LEFT_CHUNKS = 8
ATT_REACH = ATT_LEFT_CHUNKS * CHUNK
REL_CLIP = 128
D_FF = 5632
CONV_W = 3
ROPE_BASE = 10000.0
EPS = 1e-6
IN_WIDTH = 13312
NEG = -1e30
LOG2E = math.log2(math.e)

OFF_RQ, OFF_RK, OFF_RV, OFF_RG = 0, 1024, 2048, 4096
OFF_AQ, OFF_AK, OFF_AV, OFF_GR, OFF_GA = 6144, 7168, 8192, 9216, 11264
QK_W = N_HEADS * RET_DK
RV_W = N_HEADS * RET_DV

LANE = 128
MXU_N = 256
VMEM_LIMIT = 62 * 1024 * 1024

PROJ_TN = 1024
RET_TILE = 256
ATT_TILE = 256
FFN_TN = 512
FFN_SUB = 128
ROW_CHUNK = 16


def _params(sem):
    return pltpu.CompilerParams(dimension_semantics=sem, vmem_limit_bytes=VMEM_LIMIT)


def _rms(x):
    return x * lax.rsqrt(jnp.mean(x * x, axis=-1, keepdims=True) + EPS)


def _rows(ref, r, n):
    return ref[...] if ref.shape[0] == 1 else ref[r:r + n, :]


def _norm_modulate(x_ref, g_ref, sc_ref, sh_ref, u_ref):
    for r in range(0, x_ref.shape[0], ROW_CHUNK):
        u = ((_rms(x_ref[r:r + ROW_CHUNK, :]) * g_ref[...]) * (1.0 + _rows(sc_ref, r, ROW_CHUNK))
             + _rows(sh_ref, r, ROW_CHUNK))
        u_ref[r:r + ROW_CHUNK, :] = u.astype(BF16)


def _sigmoid(x):
    return 1.0 / (1.0 + jnp.exp2(x * (-LOG2E)))


def _gelu_tanh(x):
    k = -2.0 * LOG2E * math.sqrt(2.0 / math.pi)
    return x / (1.0 + jnp.exp2(x * (k + (k * 0.044715) * (x * x))))


def _const_spec(shape):
    zeros = (0,) * len(shape)
    return pl.BlockSpec(shape, lambda *_: zeros, pipeline_mode=pl.Buffered(1))


def _mod_kernel(c_ref, w_ref, b_ref, o_ref):
    a = jax.nn.silu(c_ref[...]).astype(BF16)
    o_ref[...] = jnp.dot(a, w_ref[...].astype(BF16), preferred_element_type=F32) + b_ref[...]


def _mod(c, w_ada, b_ada):
    n, tn = c.shape[0], 1024
    width = w_ada.shape[1]
    return pl.pallas_call(
        _mod_kernel,
        out_shape=jax.ShapeDtypeStruct((n, width), F32),
        grid=(width // tn,),
        in_specs=[pl.BlockSpec((n, D_MODEL), lambda j: (0, 0)),
                  pl.BlockSpec((D_MODEL, tn), lambda j: (0, j)),
                  pl.BlockSpec((1, tn), lambda j: (0, j))],
        out_specs=pl.BlockSpec((n, tn), lambda j: (0, j)),
        compiler_params=_params(("arbitrary",)),
        name="adaln_mod",
    )(c, w_ada, b_ada.reshape(1, width))


def _inproj_kernel(x_ref, sh_ref, sc_ref, g_ref, w_ref, cos_ref, sin_ref, o_ref, *rest):
    u_ref = rest[-1]
    j = pl.program_id(1)

    @pl.when(j == 0)
    def _():
        _norm_modulate(x_ref, g_ref, sc_ref, sh_ref, u_ref)

    def tile(epilogue):
        w = 2 * MXU_N
        for c in range(PROJ_TN // w):
            cols = slice(c * w, (c + 1) * w)
            wc = w_ref[:, cols]
            if len(rest) == 2:
                wc = wc.astype(BF16)
                rest[0][:, cols] = wc
            acc = jnp.dot(u_ref[...], wc, preferred_element_type=F32)
            o_ref[:, cols] = epilogue(acc).astype(BF16)

    def rotary(acc):
        scale = jnp.where(j == OFF_RK // PROJ_TN, RET_DK ** -0.5, 1.0).astype(F32)
        cos, sin = cos_ref[...], sin_ref[...]
        heads = [acc[:, h * LANE:(h + 1) * LANE] for h in range(acc.shape[1] // LANE)]
        return jnp.concatenate(
            [(a * cos + pltpu.roll(a, LANE // 2, 1) * sin) * scale for a in heads], axis=-1)

    pl.when(j < OFF_RV // PROJ_TN)(lambda: tile(rotary))
    pl.when((j >= OFF_RG // PROJ_TN) & (j < OFF_AQ // PROJ_TN))(lambda: tile(lambda a: a * _sigmoid(a)))
    pl.when(j >= OFF_GR // PROJ_TN)(lambda: tile(_sigmoid))
    pl.when(((j >= OFF_RV // PROJ_TN) & (j < OFF_RG // PROJ_TN))
            | ((j >= OFF_AQ // PROJ_TN) & (j < OFF_GR // PROJ_TN)))(lambda: tile(lambda a: a))


def _in_proj(x, mod, g_pre1, w_in, cos, sin, tm):
    B, S, _ = x.shape
    R = mod.shape[1]
    tpb = S // tm
    assert S % tm == 0 and (R == 1 or tpb == 1)
    rm = R if R == 1 else tm
    emit_w = w_in.dtype == F32
    assert not emit_w or B * tpb == 1
    w_spec = pl.BlockSpec((D_MODEL, PROJ_TN), lambda i, j: (0, j))
    proj_shape = jax.ShapeDtypeStruct((B, S, IN_WIDTH), BF16)
    proj_spec = pl.BlockSpec((None, tm, PROJ_TN), lambda i, j: (i // tpb, i % tpb, j))
    return pl.pallas_call(
        _inproj_kernel,
        out_shape=(proj_shape, jax.ShapeDtypeStruct(w_in.shape, BF16)) if emit_w else proj_shape,
        grid=(B * tpb, IN_WIDTH // PROJ_TN),
        in_specs=[
            pl.BlockSpec((None, tm, D_MODEL), lambda i, j: (i // tpb, i % tpb, 0)),
            pl.BlockSpec((None, rm, D_MODEL), lambda i, j: (i // tpb, 0, 0)),
            pl.BlockSpec((None, rm, D_MODEL), lambda i, j: (i // tpb, 0, 1)),
            pl.BlockSpec((1, D_MODEL), lambda i, j: (0, 0)),
            w_spec,
            pl.BlockSpec((tm, LANE), lambda i, j: (i % tpb, 0)),
            pl.BlockSpec((tm, LANE), lambda i, j: (i % tpb, 0)),
        ],
        out_specs=(proj_spec, w_spec) if emit_w else proj_spec,
        scratch_shapes=[pltpu.VMEM((tm, D_MODEL), BF16)],
        compiler_params=_params(("arbitrary", "arbitrary")),
        name="in_proj",
    )(x, mod, mod, g_pre1.reshape(1, D_MODEL), w_in, cos, sin)


def _retention_kernel(ds_ref, q_ref, k_ref, v_ref, rg_ref, dmat_ref, dq_ref, dk_ref, s0_ref,
                      o_ref, snew_ref, s_ref):
    t = pl.program_id(1)

    @pl.when(t == 0)
    def _():
        s_ref[...] = s0_ref[...]

    def dots(h):
        q = q_ref[:, h * RET_DK:(h + 1) * RET_DK]
        k = k_ref[:, h * RET_DK:(h + 1) * RET_DK]
        v = v_ref[:, h * RET_DV:(h + 1) * RET_DV]
        state = s_ref[h]
        s = lax.dot_general(q, k, (((1,), (1,)), ((), ())), preferred_element_type=F32)
        qs = jnp.dot(q, state.astype(BF16), preferred_element_type=F32)
        kd = (k.astype(F32) * dk_ref[h]).astype(BF16)
        kv = lax.dot_general(kd, v, (((0,), (0,)), ((), ())), preferred_element_type=F32)
        return s, qs, kv, state, v

    def finish(h, s, qs, kv, state, v):
        o = (jnp.dot((s * dmat_ref[h]).astype(BF16), v, preferred_element_type=F32)
             + qs * dq_ref[h])
        new_state = ds_ref[h] * state + kv
        s_ref[h] = new_state
        cols = slice(h * RET_DV, (h + 1) * RET_DV)
        o_ref[:, cols] = (rg_ref[:, cols].astype(F32) * _rms(o)).astype(BF16)

    pending = dots(0)
    for h in range(1, N_HEADS):
        nxt = dots(h)
        finish(h - 1, *pending)
        pending = nxt
    finish(N_HEADS - 1, *pending)

    @pl.when(t == pl.num_programs(1) - 1)
    def _():
        snew_ref[...] = s_ref[...]


def _retention(proj, state0, log_gamma, tile, chunk, rel_bias=None):
    B, S, _ = proj.shape
    nt = S // tile
    pos = jnp.arange(tile)
    ch = pos // chunk
    dist = (pos[:, None] - pos[None, :]).astype(F32)
    lg = log_gamma[:, None, None]
    same = ch[:, None] == ch[None, :]
    past = ch[None, :] < ch[:, None]
    dmat = jnp.where(same[None], jnp.exp(lg * jnp.abs(dist)[None]),
                     jnp.where(past[None], jnp.exp(lg * dist[None]), 0.0))
    tf = pos.astype(F32)
    dq = jnp.broadcast_to(jnp.exp((tf[None, :] + 1.0) * log_gamma[:, None])[:, :, None],
                          (N_HEADS, tile, RET_DV))
    dk = jnp.broadcast_to(jnp.exp((tile - 1.0 - tf)[None, :] * log_gamma[:, None])[:, :, None],
                          (N_HEADS, tile, RET_DK))
    ds = jnp.exp(tile * log_gamma)
    state_spec = pl.BlockSpec((None, N_HEADS, RET_DK, RET_DV), lambda b, t: (b, 0, 0, 0))
    args = [ds, proj, proj, proj, proj, dmat, dq, dk, state0]
    in_specs = [
        pl.BlockSpec(memory_space=pltpu.SMEM),
        pl.BlockSpec((None, tile, QK_W), lambda b, t: (b, t, OFF_RQ // QK_W)),
        pl.BlockSpec((None, tile, QK_W), lambda b, t: (b, t, OFF_RK // QK_W)),
        pl.BlockSpec((None, tile, RV_W), lambda b, t: (b, t, OFF_RV // RV_W)),
        pl.BlockSpec((None, tile, RV_W), lambda b, t: (b, t, OFF_RG // RV_W)),
        _const_spec(dmat.shape), _const_spec(dq.shape), _const_spec(dk.shape),
        state_spec,
    ]
    out_shape = [jax.ShapeDtypeStruct((B, S, RV_W), BF16),
                 jax.ShapeDtypeStruct((B, N_HEADS, RET_DK, RET_DV), F32)]
    out_specs = [pl.BlockSpec((None, tile, RV_W), lambda b, t: (b, t, 0)), state_spec]
    body, name = _retention_kernel, "retention"
    if rel_bias is not None:
        assert tile == ATT_TILE and ATT_REACH // tile + 1 == 3
        table = _bias_table(rel_bias)

        def blk(off, back):
            return pl.BlockSpec((None, tile, QK_W),
                                lambda b, i: (b, jnp.maximum(i - back, 0), off // QK_W))

        args += [proj] * 7 + [table]
        in_specs += [blk(OFF_AQ, 0), blk(OFF_AK, 2), blk(OFF_AK, 1), blk(OFF_AK, 0),
                     blk(OFF_AV, 2), blk(OFF_AV, 1), blk(OFF_AV, 0), _const_spec(table.shape)]
        out_shape.append(jax.ShapeDtypeStruct((B, S, QK_W), BF16))
        out_specs.append(pl.BlockSpec((None, tile, QK_W), lambda b, i: (b, i, 0)))
        body, name = _token_mix_kernel, "token_mix"
    return pl.pallas_call(
        body,
        out_shape=tuple(out_shape),
        grid=(B, nt),
        in_specs=in_specs,
        out_specs=tuple(out_specs),
        scratch_shapes=[pltpu.VMEM((N_HEADS, RET_DK, RET_DV), F32)],
        compiler_params=_params(("arbitrary", "arbitrary")),
        name=name,
    )(*args)


def _qk(q, k):
    s = lax.dot_general(q, k, (((1,), (1,)), ((), ())), preferred_element_type=F32)
    return s * ((ATT_DH ** -0.5) * LOG2E)


def _softmax_pv(scores, values):
    m = functools.reduce(jnp.maximum, [jnp.max(s, axis=-1, keepdims=True) for s in scores])
    ps = [jnp.exp2(s - m) for s in scores]
    l = functools.reduce(jnp.add, [jnp.sum(p, axis=-1, keepdims=True) for p in ps])
    o = functools.reduce(jnp.add, [jnp.dot(p.astype(BF16), v, preferred_element_type=F32)
                                   for p, v in zip(ps, values)])
    return o / l


def _per_head_pipelined(scores_fn, finish_fn):
    pending = scores_fn(0)
    for h in range(1, N_HEADS):
        nxt = scores_fn(h)
        finish_fn(h - 1, pending)
        pending = nxt
    finish_fn(N_HEADS - 1, pending)


def _bias_table_kernel(base_ref, o_ref):
    T, W = o_ref.shape
    n = base_ref.shape[-1]
    x = jnp.broadcast_to(base_ref[...], (T, n))
    x = pltpu.roll(x, n - T + 1, 1, stride=1, stride_axis=0)[:, :W]
    qc = lax.broadcasted_iota(jnp.int32, (T, W), 0) // CHUNK
    kc = lax.broadcasted_iota(jnp.int32, (T, W), 1) // CHUNK
    o_ref[...] = jnp.where((kc >= qc) & (kc <= qc + ATT_LEFT_CHUNKS), x * LOG2E, NEG)


def _bias_table(rel_bias):
    T = ATT_TILE
    W = (ATT_REACH // T + 1) * T
    n = 1024
    assert W + T - 1 <= n
    lo = (T - 1) + ATT_REACH - REL_CLIP
    hi = n - lo - (2 * REL_CLIP + 1)
    assert hi >= 0
    base = jnp.concatenate([jnp.broadcast_to(rel_bias[:, :1], (N_HEADS, lo)), rel_bias,
                            jnp.broadcast_to(rel_bias[:, -1:], (N_HEADS, hi))], axis=1)
    return pl.pallas_call(
        _bias_table_kernel,
        out_shape=jax.ShapeDtypeStruct((N_HEADS, T, W), F32),
        grid=(N_HEADS,),
        in_specs=[pl.BlockSpec((None, 1, n), lambda h: (h, 0, 0))],
        out_specs=pl.BlockSpec((None, T, W), lambda h: (h, 0, 0)),
        compiler_params=_params(("arbitrary",)),
        name="bias_table",
    )(base.reshape(N_HEADS, 1, n).astype(F32))


def _att_prompt_kernel(q_ref, k0_ref, k1_ref, k2_ref, v0_ref, v1_ref, v2_ref, tab_ref, o_ref):
    i = pl.program_id(1)
    T = ATT_TILE

    def scores(h):
        cols = slice(h * ATT_DH, (h + 1) * ATT_DH)
        q = q_ref[:, cols]
        s0 = _qk(q, k0_ref[:, cols]) + tab_ref[h, :, 0:T]
        s1 = _qk(q, k1_ref[:, cols]) + tab_ref[h, :, T:2 * T]
        s2 = _qk(q, k2_ref[:, cols]) + tab_ref[h, :, 2 * T:3 * T]
        return [jnp.where(i >= 2, s0, NEG), jnp.where(i >= 1, s1, NEG), s2]

    def finish(h, ss):
        cols = slice(h * ATT_DH, (h + 1) * ATT_DH)
        o = _softmax_pv(ss, [v0_ref[:, cols], v1_ref[:, cols], v2_ref[:, cols]])
        o_ref[:, cols] = o.astype(BF16)

    _per_head_pipelined(scores, finish)


def _token_mix_kernel(*refs):
    ret_in, att_in = refs[:9], refs[9:17]
    o_ret_ref, snew_ref, o_att_ref, s_ref = refs[17:]
    _retention_kernel(*ret_in, o_ret_ref, snew_ref, s_ref)
    _att_prompt_kernel(*att_in, o_att_ref)


def _att_sample_kernel(q_ref, kn_ref, vn_ref, kc_ref, vc_ref, tabc_ref, tabn_ref, o_ref):
    L = q_ref.shape[0]
    head = lambda h: slice(h * ATT_DH, (h + 1) * ATT_DH)
    rows = lambda h: slice(h * L, (h + 1) * L)
    q = jnp.concatenate([q_ref[:, head(h)] for h in range(N_HEADS)], axis=0)
    sc = _qk(q, kc_ref[...].astype(BF16)) + tabc_ref[...]
    sn = jnp.concatenate([_qk(q_ref[:, head(h)], kn_ref[:, head(h)]) + tabn_ref[h]
                          for h in range(N_HEADS)], axis=0)
    m = jnp.maximum(jnp.max(sc, axis=-1, keepdims=True), jnp.max(sn, axis=-1, keepdims=True))
    pc, pn = jnp.exp2(sc - m), jnp.exp2(sn - m)
    l = jnp.sum(pc, axis=-1, keepdims=True) + jnp.sum(pn, axis=-1, keepdims=True)
    pn = pn.astype(BF16)
    o = jnp.dot(pc.astype(BF16), vc_ref[...].astype(BF16), preferred_element_type=F32)
    o = o + jnp.concatenate([jnp.dot(pn[rows(h)], vn_ref[:, head(h)], preferred_element_type=F32)
                             for h in range(N_HEADS)], axis=0)
    o = (o / l).astype(BF16)
    for h in range(N_HEADS):
        o_ref[:, head(h)] = o[rows(h)]


def _attention_sample(proj, cache_k, cache_v, layer, rel_bias):
    B, L, _ = proj.shape
    P = cache_k.shape[2]
    assert P == ATT_REACH and PAST_LEN >= ATT_REACH
    t = jnp.arange(L)
    rel = jnp.concatenate([jnp.arange(P)[None, :] - P - t[:, None], t[None, :] - t[:, None]], axis=1)
    table = rel_bias[:, jnp.clip(rel, -REL_CLIP, REL_CLIP) + REL_CLIP].astype(F32) * LOG2E
    same_head = jnp.eye(N_HEADS, dtype=bool)[:, None, None, :]
    table_c = jnp.where(same_head, table[:, :, :P, None], NEG).reshape(N_HEADS * L, P * N_HEADS)
    table_n = table[:, :, P:]
    depth = cache_k.shape[0]
    cache_k = cache_k.reshape(depth, B, P * N_HEADS, ATT_DH)
    cache_v = cache_v.reshape(depth, B, P * N_HEADS, ATT_DH)
    cache_spec = pl.BlockSpec((None, None, P * N_HEADS, ATT_DH), lambda b: (layer, b, 0, 0))
    return pl.pallas_call(
        _att_sample_kernel,
        out_shape=jax.ShapeDtypeStruct((B, L, QK_W), BF16),
        grid=(B,),
        in_specs=[pl.BlockSpec((None, L, QK_W), lambda b: (b, 0, OFF_AQ // QK_W)),
                  pl.BlockSpec((None, L, QK_W), lambda b: (b, 0, OFF_AK // QK_W)),
                  pl.BlockSpec((None, L, QK_W), lambda b: (b, 0, OFF_AV // QK_W)),
                  cache_spec, cache_spec,
                  _const_spec(table_c.shape), _const_spec(table_n.shape)],
        out_specs=pl.BlockSpec((None, L, QK_W), lambda b: (b, 0, 0)),
        compiler_params=_params(("arbitrary",)),
        name="attention_sample",
    )(proj, proj, proj, cache_k, cache_v, table_c, table_n)


def _merge_kernel(gret_ref, oatt_ref, gr0_ref, gr1_ref, ga0_ref, ga1_ref, x_ref, gt_ref, g_ref,
                  wr_ref, wa_ref, wo_ref, o_ref, m_ref):
    y_ret = jnp.dot(gret_ref[...], wr_ref[...], preferred_element_type=F32)
    y_att = jnp.dot(oatt_ref[...], wa_ref[...], preferred_element_type=F32)
    half = D_MODEL // 2
    m_ref[:, :half] = (gr0_ref[...].astype(F32) * y_ret[:, :half]
                       + ga0_ref[...].astype(F32) * y_att[:, :half]).astype(BF16)
    m_ref[:, half:] = (gr1_ref[...].astype(F32) * y_ret[:, half:]
                       + ga1_ref[...].astype(F32) * y_att[:, half:]).astype(BF16)
    z = jnp.dot(m_ref[...], wo_ref[...], preferred_element_type=F32)
    o_ref[...] = x_ref[...] + gt_ref[...] * (_rms(z) * g_ref[...])


def _merge(x, g_ret, o_att, proj, mod, g_post1, w_br_ret, w_br_att, w_out, tm):
    B, S, _ = x.shape
    R = mod.shape[1]
    tpb = S // tm
    assert S % tm == 0 and (R == 1 or tpb == 1)
    rm = R if R == 1 else tm
    half = D_MODEL // 2
    row = lambda i: (i // tpb, i % tpb)

    def gate_spec(off):
        return pl.BlockSpec((None, tm, half), lambda i: (*row(i), off // half))

    return pl.pallas_call(
        _merge_kernel,
        out_shape=jax.ShapeDtypeStruct((B, S, D_MODEL), F32),
        grid=(B * tpb,),
        in_specs=[pl.BlockSpec((None, tm, RV_W), lambda i: (*row(i), 0)),
                  pl.BlockSpec((None, tm, QK_W), lambda i: (*row(i), 0)),
                  gate_spec(OFF_GR), gate_spec(OFF_GR + half),
                  gate_spec(OFF_GA), gate_spec(OFF_GA + half),
                  pl.BlockSpec((None, tm, D_MODEL), lambda i: (*row(i), 0)),
                  pl.BlockSpec((None, rm, D_MODEL), lambda i: (i // tpb, 0, 2)),
                  pl.BlockSpec((1, D_MODEL), lambda i: (0, 0)),
                  _const_spec(w_br_ret.shape), _const_spec(w_br_att.shape), _const_spec(w_out.shape)],
        out_specs=pl.BlockSpec((None, tm, D_MODEL), lambda i: (*row(i), 0)),
        scratch_shapes=[pltpu.VMEM((tm, D_MODEL), BF16)],
        compiler_params=_params(("arbitrary",)),
        name="merge_out",
    )(g_ret, o_att, proj, proj, proj, proj, x, mod, g_post1.reshape(1, D_MODEL),
      w_br_ret, w_br_att, w_out)


HALO = 8


def _ffn_kernel(x_ref, sh_ref, sc_ref, gt_ref, gpre_ref, gpost_ref, wv_ref, wg_ref, cw_ref, cb_ref,
                wd_ref, cin_ref, o_ref, cout_ref, *rest, nb, tpb, nj):
    u_ref, up_ref, h_ref, carry_ref = rest[-4:]
    wvb_ref, wgb_ref, wdb_ref = rest[:-4] if len(rest) > 4 else (None, None, None)
    i, j = pl.program_id(0), pl.program_id(1)
    tn = FFN_TN
    L = x_ref.shape[0] // nb
    base = [s * (L + HALO) + HALO for s in range(nb)]
    rb = min(L, 64)

    def bf16_weights(w_ref, cast_ref, idx):
        w = w_ref[idx]
        if cast_ref is not None:
            w = w.astype(BF16)
            cast_ref[idx] = w
        return w

    def up_phase():
        for half, (w_ref, cast_ref) in enumerate(((wv_ref, wvb_ref), (wg_ref, wgb_ref))):
            acc = jnp.dot(u_ref[...], bf16_weights(w_ref, cast_ref, (slice(None), slice(None))),
                          preferred_element_type=F32)
            for s in range(nb):
                up_ref[base[s]:base[s] + L, half * tn:(half + 1) * tn] = acc[s * L:(s + 1) * L]

    def gate_down_phase():
        src = up_ref
        for s in range(nb):
            src[base[s] - 2:base[s], :] = carry_ref[j - 1, s]

        tile0 = pl.multiple_of((j - 1) * tn, tn)

        def conv(r0, cols, wcols):
            cur = src[r0:r0 + rb, cols]
            p1 = src[r0 - 1:r0 - 1 + rb, cols]
            p2 = src[r0 - 2:r0 - 2 + rb, cols]
            return (cb_ref[:, wcols] + cw_ref[2:3, wcols] * cur + cw_ref[1:2, wcols] * p1
                    + cw_ref[0:1, wcols] * p2)

        for p in range(tn // FFN_SUB):
            wc = slice(p * FFN_SUB, (p + 1) * FFN_SUB)
            cg = slice(tn + p * FFN_SUB, tn + (p + 1) * FFN_SUB)
            wv = pl.ds(tile0 + p * FFN_SUB, FFN_SUB)
            wg = pl.ds(tile0 + D_FF + p * FFN_SUB, FFN_SUB)
            for s in range(nb):
                for r in range(0, L, rb):
                    value = conv(base[s] + r, wc, wv)
                    gate = conv(base[s] + r, cg, wg)
                    h_ref[s * L + r:s * L + r + rb, wc] = (_gelu_tanh(gate) * value).astype(BF16)

        for s in range(nb):
            last = src[base[s] + L - 2:base[s] + L, :]
            carry_ref[j - 1, s] = last
            cout_ref[2 * s:2 * s + 2, pl.ds(tile0, tn)] = last[:, :tn]
            cout_ref[2 * s:2 * s + 2, pl.ds(tile0 + D_FF, tn)] = last[:, tn:]

        w = 512
        for n in range(D_MODEL // w):
            wd = bf16_weights(wd_ref, wdb_ref, (slice(None), slice(n * w, (n + 1) * w)))
            o_ref[:, n * w:(n + 1) * w] += jnp.dot(h_ref[...], wd, preferred_element_type=F32)

    @pl.when(j == 0)
    def _():
        _norm_modulate(x_ref, gpre_ref, sc_ref, sh_ref, u_ref)
        o_ref[...] = jnp.zeros(o_ref.shape, F32)

        @pl.when(i % tpb == 0)
        def _():
            for jj in range(nj):
                carry_ref[jj, :, :, :tn] = cin_ref[:, :, jj * tn:(jj + 1) * tn]
                carry_ref[jj, :, :, tn:] = cin_ref[:, :, D_FF + jj * tn:D_FF + (jj + 1) * tn]

        up_phase()

    @pl.when((j >= 1) & (j < nj))
    def _():
        gate_down_phase()
        up_phase()

    @pl.when(j == nj)
    def _():
        gate_down_phase()
        o_ref[...] = x_ref[...] + gt_ref[...] * (_rms(o_ref[...]) * gpost_ref[...])


def _ffn(x, mod, conv_in, g_pre2, g_post2, w_up, conv_w, conv_b, w_down, tm, nb):
    B, S, _ = x.shape
    R = mod.shape[1]
    tpb = S // tm
    assert S % tm == 0 and (R == 1 or tpb == 1) and (nb == 1 or tpb == 1)
    rm = R if R == 1 else tm
    tn = FFN_TN
    nj = D_FF // tn
    row = lambda i: (i // tpb, i % tpb)
    t_up = lambda j: jnp.minimum(j, nj - 1)
    t_gate = lambda j: jnp.maximum(j - 1, 0)
    conv_b2 = conv_b.reshape(1, 2 * D_FF)
    conv_out = jax.ShapeDtypeStruct((B * tpb, nb * (CONV_W - 1), 2 * D_FF), F32)
    conv_out_spec = pl.BlockSpec((None, nb * (CONV_W - 1), 2 * D_FF), lambda i, j: (i, 0, 0))
    up_rows = nb * (tm // nb + HALO)
    emit_w = not isinstance(w_up, tuple)
    assert not emit_w or (B * tpb == 1 and w_up.dtype == F32 and w_down.dtype == F32)
    w_v, w_g = (w_up, w_up) if emit_w else w_up
    gate_off = nj if emit_w else 0
    wv_spec = pl.BlockSpec((D_MODEL, tn), lambda i, j: (0, t_up(j)))
    wd_spec = pl.BlockSpec((tn, D_MODEL), lambda i, j: (t_gate(j), 0))
    half_shape = jax.ShapeDtypeStruct((D_MODEL, D_FF), BF16)
    outs = pl.pallas_call(
        functools.partial(_ffn_kernel, nb=nb, tpb=tpb, nj=nj),
        out_shape=(jax.ShapeDtypeStruct((B, S, D_MODEL), F32), conv_out)
        + ((half_shape, half_shape, jax.ShapeDtypeStruct(w_down.shape, BF16)) if emit_w else ()),
        grid=(B * tpb, nj + 1),
        in_specs=[pl.BlockSpec((None, tm, D_MODEL), lambda i, j: (*row(i), 0)),
                  pl.BlockSpec((None, rm, D_MODEL), lambda i, j: (i // tpb, 0, 3)),
                  pl.BlockSpec((None, rm, D_MODEL), lambda i, j: (i // tpb, 0, 4)),
                  pl.BlockSpec((None, rm, D_MODEL), lambda i, j: (i // tpb, 0, 5)),
                  pl.BlockSpec((1, D_MODEL), lambda i, j: (0, 0)),
                  pl.BlockSpec((1, D_MODEL), lambda i, j: (0, 0)),
                  wv_spec,
                  pl.BlockSpec((D_MODEL, tn), lambda i, j: (0, gate_off + t_up(j))),
                  _const_spec((CONV_W, 2 * D_FF)),
                  _const_spec((1, 2 * D_FF)),
                  wd_spec,
                  pl.BlockSpec((nb, CONV_W - 1, 2 * D_FF), lambda i, j: (i // tpb, 0, 0))],
        out_specs=(pl.BlockSpec((None, tm, D_MODEL), lambda i, j: (*row(i), 0)), conv_out_spec)
        + ((wv_spec, wv_spec, wd_spec) if emit_w else ()),
        scratch_shapes=[pltpu.VMEM((tm, D_MODEL), BF16),
                        pltpu.VMEM((up_rows, 2 * tn), F32),
                        pltpu.VMEM((tm, tn), BF16),
                        pltpu.VMEM((nj, nb, CONV_W - 1, 2 * tn), F32)],
        compiler_params=_params(("arbitrary", "arbitrary")),
        name="conv_ffn",
    )(x, mod, mod, mod, g_pre2.reshape(1, D_MODEL), g_post2.reshape(1, D_MODEL), w_v, w_g,
      conv_w, conv_b2, w_down, conv_in)
    weights_b = ((outs[2], outs[3]), outs[4]) if emit_w else (w_up, w_down)
    return outs[0], outs[1].reshape(B * tpb * nb, CONV_W - 1, 2 * D_FF), weights_b


def _rotary_tables(pos):
    half = RET_DK // 2
    inv = ROPE_BASE ** (-jnp.arange(half, dtype=F32) / half)
    ang = pos.astype(F32)[:, None] * inv[None, :]
    cos, sin = jnp.cos(ang), jnp.sin(ang)
    return jnp.concatenate([cos, cos], axis=-1), jnp.concatenate([-sin, sin], axis=-1)


def kernel(x_prompt, x_sample, cache_att_k, cache_att_v, state_ret, state_conv, c_prompt, c_sample,
           w_ada, b_ada, g_pre1, w_in, rel_bias, w_br_ret, w_br_att, w_out, g_post1, g_pre2,
           w_up, conv_w, conv_b, w_down, g_post2):
    depth = w_ada.shape[0]
    Bp, Sp, _ = x_prompt.shape
    Bs, Ls, _ = x_sample.shape
    log_gamma = jnp.log(1.0 - 2.0 ** (-5.0 - jnp.arange(N_HEADS, dtype=F32)))
    cos_p, sin_p = _rotary_tables(jnp.arange(Sp))
    cos_s, sin_s = _rotary_tables(PAST_LEN + jnp.arange(Ls))
    cos_s, sin_s = jnp.tile(cos_s, (Bs, 1)), jnp.tile(sin_s, (Bs, 1))
    keep = min(ATT_REACH, Sp)
    tm_p, tm_wide = 512, 1024

    yp = x_prompt
    ys = x_sample.reshape(1, Bs * Ls, D_MODEL)
    outs = [[] for _ in range(8)]
    for l in range(depth):
        w_br_ret_b, w_br_att_b, w_out_b = (w_br_ret[l].astype(BF16), w_br_att[l].astype(BF16),
                                           w_out[l].astype(BF16))
        mod = _mod(jnp.concatenate([c_prompt, c_sample], axis=0), w_ada[l], b_ada[l])
        mod_p = mod[:Bp].reshape(Bp, 1, 6 * D_MODEL)
        mod_s = jnp.repeat(mod[Bp:], Ls, axis=0).reshape(1, Bs * Ls, 6 * D_MODEL)

        proj_s, w_in_b = _in_proj(ys, mod_s, g_pre1[l], w_in[l], cos_s, sin_s, tm=Bs * Ls)
        proj_s3 = proj_s.reshape(Bs, Ls, IN_WIDTH)
        g_ret_s, r_s = _retention(proj_s3, state_ret[l], log_gamma, Ls, Ls)
        o_att_s = _attention_sample(proj_s3, cache_att_k, cache_att_v, l, rel_bias[l])
        x1_s = _merge(ys, g_ret_s.reshape(1, Bs * Ls, -1), o_att_s.reshape(1, Bs * Ls, -1), proj_s,
                      mod_s, g_post1[l], w_br_ret_b, w_br_att_b, w_out_b, tm=Bs * Ls)
        ys, cv_s, (w_up_b, w_down_b) = _ffn(x1_s, mod_s, state_conv[l], g_pre2[l], g_post2[l], w_up[l],
                                            conv_w[l], conv_b[l], w_down[l], tm=Bs * Ls, nb=Bs)
        k_s = proj_s3[:, :, OFF_AK:OFF_AV].astype(F32).reshape(Bs, Ls, N_HEADS, ATT_DH)
        v_s = proj_s3[:, :, OFF_AV:OFF_GR].astype(F32).reshape(Bs, Ls, N_HEADS, ATT_DH)

        proj = _in_proj(yp, mod_p, g_pre1[l], w_in_b, cos_p, sin_p, tm=tm_wide)
        zero_state = jnp.zeros((Bp, N_HEADS, RET_DK, RET_DV), F32)
        g_ret, r_p, o_att = _retention(proj, zero_state, log_gamma, RET_TILE, CHUNK, rel_bias[l])
        x1 = _merge(yp, g_ret, o_att, proj, mod_p, g_post1[l], w_br_ret_b, w_br_att_b, w_out_b, tm=tm_p)
        conv0 = jnp.zeros((Bp, CONV_W - 1, 2 * D_FF), F32)
        yp, cv_p, _ = _ffn(x1, mod_p, conv0, g_pre2[l], g_post2[l], w_up_b, conv_w[l], conv_b[l],
                           w_down_b, tm=tm_wide, nb=1)
        cv_p = cv_p.reshape(Bp, Sp // tm_wide, CONV_W - 1, 2 * D_FF)[:, -1]
        k_p = proj[:, Sp - keep:, OFF_AK:OFF_AV].astype(F32).reshape(Bp, keep, N_HEADS, ATT_DH)
        v_p = proj[:, Sp - keep:, OFF_AV:OFF_GR].astype(F32).reshape(Bp, keep, N_HEADS, ATT_DH)

        for lst, val in zip(outs, (k_p, v_p, r_p, cv_p, k_s, v_s, r_s, cv_s)):
            lst.append(val)
    return (yp, ys.reshape(Bs, Ls, D_MODEL), *[jnp.stack(o) for o in outs])
```

```python
import functools
import math

import jax
import jax.numpy as jnp
from jax import lax
from jax.experimental import pallas as pl
from jax.experimental.pallas import tpu as pltpu

F32 = jnp.float32
BF16 = jnp.bfloat16

D_MODEL = 2048
PAST_LEN = 4096
CHUNK = 64
N_HEADS = 8
RET_DK = 128
RET_DV = 256
ATT_DH = 128
ATT_LEFT_CHUNKS = 8
ATT_REACH = ATT_LEFT_CHUNKS * CHUNK
REL_CLIP = 128
D_FF = 5632
CONV_W = 3
ROPE_BASE = 10000.0
EPS = 1e-6
IN_WIDTH = 13312
NEG = -1e30
LOG2E = math.log2(math.e)

OFF_RQ, OFF_RK, OFF_RV, OFF_RG = 0, 1024, 2048, 4096
OFF_AQ, OFF_AK, OFF_AV, OFF_GR, OFF_GA = 6144, 7168, 8192, 9216, 11264
QK_W = N_HEADS * RET_DK
RV_W = N_HEADS * RET_DV

LANE = 128
MXU_N = 256
VMEM_LIMIT = 62 * 1024 * 1024

PROJ_TN = 1024
RET_TILE = 256
ATT_TILE = 256
FFN_TN = 512
FFN_SUB = 128
ROW_CHUNK = 16


def _params(sem):
    return pltpu.CompilerParams(dimension_semantics=sem, vmem_limit_bytes=VMEM_LIMIT)


def _rms(x):
    return x * lax.rsqrt(jnp.mean(x * x, axis=-1, keepdims=True) + EPS)


def _rows(ref, r, n):
    return ref[...] if ref.shape[0] == 1 else ref[r:r + n, :]


def _norm_modulate(x_ref, g_ref, sc_ref, sh_ref, u_ref):
    for r in range(0, x_ref.shape[0], ROW_CHUNK):
        u = ((_rms(x_ref[r:r + ROW_CHUNK, :]) * g_ref[...]) * (1.0 + _rows(sc_ref, r, ROW_CHUNK))
             + _rows(sh_ref, r, ROW_CHUNK))
        u_ref[r:r + ROW_CHUNK, :] = u.astype(BF16)


def _sigmoid(x):
    return 1.0 / (1.0 + jnp.exp2(x * (-LOG2E)))


def _gelu_tanh(x):
    k = -2.0 * LOG2E * math.sqrt(2.0 / math.pi)
    return x / (1.0 + jnp.exp2(x * (k + (k * 0.044715) * (x * x))))


def _const_spec(shape):
    zeros = (0,) * len(shape)
    return pl.BlockSpec(shape, lambda *_: zeros, pipeline_mode=pl.Buffered(1))


def _mod_kernel(c_ref, w_ref, b_ref, o_ref):
    a = jax.nn.silu(c_ref[...]).astype(BF16)
    o_ref[...] = jnp.dot(a, w_ref[...].astype(BF16), preferred_element_type=F32) + b_ref[...]


def _mod(c, w_ada, b_ada):
    n, tn = c.shape[0], 1024
    width = w_ada.shape[1]
    return pl.pallas_call(
        _mod_kernel,
        out_shape=jax.ShapeDtypeStruct((n, width), F32),
        grid=(width // tn,),
        in_specs=[pl.BlockSpec((n, D_MODEL), lambda j: (0, 0)),
                  pl.BlockSpec((D_MODEL, tn), lambda j: (0, j)),
                  pl.BlockSpec((1, tn), lambda j: (0, j))],
        out_specs=pl.BlockSpec((n, tn), lambda j: (0, j)),
        compiler_params=_params(("arbitrary",)),
        name="adaln_mod",
    )(c, w_ada, b_ada.reshape(1, width))


def _inproj_kernel(x_ref, sh_ref, sc_ref, g_ref, w_ref, cos_ref, sin_ref, o_ref, *rest):
    u_ref = rest[-1]
    j = pl.program_id(1)

    @pl.when(j == 0)
    def _():
        _norm_modulate(x_ref, g_ref, sc_ref, sh_ref, u_ref)

    def tile(epilogue):
        w = 2 * MXU_N
        for c in range(PROJ_TN // w):
            cols = slice(c * w, (c + 1) * w)
            wc = w_ref[:, cols]
            if len(rest) == 2:
                wc = wc.astype(BF16)
                rest[0][:, cols] = wc
            acc = jnp.dot(u_ref[...], wc, preferred_element_type=F32)
            o_ref[:, cols] = epilogue(acc).astype(BF16)

    def rotary(acc):
        scale = jnp.where(j == OFF_RK // PROJ_TN, RET_DK ** -0.5, 1.0).astype(F32)
        cos, sin = cos_ref[...], sin_ref[...]
        heads = [acc[:, h * LANE:(h + 1) * LANE] for h in range(acc.shape[1] // LANE)]
        return jnp.concatenate(
            [(a * cos + pltpu.roll(a, LANE // 2, 1) * sin) * scale for a in heads], axis=-1)

    pl.when(j < OFF_RV // PROJ_TN)(lambda: tile(rotary))
    pl.when((j >= OFF_RG // PROJ_TN) & (j < OFF_AQ // PROJ_TN))(lambda: tile(lambda a: a * _sigmoid(a)))
    pl.when(j >= OFF_GR // PROJ_TN)(lambda: tile(_sigmoid))
    pl.when(((j >= OFF_RV // PROJ_TN) & (j < OFF_RG // PROJ_TN))
            | ((j >= OFF_AQ // PROJ_TN) & (j < OFF_GR // PROJ_TN)))(lambda: tile(lambda a: a))


def _in_proj(x, mod, g_pre1, w_in, cos, sin, tm):
    B, S, _ = x.shape
    R = mod.shape[1]
    tpb = S // tm
    assert S % tm == 0 and (R == 1 or tpb == 1)
    rm = R if R == 1 else tm
    emit_w = w_in.dtype == F32
    assert not emit_w or B * tpb == 1
    w_spec = pl.BlockSpec((D_MODEL, PROJ_TN), lambda i, j: (0, j))
    proj_shape = jax.ShapeDtypeStruct((B, S, IN_WIDTH), BF16)
    proj_spec = pl.BlockSpec((None, tm, PROJ_TN), lambda i, j: (i // tpb, i % tpb, j))
    return pl.pallas_call(
        _inproj_kernel,
        out_shape=(proj_shape, jax.ShapeDtypeStruct(w_in.shape, BF16)) if emit_w else proj_shape,
        grid=(B * tpb, IN_WIDTH // PROJ_TN),
        in_specs=[
            pl.BlockSpec((None, tm, D_MODEL), lambda i, j: (i // tpb, i % tpb, 0)),
            pl.BlockSpec((None, rm, D_MODEL), lambda i, j: (i // tpb, 0, 0)),
            pl.BlockSpec((None, rm, D_MODEL), lambda i, j: (i // tpb, 0, 1)),
            pl.BlockSpec((1, D_MODEL), lambda i, j: (0, 0)),
            w_spec,
            pl.BlockSpec((tm, LANE), lambda i, j: (i % tpb, 0)),
            pl.BlockSpec((tm, LANE), lambda i, j: (i % tpb, 0)),
        ],
        out_specs=(proj_spec, w_spec) if emit_w else proj_spec,
        scratch_shapes=[pltpu.VMEM((tm, D_MODEL), BF16)],
        compiler_params=_params(("arbitrary", "arbitrary")),
        name="in_proj",
    )(x, mod, mod, g_pre1.reshape(1, D_MODEL), w_in, cos, sin)


def _retention_kernel(ds_ref, q_ref, k_ref, v_ref, rg_ref, dmat_ref, dq_ref, dk_ref, s0_ref,
                      o_ref, snew_ref, s_ref):
    t = pl.program_id(1)

    @pl.when(t == 0)
    def _():
        s_ref[...] = s0_ref[...]

    def dots(h):
        q = q_ref[:, h * RET_DK:(h + 1) * RET_DK]
        k = k_ref[:, h * RET_DK:(h + 1) * RET_DK]
        v = v_ref[:, h * RET_DV:(h + 1) * RET_DV]
        state = s_ref[h]
        s = lax.dot_general(q, k, (((1,), (1,)), ((), ())), preferred_element_type=F32)
        qs = jnp.dot(q, state.astype(BF16), preferred_element_type=F32)
        kd = (k.astype(F32) * dk_ref[h]).astype(BF16)
        kv = lax.dot_general(kd, v, (((0,), (0,)), ((), ())), preferred_element_type=F32)
        return s, qs, kv, state, v

    def finish(h, s, qs, kv, state, v):
        o = (jnp.dot((s * dmat_ref[h]).astype(BF16), v, preferred_element_type=F32)
             + qs * dq_ref[h])
        new_state = ds_ref[h] * state + kv
        s_ref[h] = new_state
        cols = slice(h * RET_DV, (h + 1) * RET_DV)
        o_ref[:, cols] = (rg_ref[:, cols].astype(F32) * _rms(o)).astype(BF16)

    pending = dots(0)
    for h in range(1, N_HEADS):
        nxt = dots(h)
        finish(h - 1, *pending)
        pending = nxt
    finish(N_HEADS - 1, *pending)

    @pl.when(t == pl.num_programs(1) - 1)
    def _():
        snew_ref[...] = s_ref[...]


def _retention(proj, state0, log_gamma, tile, chunk, rel_bias=None, sample_att=None):
    B, S, _ = proj.shape
    nt = S // tile
    pos = jnp.arange(tile)
    ch = pos // chunk
    dist = (pos[:, None] - pos[None, :]).astype(F32)
    lg = log_gamma[:, None, None]
    same = ch[:, None] == ch[None, :]
    past = ch[None, :] < ch[:, None]
    dmat = jnp.where(same[None], jnp.exp(lg * jnp.abs(dist)[None]),
                     jnp.where(past[None], jnp.exp(lg * dist[None]), 0.0))
    tf = pos.astype(F32)
    dq = jnp.broadcast_to(jnp.exp((tf[None, :] + 1.0) * log_gamma[:, None])[:, :, None],
                          (N_HEADS, tile, RET_DV))
    dk = jnp.broadcast_to(jnp.exp((tile - 1.0 - tf)[None, :] * log_gamma[:, None])[:, :, None],
                          (N_HEADS, tile, RET_DK))
    ds = jnp.exp(tile * log_gamma)
    state_spec = pl.BlockSpec((None, N_HEADS, RET_DK, RET_DV), lambda b, t: (b, 0, 0, 0))
    args = [ds, proj, proj, proj, proj, dmat, dq, dk, state0]
    in_specs = [
        pl.BlockSpec(memory_space=pltpu.SMEM),
        pl.BlockSpec((None, tile, QK_W), lambda b, t: (b, t, OFF_RQ // QK_W)),
        pl.BlockSpec((None, tile, QK_W), lambda b, t: (b, t, OFF_RK // QK_W)),
        pl.BlockSpec((None, tile, RV_W), lambda b, t: (b, t, OFF_RV // RV_W)),
        pl.BlockSpec((None, tile, RV_W), lambda b, t: (b, t, OFF_RG // RV_W)),
        _const_spec(dmat.shape), _const_spec(dq.shape), _const_spec(dk.shape),
        state_spec,
    ]
    out_shape = [jax.ShapeDtypeStruct((B, S, RV_W), BF16),
                 jax.ShapeDtypeStruct((B, N_HEADS, RET_DK, RET_DV), F32)]
    out_specs = [pl.BlockSpec((None, tile, RV_W), lambda b, t: (b, t, 0)), state_spec]
    body, name = _retention_kernel, "retention"
    if rel_bias is not None:
        assert tile == ATT_TILE and ATT_REACH // tile + 1 == 3
        table = _bias_table(rel_bias)

        def blk(off, back):
            return pl.BlockSpec((None, tile, QK_W),
                                lambda b, i: (b, jnp.maximum(i - back, 0), off // QK_W))

        args += [proj] * 7 + [table]
        in_specs += [blk(OFF_AQ, 0), blk(OFF_AK, 2), blk(OFF_AK, 1), blk(OFF_AK, 0),
                     blk(OFF_AV, 2), blk(OFF_AV, 1), blk(OFF_AV, 0), _const_spec(table.shape)]
        out_shape.append(jax.ShapeDtypeStruct((B, S, QK_W), BF16))
        out_specs.append(pl.BlockSpec((None, tile, QK_W), lambda b, i: (b, i, 0)))
        body, name = _token_mix_kernel, "token_mix"
    if sample_att is not None:
        assert nt == 1
        args += sample_att["args"]
        in_specs += sample_att["in_specs"]
        out_shape.append(sample_att["out_shape"])
        out_specs.append(sample_att["out_spec"])
        body, name = _token_mix_sample_kernel, "token_mix_sample"
    return pl.pallas_call(
        body,
        out_shape=tuple(out_shape),
        grid=(B, nt),
        in_specs=in_specs,
        out_specs=tuple(out_specs),
        scratch_shapes=[pltpu.VMEM((N_HEADS, RET_DK, RET_DV), F32)],
        compiler_params=_params(("arbitrary", "arbitrary")),
        name=name,
    )(*args)


def _qk(q, k):
    s = lax.dot_general(q, k, (((1,), (1,)), ((), ())), preferred_element_type=F32)
    return s * ((ATT_DH ** -0.5) * LOG2E)


def _softmax_pv(scores, values):
    m = functools.reduce(jnp.maximum, [jnp.max(s, axis=-1, keepdims=True) for s in scores])
    ps = [jnp.exp2(s - m) for s in scores]
    l = functools.reduce(jnp.add, [jnp.sum(p, axis=-1, keepdims=True) for p in ps])
    o = functools.reduce(jnp.add, [jnp.dot(p.astype(BF16), v, preferred_element_type=F32)
                                   for p, v in zip(ps, values)])
    return o / l


def _per_head_pipelined(scores_fn, finish_fn):
    pending = scores_fn(0)
    for h in range(1, N_HEADS):
        nxt = scores_fn(h)
        finish_fn(h - 1, pending)
        pending = nxt
    finish_fn(N_HEADS - 1, pending)


def _bias_table_kernel(base_ref, o_ref):
    T, W = o_ref.shape
    n = base_ref.shape[-1]
    x = jnp.broadcast_to(base_ref[...], (T, n))
    x = pltpu.roll(x, n - T + 1, 1, stride=1, stride_axis=0)[:, :W]
    qc = lax.broadcasted_iota(jnp.int32, (T, W), 0) // CHUNK
    kc = lax.broadcasted_iota(jnp.int32, (T, W), 1) // CHUNK
    o_ref[...] = jnp.where((kc >= qc) & (kc <= qc + ATT_LEFT_CHUNKS), x * LOG2E, NEG)


def _bias_table(rel_bias):
    T = ATT_TILE
    W = (ATT_REACH // T + 1) * T
    n = 1024
    assert W + T - 1 <= n
    lo = (T - 1) + ATT_REACH - REL_CLIP
    hi = n - lo - (2 * REL_CLIP + 1)
    assert hi >= 0
    base = jnp.concatenate([jnp.broadcast_to(rel_bias[:, :1], (N_HEADS, lo)), rel_bias,
                            jnp.broadcast_to(rel_bias[:, -1:], (N_HEADS, hi))], axis=1)
    return pl.pallas_call(
        _bias_table_kernel,
        out_shape=jax.ShapeDtypeStruct((N_HEADS, T, W), F32),
        grid=(N_HEADS,),
        in_specs=[pl.BlockSpec((None, 1, n), lambda h: (h, 0, 0))],
        out_specs=pl.BlockSpec((None, T, W), lambda h: (h, 0, 0)),
        compiler_params=_params(("arbitrary",)),
        name="bias_table",
    )(base.reshape(N_HEADS, 1, n).astype(F32))


def _att_prompt_kernel(q_ref, k0_ref, k1_ref, k2_ref, v0_ref, v1_ref, v2_ref, tab_ref, o_ref):
    i = pl.program_id(1)
    T = ATT_TILE

    def scores(h):
        cols = slice(h * ATT_DH, (h + 1) * ATT_DH)
        q = q_ref[:, cols]
        s0 = _qk(q, k0_ref[:, cols]) + tab_ref[h, :, 0:T]
        s1 = _qk(q, k1_ref[:, cols]) + tab_ref[h, :, T:2 * T]
        s2 = _qk(q, k2_ref[:, cols]) + tab_ref[h, :, 2 * T:3 * T]
        return [jnp.where(i >= 2, s0, NEG), jnp.where(i >= 1, s1, NEG), s2]

    def finish(h, ss):
        cols = slice(h * ATT_DH, (h + 1) * ATT_DH)
        o = _softmax_pv(ss, [v0_ref[:, cols], v1_ref[:, cols], v2_ref[:, cols]])
        o_ref[:, cols] = o.astype(BF16)

    _per_head_pipelined(scores, finish)


def _token_mix_kernel(*refs):
    ret_in, att_in = refs[:9], refs[9:17]
    o_ret_ref, snew_ref, o_att_ref, s_ref = refs[17:]
    _retention_kernel(*ret_in, o_ret_ref, snew_ref, s_ref)
    _att_prompt_kernel(*att_in, o_att_ref)


def _att_sample_kernel(q_ref, kn_ref, vn_ref, kc_ref, vc_ref, tabc_ref, tabn_ref, o_ref):
    L = q_ref.shape[0]
    head = lambda h: slice(h * ATT_DH, (h + 1) * ATT_DH)
    rows = lambda h: slice(h * L, (h + 1) * L)
    q = jnp.concatenate([q_ref[:, head(h)] for h in range(N_HEADS)], axis=0)
    sc = _qk(q, kc_ref[...].astype(BF16)) + tabc_ref[...]
    sn = jnp.concatenate([_qk(q_ref[:, head(h)], kn_ref[:, head(h)]) + tabn_ref[h]
                          for h in range(N_HEADS)], axis=0)
    m = jnp.maximum(jnp.max(sc, axis=-1, keepdims=True), jnp.max(sn, axis=-1, keepdims=True))
    pc, pn = jnp.exp2(sc - m), jnp.exp2(sn - m)
    l = jnp.sum(pc, axis=-1, keepdims=True) + jnp.sum(pn, axis=-1, keepdims=True)
    pn = pn.astype(BF16)
    o = jnp.dot(pc.astype(BF16), vc_ref[...].astype(BF16), preferred_element_type=F32)
    o = o + jnp.concatenate([jnp.dot(pn[rows(h)], vn_ref[:, head(h)], preferred_element_type=F32)
                             for h in range(N_HEADS)], axis=0)
    o = (o / l).astype(BF16)
    for h in range(N_HEADS):
        o_ref[:, head(h)] = o[rows(h)]


def _token_mix_sample_kernel(*refs):
    ret_in, att_in = refs[:9], refs[9:16]
    o_ret_ref, snew_ref, o_att_ref, s_ref = refs[16:]
    _retention_kernel(*ret_in, o_ret_ref, snew_ref, s_ref)
    _att_sample_kernel(*att_in, o_att_ref)


def _attention_sample(proj, cache_k, cache_v, layer, rel_bias):
    B, L, _ = proj.shape
    P = cache_k.shape[2]
    assert P == ATT_REACH and PAST_LEN >= ATT_REACH
    t = jnp.arange(L)
    rel = jnp.concatenate([jnp.arange(P)[None, :] - P - t[:, None], t[None, :] - t[:, None]], axis=1)
    table = rel_bias[:, jnp.clip(rel, -REL_CLIP, REL_CLIP) + REL_CLIP].astype(F32) * LOG2E
    same_head = jnp.eye(N_HEADS, dtype=bool)[:, None, None, :]
    table_c = jnp.where(same_head, table[:, :, :P, None], NEG).reshape(N_HEADS * L, P * N_HEADS)
    table_n = table[:, :, P:]
    depth = cache_k.shape[0]
    cache_k = cache_k.reshape(depth, B, P * N_HEADS, ATT_DH)
    cache_v = cache_v.reshape(depth, B, P * N_HEADS, ATT_DH)
    cache_spec = pl.BlockSpec((None, None, P * N_HEADS, ATT_DH), lambda b, t: (layer, b, 0, 0))
    return dict(
        args=[proj, proj, proj, cache_k, cache_v, table_c, table_n],
        in_specs=[pl.BlockSpec((None, L, QK_W), lambda b, t: (b, 0, OFF_AQ // QK_W)),
                  pl.BlockSpec((None, L, QK_W), lambda b, t: (b, 0, OFF_AK // QK_W)),
                  pl.BlockSpec((None, L, QK_W), lambda b, t: (b, 0, OFF_AV // QK_W)),
                  cache_spec, cache_spec,
                  _const_spec(table_c.shape), _const_spec(table_n.shape)],
        out_shape=jax.ShapeDtypeStruct((B, L, QK_W), BF16),
        out_spec=pl.BlockSpec((None, L, QK_W), lambda b, t: (b, 0, 0)))


def _merge_kernel(gret_ref, oatt_ref, gr0_ref, gr1_ref, ga0_ref, ga1_ref, x_ref, gt_ref, g_ref,
                  wr_ref, wa_ref, wo_ref, o_ref, m_ref):
    y_ret = jnp.dot(gret_ref[...], wr_ref[...], preferred_element_type=F32)
    y_att = jnp.dot(oatt_ref[...], wa_ref[...], preferred_element_type=F32)
    half = D_MODEL // 2
    m_ref[:, :half] = (gr0_ref[...].astype(F32) * y_ret[:, :half]
                       + ga0_ref[...].astype(F32) * y_att[:, :half]).astype(BF16)
    m_ref[:, half:] = (gr1_ref[...].astype(F32) * y_ret[:, half:]
                       + ga1_ref[...].astype(F32) * y_att[:, half:]).astype(BF16)
    z = jnp.dot(m_ref[...], wo_ref[...], preferred_element_type=F32)
    o_ref[...] = x_ref[...] + gt_ref[...] * (_rms(z) * g_ref[...])


def _merge(x, g_ret, o_att, proj, mod, g_post1, w_br_ret, w_br_att, w_out, tm):
    B, S, _ = x.shape
    R = mod.shape[1]
    tpb = S // tm
    assert S % tm == 0 and (R == 1 or tpb == 1)
    rm = R if R == 1 else tm
    half = D_MODEL // 2
    row = lambda i: (i // tpb, i % tpb)

    def gate_spec(off):
        return pl.BlockSpec((None, tm, half), lambda i: (*row(i), off // half))

    return pl.pallas_call(
        _merge_kernel,
        out_shape=jax.ShapeDtypeStruct((B, S, D_MODEL), F32),
        grid=(B * tpb,),
        in_specs=[pl.BlockSpec((None, tm, RV_W), lambda i: (*row(i), 0)),
                  pl.BlockSpec((None, tm, QK_W), lambda i: (*row(i), 0)),
                  gate_spec(OFF_GR), gate_spec(OFF_GR + half),
                  gate_spec(OFF_GA), gate_spec(OFF_GA + half),
                  pl.BlockSpec((None, tm, D_MODEL), lambda i: (*row(i), 0)),
                  pl.BlockSpec((None, rm, D_MODEL), lambda i: (i // tpb, 0, 2)),
                  pl.BlockSpec((1, D_MODEL), lambda i: (0, 0)),
                  _const_spec(w_br_ret.shape), _const_spec(w_br_att.shape), _const_spec(w_out.shape)],
        out_specs=pl.BlockSpec((None, tm, D_MODEL), lambda i: (*row(i), 0)),
        scratch_shapes=[pltpu.VMEM((tm, D_MODEL), BF16)],
        compiler_params=_params(("arbitrary",)),
        name="merge_out",
    )(g_ret, o_att, proj, proj, proj, proj, x, mod, g_post1.reshape(1, D_MODEL),
      w_br_ret, w_br_att, w_out)


HALO = 8


def _ffn_kernel(x_ref, sh_ref, sc_ref, gt_ref, gpre_ref, gpost_ref, wv_ref, wg_ref, cw_ref, cb_ref,
                wd_ref, cin_ref, o_ref, cout_ref, *rest, nb, tpb, nj):
    u_ref, up_ref, h_ref, carry_ref = rest[-4:]
    wvb_ref, wgb_ref, wdb_ref = rest[:-4] if len(rest) > 4 else (None, None, None)
    i, j = pl.program_id(0), pl.program_id(1)
    tn = FFN_TN
    L = x_ref.shape[0] // nb
    base = [s * (L + HALO) + HALO for s in range(nb)]
    rb = min(L, 64)

    def bf16_weights(w_ref, cast_ref, idx):
        w = w_ref[idx]
        if cast_ref is not None:
            w = w.astype(BF16)
            cast_ref[idx] = w
        return w

    def up_phase():
        for half, (w_ref, cast_ref) in enumerate(((wv_ref, wvb_ref), (wg_ref, wgb_ref))):
            acc = jnp.dot(u_ref[...], bf16_weights(w_ref, cast_ref, (slice(None), slice(None))),
                          preferred_element_type=F32)
            for s in range(nb):
                up_ref[base[s]:base[s] + L, half * tn:(half + 1) * tn] = acc[s * L:(s + 1) * L]

    def gate_down_phase():
        src = up_ref
        for s in range(nb):
            src[base[s] - 2:base[s], :] = carry_ref[j - 1, s]

        tile0 = pl.multiple_of((j - 1) * tn, tn)

        def conv(r0, cols, wcols):
            cur = src[r0:r0 + rb, cols]
            p1 = src[r0 - 1:r0 - 1 + rb, cols]
            p2 = src[r0 - 2:r0 - 2 + rb, cols]
            return (cb_ref[:, wcols] + cw_ref[2:3, wcols] * cur + cw_ref[1:2, wcols] * p1
                    + cw_ref[0:1, wcols] * p2)

        for p in range(tn // FFN_SUB):
            wc = slice(p * FFN_SUB, (p + 1) * FFN_SUB)
            cg = slice(tn + p * FFN_SUB, tn + (p + 1) * FFN_SUB)
            wv = pl.ds(tile0 + p * FFN_SUB, FFN_SUB)
            wg = pl.ds(tile0 + D_FF + p * FFN_SUB, FFN_SUB)
            for s in range(nb):
                for r in range(0, L, rb):
                    value = conv(base[s] + r, wc, wv)
                    gate = conv(base[s] + r, cg, wg)
                    h_ref[s * L + r:s * L + r + rb, wc] = (_gelu_tanh(gate) * value).astype(BF16)

        for s in range(nb):
            last = src[base[s] + L - 2:base[s] + L, :]
            carry_ref[j - 1, s] = last
            cout_ref[2 * s:2 * s + 2, pl.ds(tile0, tn)] = last[:, :tn]
            cout_ref[2 * s:2 * s + 2, pl.ds(tile0 + D_FF, tn)] = last[:, tn:]

        w = 512
        for n in range(D_MODEL // w):
            wd = bf16_weights(wd_ref, wdb_ref, (slice(None), slice(n * w, (n + 1) * w)))
            o_ref[:, n * w:(n + 1) * w] += jnp.dot(h_ref[...], wd, preferred_element_type=F32)

    @pl.when(j == 0)
    def _():
        _norm_modulate(x_ref, gpre_ref, sc_ref, sh_ref, u_ref)
        o_ref[...] = jnp.zeros(o_ref.shape, F32)

        @pl.when(i % tpb == 0)
        def _():
            for jj in range(nj):
                carry_ref[jj, :, :, :tn] = cin_ref[:, :, jj * tn:(jj + 1) * tn]
                carry_ref[jj, :, :, tn:] = cin_ref[:, :, D_FF + jj * tn:D_FF + (jj + 1) * tn]

        up_phase()

    @pl.when((j >= 1) & (j < nj))
    def _():
        gate_down_phase()
        up_phase()

    @pl.when(j == nj)
    def _():
        gate_down_phase()
        o_ref[...] = x_ref[...] + gt_ref[...] * (_rms(o_ref[...]) * gpost_ref[...])


def _ffn(x, mod, conv_in, g_pre2, g_post2, w_up, conv_w, conv_b, w_down, tm, nb):
    B, S, _ = x.shape
    R = mod.shape[1]
    tpb = S // tm
    assert S % tm == 0 and (R == 1 or tpb == 1) and (nb == 1 or tpb == 1)
    rm = R if R == 1 else tm
    tn = FFN_TN
    nj = D_FF // tn
    row = lambda i: (i // tpb, i % tpb)
    t_up = lambda j: jnp.minimum(j, nj - 1)
    t_gate = lambda j: jnp.maximum(j - 1, 0)
    conv_b2 = conv_b.reshape(1, 2 * D_FF)
    conv_out = jax.ShapeDtypeStruct((B * tpb, nb * (CONV_W - 1), 2 * D_FF), F32)
    conv_out_spec = pl.BlockSpec((None, nb * (CONV_W - 1), 2 * D_FF), lambda i, j: (i, 0, 0))
    up_rows = nb * (tm // nb + HALO)
    emit_w = not isinstance(w_up, tuple)
    assert not emit_w or (B * tpb == 1 and w_up.dtype == F32 and w_down.dtype == F32)
    w_v, w_g = (w_up, w_up) if emit_w else w_up
    gate_off = nj if emit_w else 0
    wv_spec = pl.BlockSpec((D_MODEL, tn), lambda i, j: (0, t_up(j)))
    wd_spec = pl.BlockSpec((tn, D_MODEL), lambda i, j: (t_gate(j), 0))
    half_shape = jax.ShapeDtypeStruct((D_MODEL, D_FF), BF16)
    outs = pl.pallas_call(
        functools.partial(_ffn_kernel, nb=nb, tpb=tpb, nj=nj),
        out_shape=(jax.ShapeDtypeStruct((B, S, D_MODEL), F32), conv_out)
        + ((half_shape, half_shape, jax.ShapeDtypeStruct(w_down.shape, BF16)) if emit_w else ()),
        grid=(B * tpb, nj + 1),
        in_specs=[pl.BlockSpec((None, tm, D_MODEL), lambda i, j: (*row(i), 0)),
                  pl.BlockSpec((None, rm, D_MODEL), lambda i, j: (i // tpb, 0, 3)),
                  pl.BlockSpec((None, rm, D_MODEL), lambda i, j: (i // tpb, 0, 4)),
                  pl.BlockSpec((None, rm, D_MODEL), lambda i, j: (i // tpb, 0, 5)),
                  pl.BlockSpec((1, D_MODEL), lambda i, j: (0, 0)),
                  pl.BlockSpec((1, D_MODEL), lambda i, j: (0, 0)),
                  wv_spec,
                  pl.BlockSpec((D_MODEL, tn), lambda i, j: (0, gate_off + t_up(j))),
                  _const_spec((CONV_W, 2 * D_FF)),
                  _const_spec((1, 2 * D_FF)),
                  wd_spec,
                  pl.BlockSpec((nb, CONV_W - 1, 2 * D_FF), lambda i, j: (i // tpb, 0, 0))],
        out_specs=(pl.BlockSpec((None, tm, D_MODEL), lambda i, j: (*row(i), 0)), conv_out_spec)
        + ((wv_spec, wv_spec, wd_spec) if emit_w else ()),
        scratch_shapes=[pltpu.VMEM((tm, D_MODEL), BF16),
                        pltpu.VMEM((up_rows, 2 * tn), F32),
                        pltpu.VMEM((tm, tn), BF16),
                        pltpu.VMEM((nj, nb, CONV_W - 1, 2 * tn), F32)],
        compiler_params=_params(("arbitrary", "arbitrary")),
        name="conv_ffn",
    )(x, mod, mod, mod, g_pre2.reshape(1, D_MODEL), g_post2.reshape(1, D_MODEL), w_v, w_g,
      conv_w, conv_b2, w_down, conv_in)
    weights_b = ((outs[2], outs[3]), outs[4]) if emit_w else (w_up, w_down)
    return outs[0], outs[1].reshape(B * tpb * nb, CONV_W - 1, 2 * D_FF), weights_b


def _rotary_tables(pos):
    half = RET_DK // 2
    inv = ROPE_BASE ** (-jnp.arange(half, dtype=F32) / half)
    ang = pos.astype(F32)[:, None] * inv[None, :]
    cos, sin = jnp.cos(ang), jnp.sin(ang)
    return jnp.concatenate([cos, cos], axis=-1), jnp.concatenate([-sin, sin], axis=-1)


def kernel(x_prompt, x_sample, cache_att_k, cache_att_v, state_ret, state_conv, c_prompt, c_sample,
           w_ada, b_ada, g_pre1, w_in, rel_bias, w_br_ret, w_br_att, w_out, g_post1, g_pre2,
           w_up, conv_w, conv_b, w_down, g_post2):
    depth = w_ada.shape[0]
    Bp, Sp, _ = x_prompt.shape
    Bs, Ls, _ = x_sample.shape
    log_gamma = jnp.log(1.0 - 2.0 ** (-5.0 - jnp.arange(N_HEADS, dtype=F32)))
    cos_p, sin_p = _rotary_tables(jnp.arange(Sp))
    cos_s, sin_s = _rotary_tables(PAST_LEN + jnp.arange(Ls))
    cos_s, sin_s = jnp.tile(cos_s, (Bs, 1)), jnp.tile(sin_s, (Bs, 1))
    keep = min(ATT_REACH, Sp)
    tm_p, tm_wide = 512, 1024

    yp = x_prompt
    ys = x_sample.reshape(1, Bs * Ls, D_MODEL)
    outs = [[] for _ in range(8)]
    for l in range(depth):
        w_br_ret_b, w_br_att_b, w_out_b = (w_br_ret[l].astype(BF16), w_br_att[l].astype(BF16),
                                           w_out[l].astype(BF16))
        mod = _mod(jnp.concatenate([c_prompt, c_sample], axis=0), w_ada[l], b_ada[l])
        mod_p = mod[:Bp].reshape(Bp, 1, 6 * D_MODEL)
        mod_s = jnp.repeat(mod[Bp:], Ls, axis=0).reshape(1, Bs * Ls, 6 * D_MODEL)

        proj_s, w_in_b = _in_proj(ys, mod_s, g_pre1[l], w_in[l], cos_s, sin_s, tm=Bs * Ls)
        proj_s3 = proj_s.reshape(Bs, Ls, IN_WIDTH)
        g_ret_s, r_s, o_att_s = _retention(
            proj_s3, state_ret[l], log_gamma, Ls, Ls,
            sample_att=_attention_sample(proj_s3, cache_att_k, cache_att_v, l, rel_bias[l]))
        x1_s = _merge(ys, g_ret_s.reshape(1, Bs * Ls, -1), o_att_s.reshape(1, Bs * Ls, -1), proj_s,
                      mod_s, g_post1[l], w_br_ret_b, w_br_att_b, w_out_b, tm=Bs * Ls)
        ys, cv_s, (w_up_b, w_down_b) = _ffn(x1_s, mod_s, state_conv[l], g_pre2[l], g_post2[l], w_up[l],
                                            conv_w[l], conv_b[l], w_down[l], tm=Bs * Ls, nb=Bs)
        k_s = proj_s3[:, :, OFF_AK:OFF_AV].astype(F32).reshape(Bs, Ls, N_HEADS, ATT_DH)
        v_s = proj_s3[:, :, OFF_AV:OFF_GR].astype(F32).reshape(Bs, Ls, N_HEADS, ATT_DH)

        proj = _in_proj(yp, mod_p, g_pre1[l], w_in_b, cos_p, sin_p, tm=tm_wide)
        zero_state = jnp.zeros((Bp, N_HEADS, RET_DK, RET_DV), F32)
        g_ret, r_p, o_att = _retention(proj, zero_state, log_gamma, RET_TILE, CHUNK, rel_bias[l])
        x1 = _merge(yp, g_ret, o_att, proj, mod_p, g_post1[l], w_br_ret_b, w_br_att_b, w_out_b, tm=tm_p)
        conv0 = jnp.zeros((Bp, CONV_W - 1, 2 * D_FF), F32)
        yp, cv_p, _ = _ffn(x1, mod_p, conv0, g_pre2[l], g_post2[l], w_up_b, conv_w[l], conv_b[l],
                           w_down_b, tm=tm_wide, nb=1)
        cv_p = cv_p.reshape(Bp, Sp // tm_wide, CONV_W - 1, 2 * D_FF)[:, -1]
        k_p = proj[:, Sp - keep:, OFF_AK:OFF_AV].astype(F32).reshape(Bp, keep, N_HEADS, ATT_DH)
        v_p = proj[:, Sp - keep:, OFF_AV:OFF_GR].astype(F32).reshape(Bp, keep, N_HEADS, ATT_DH)

        for lst, val in zip(outs, (k_p, v_p, r_p, cv_p, k_s, v_s, r_s, cv_s)):
            lst.append(val)
    return (yp, ys.reshape(Bs, Ls, D_MODEL), *[jnp.stack(o) for o in outs])
```
